```python
import math
import jax, jax.numpy as jnp
from jax import lax
import numpy as np

D_MODEL = 1024
BATCH = 2
SEQ = 8192
DEPTH = 4

N_MIXERS = 4
DN_ALPHA = (2.0 * DEPTH) ** 0.25
DN_BETA = (8.0 * DEPTH) ** -0.25
LN_EPS = 1e-5

GDN_HEADS = 8
GDN_DK = D_MODEL // GDN_HEADS
GDN_DV = D_MODEL // GDN_HEADS
GDN_CONV = 4
GDN_CHUNK = 64
RET_HEADS = 4
RET_DK = D_MODEL // RET_HEADS
RET_DV = 2 * D_MODEL // RET_HEADS
RET_CHUNK = 128
RET_ROPE_BASE = 10000.0
GMLP_CHUNK = 128
GMLP_WIDTH = 2 * D_MODEL
GMLP_GROUPS = 8
SB_HEADS = 16
SB_DH = D_MODEL // SB_HEADS
SB_BLOCK = 128
FFN_HIDDEN = ((8 * D_MODEL // 3 + 255) // 256) * 256
FFN_CONV = 3

kernel_name = 'hybrid_interleaved_gdn_ret_gmlp_sb_trunk'

F32 = jnp.float32


def _standardize(x, eps):
    xf = x.astype(F32)
    mu = jnp.mean(xf, axis=-1, keepdims=True)
    xc = xf - mu
    var = jnp.mean(xc * xc, axis=-1, keepdims=True)
    return xc * lax.rsqrt(var + eps)


def layer_norm(x, g, b):
    return (_standardize(x, LN_EPS) * g.astype(F32) + b.astype(F32)).astype(x.dtype)


def _l2norm(x, eps=1e-6):
    return x * lax.rsqrt(jnp.sum(x * x, axis=-1, keepdims=True) + eps)


def causal_dwconv(x, w):
    k_w = w.shape[0]
    s = x.shape[1]
    xp = jnp.pad(x, ((0, 0), (k_w - 1, 0), (0, 0)))
    y = xp[:, k_w - 1:k_w - 1 + s] * w[k_w - 1]
    for j in range(k_w - 1):
        y = y + xp[:, j:j + s] * w[j]
    return y


def _chunk_heads(t, n_heads, chunk):
    b, s, hd = t.shape
    return t.astype(F32).reshape(b, s // chunk, chunk, n_heads, hd // n_heads).transpose(0, 3, 1, 2, 4)


def _unchunk_heads(t):
    b, h, n, c, d = t.shape
    return t.transpose(0, 2, 3, 1, 4).reshape(b, n * c, h, d)


def gated_deltanet(h, w_in, conv_w, a_log, dt_bias, norm_w, w_out):
    H, dk, dv, C = GDN_HEADS, GDN_DK, GDN_DV, GDN_CHUNK
    b_, s, _ = h.shape
    n_qkv = 2 * H * dk + H * dv
    proj = h @ w_in
    qkv, z, a, bt = jnp.split(proj, [n_qkv, n_qkv + H * dv, n_qkv + H * dv + H], axis=-1)
    qkv = jax.nn.silu(causal_dwconv(qkv, conv_w))
    q, k, v = jnp.split(qkv, [H * dk, 2 * H * dk], axis=-1)
    q = _l2norm(_chunk_heads(q, H, C)) * (dk ** -0.5)
    k = _l2norm(_chunk_heads(k, H, C))
    v = _chunk_heads(v, H, C)
    beta = jax.nn.sigmoid(_chunk_heads(bt, H, C)[..., 0])
    g = -jnp.exp(a_log.astype(F32))[:, None, None] * jax.nn.softplus(
        _chunk_heads(a, H, C)[..., 0] + dt_bias.astype(F32)[:, None, None])
    gc = jnp.cumsum(g, axis=-1)
    idx = jnp.arange(C)
    causal = idx[:, None] >= idx[None, :]
    strict = idx[:, None] > idx[None, :]
    diff = gc[..., :, None] - gc[..., None, :]
    decay = jnp.where(causal, jnp.exp(jnp.where(causal, diff, 0.0)), 0.0)
    kb = k * beta[..., None]
    kk = jnp.where(strict, jnp.einsum('bhncd,bhnmd->bhncm', kb, k) * decay, 0.0)
    eye = jnp.eye(C, dtype=F32)
    rhs = jnp.concatenate([v * beta[..., None], kb * jnp.exp(gc)[..., None]], axis=-1)
    sol = lax.linalg.triangular_solve(kk + eye, rhs, left_side=True, lower=True, unit_diagonal=True)
    u, w = sol[..., :dv], sol[..., dv:]
    qk = jnp.where(causal, jnp.einsum('bhncd,bhnmd->bhncm', q, k) * decay, 0.0)

    def step(state, inp):
        q_n, k_n, u_n, w_n, qk_n, g_n = inp
        v_new = u_n - jnp.einsum('bhck,bhkv->bhcv', w_n, state)
        o = (jnp.einsum('bhck,bhkv->bhcv', q_n * jnp.exp(g_n)[..., None], state)
             + jnp.einsum('bhcm,bhmv->bhcv', qk_n, v_new))
        g_last = g_n[..., -1:]
        state = (state * jnp.exp(g_last)[..., None]
                 + jnp.einsum('bhck,bhcv->bhkv', k_n * jnp.exp(g_last - g_n)[..., None], v_new))
        return state, o

    xs = tuple(jnp.moveaxis(t, 2, 0) for t in (q, k, u, w, qk, gc))
    state0 = jnp.zeros((b_, H, dk, dv), F32)
    _, o = lax.scan(step, state0, xs)
    o = _unchunk_heads(jnp.moveaxis(o, 0, 2))
    o = o * lax.rsqrt(jnp.mean(o * o, axis=-1, keepdims=True) + 1e-6) * norm_w.astype(F32)
    o = o * jax.nn.silu(z.astype(F32).reshape(b_, s, H, dv))
    return o.reshape(b_, s, H * dv).astype(h.dtype) @ w_out


def retention(h, w_in, w_out):
    H, dk, dv, C = RET_HEADS, RET_DK, RET_DV, RET_CHUNK
    b_, s, _ = h.shape
    q, k, v, gate = jnp.split(h @ w_in, [H * dk, 2 * H * dk, 2 * H * dk + H * dv], axis=-1)
    pos = jnp.arange(s, dtype=F32)
    inv_freq = RET_ROPE_BASE ** (-jnp.linspace(0.0, 1.0, dk // 2, dtype=F32))
    ang = pos[:, None] * inv_freq[None, :]
    cos_a, sin_a = jnp.cos(ang)[:, None, :], jnp.sin(ang)[:, None, :]

    def rot(t):
        t = t.astype(F32).reshape(b_, s, H, dk)
        t1, t2 = t[..., :dk // 2], t[..., dk // 2:]
        return jnp.concatenate([t1 * cos_a - t2 * sin_a, t1 * sin_a + t2 * cos_a], axis=-1).reshape(b_, s, H * dk)

    q = _chunk_heads(rot(q), H, C)
    k = _chunk_heads(rot(k), H, C) * (dk ** -0.5)
    v = _chunk_heads(v, H, C)
    log_gamma = jnp.log(1.0 - jnp.power(2.0, -5.0 - jnp.arange(H, dtype=F32)))
    idx = jnp.arange(C, dtype=F32)
    rel = idx[:, None] - idx[None, :]
    dmask = jnp.where(rel >= 0, jnp.exp(jnp.maximum(rel, 0.0) * log_gamma[:, None, None]), 0.0)
    scores = jnp.einsum('bhncd,bhnmd->bhncm', q, k) * dmask[None, :, None]
    intra = jnp.einsum('bhncm,bhnmv->bhncv', scores, v)
    zeta = jnp.exp((C - 1.0 - idx)[None, :] * log_gamma[:, None])
    xi = jnp.exp((idx + 1.0)[None, :] * log_gamma[:, None])
    gamma_c = jnp.exp(C * log_gamma)

    def step(state, inp):
        q_n, k_n, v_n = inp
        o = jnp.einsum('bhck,bhkv->bhcv', q_n, state) * xi[None, :, :, None]
        state = (state * gamma_c[None, :, None, None]
                 + jnp.einsum('bhck,bhcv->bhkv', k_n * zeta[None, :, :, None], v_n))
        return state, o

    xs = tuple(jnp.moveaxis(t, 2, 0) for t in (q, k, v))
    _, inter = lax.scan(step, jnp.zeros((b_, H, dk, dv), F32), xs)
    o = _unchunk_heads(intra + jnp.moveaxis(inter, 0, 2))
    o = _standardize(o, 1e-6).reshape(b_, s, H * dv)
    o = o * jax.nn.silu(gate.astype(F32))
    return o.astype(h.dtype) @ w_out


def chunked_gmlp(h, w_in, ln_g, ln_b, w_s, b_s, w_out):
    C, G, W = GMLP_CHUNK, GMLP_GROUPS, GMLP_WIDTH
    b_, s, _ = h.shape
    u, v = jnp.split(jax.nn.gelu(h @ w_in, approximate=False), 2, axis=-1)
    v = layer_norm(v, ln_g, ln_b).reshape(b_, s // C, C, G, W // G)
    causal = jnp.tril(jnp.ones((C, C), dtype=bool))
    ws = jnp.where(causal, w_s, 0.0).astype(v.dtype)
    vs = jnp.einsum('gts,bnsgd->bntgd', ws, v) + b_s.T.astype(v.dtype)[None, None, :, :, None]
    return (u * vs.reshape(b_, s, W)) @ w_out


def stick_breaking(h, w_in, w_out):
    H, dh, T = SB_HEADS, SB_DH, SB_BLOCK
    b_, s, _ = h.shape
    nb = s // T
    q, k, v = jnp.split(h @ w_in, 3, axis=-1)
    q, k, v = (t.reshape(b_, s, H, dh).transpose(0, 2, 1, 3) for t in (q, k, v))
    qb = q.reshape(b_, H, nb, T, dh).transpose(2, 0, 1, 3, 4)
    key_pos = jnp.arange(s)
    scale = dh ** -0.5

    def block(args):
        q_blk, blk = args
        z = jnp.einsum('bhtd,bhsd->bhts', q_blk, k).astype(F32) * scale
        q_pos = blk * T + jnp.arange(T)
        strict = key_pos[None, :] < q_pos[:, None]
        log_1mb = jnp.where(strict, jax.nn.log_sigmoid(-z), 0.0)
        after = lax.cumsum(log_1mb, axis=3, reverse=True) - log_1mb
        a = jnp.where(strict, jnp.exp(jax.nn.log_sigmoid(z) + after), 0.0)
        return jnp.einsum('bhts,bhsd->bhtd', a.astype(v.dtype), v)

    o = lax.map(block, (qb, jnp.arange(nb)))
    o = o.transpose(1, 0, 3, 2, 4).reshape(b_, s, H * dh)
    return o @ w_out


def conv_ffn(h, w_up, conv_w, conv_b, w_down):
    gate, up = jnp.split(h @ w_up, 2, axis=-1)
    gate = causal_dwconv(gate, conv_w) + conv_b
    return (jax.nn.silu(gate) * up) @ w_down


def setup_inputs(seed: int = 0) -> dict:
    key = jax.random.key(seed)
    ks = jax.random.split(key, 32)
    D, F = D_MODEL, FFN_HIDDEN

    def nrm(i, shape, scale):
        return jax.random.normal(ks[i], shape, F32) * scale

    gdn_qkv = 2 * GDN_HEADS * GDN_DK + GDN_HEADS * GDN_DV
    gdn_in = gdn_qkv + GDN_HEADS * GDN_DV + 2 * GDN_HEADS
    ret_in = 2 * RET_HEADS * RET_DK + 2 * RET_HEADS * RET_DV
    a_vals = jax.random.uniform(ks[14], (GDN_HEADS,), F32, 1.0, 16.0)
    dt = jnp.exp(jax.random.uniform(ks[15], (GDN_HEADS,), F32, math.log(1e-3), math.log(1e-1)))
    return {
        'x': nrm(0, (BATCH, SEQ, D), 1.0),
        'c': nrm(1, (BATCH, D), 1.0),
        'cond_w': nrm(2, (D, D), D ** -0.5),
        'cond_b': nrm(3, (D,), 0.01),
        'ada_w': nrm(4, (DEPTH, D, 6 * D), 0.1 * D ** -0.5),
        'ada_b': nrm(5, (DEPTH, 6 * D), 0.01),
        'ln_g': 1.0 + nrm(6, (DEPTH, 2, D), 0.02),
        'ln_b': nrm(7, (DEPTH, 2, D), 0.02),
        'ffn_up': nrm(8, (DEPTH, D, 2 * F), D ** -0.5),
        'ffn_conv_w': nrm(9, (DEPTH, FFN_CONV, F), FFN_CONV ** -0.5),
        'ffn_conv_b': nrm(10, (DEPTH, F), 0.01),
        'ffn_down': nrm(11, (DEPTH, F, D), DN_BETA * F ** -0.5),
        'gdn_w_in': nrm(12, (D, gdn_in), D ** -0.5),
        'gdn_conv_w': nrm(13, (GDN_CONV, gdn_qkv), GDN_CONV ** -0.5),
        'gdn_a_log': jnp.log(a_vals),
        'gdn_dt_bias': dt + jnp.log(-jnp.expm1(-dt)),
        'gdn_norm_w': 1.0 + nrm(16, (GDN_DV,), 0.02),
        'gdn_w_out': nrm(17, (GDN_HEADS * GDN_DV, D), DN_BETA * (GDN_HEADS * GDN_DV) ** -0.5),
        'ret_w_in': nrm(18, (D, ret_in), D ** -0.5),
        'ret_w_out': nrm(19, (RET_HEADS * RET_DV, D), DN_BETA * (RET_HEADS * RET_DV) ** -0.5),
        'gmlp_w_in': nrm(20, (D, 2 * GMLP_WIDTH), D ** -0.5),
        'gmlp_ln_g': 1.0 + nrm(21, (GMLP_WIDTH,), 0.02),
        'gmlp_ln_b': nrm(22, (GMLP_WIDTH,), 0.02),
        'gmlp_w_s': nrm(23, (GMLP_GROUPS, GMLP_CHUNK, GMLP_CHUNK), GMLP_CHUNK ** -0.5),
        'gmlp_b_s': 1.0 + nrm(24, (GMLP_GROUPS, GMLP_CHUNK), 0.01),
        'gmlp_w_out': nrm(25, (GMLP_WIDTH, D), DN_BETA * GMLP_WIDTH ** -0.5),
        'sb_w_in': nrm(26, (D, 3 * D), D ** -0.5),
        'sb_w_out': nrm(27, (D, D), DN_BETA * D ** -0.5),
    }


def reference(x, c, cond_w, cond_b, ada_w, ada_b, ln_g, ln_b, ffn_up, ffn_conv_w, ffn_conv_b, ffn_down,
              gdn_w_in, gdn_conv_w, gdn_a_log, gdn_dt_bias, gdn_norm_w, gdn_w_out,
              ret_w_in, ret_w_out,
              gmlp_w_in, gmlp_ln_g, gmlp_ln_b, gmlp_w_s, gmlp_b_s, gmlp_w_out,
              sb_w_in, sb_w_out):
    mixers = (
        lambda t: gated_deltanet(t, gdn_w_in, gdn_conv_w, gdn_a_log, gdn_dt_bias, gdn_norm_w, gdn_w_out),
        lambda t: retention(t, ret_w_in, ret_w_out),
        lambda t: chunked_gmlp(t, gmlp_w_in, gmlp_ln_g, gmlp_ln_b, gmlp_w_s, gmlp_b_s, gmlp_w_out),
        lambda t: stick_breaking(t, sb_w_in, sb_w_out),
    )
    e = jax.nn.silu(c @ cond_w + cond_b)
    for i in range(DEPTH):
        mod = (e @ ada_w[i] + ada_b[i])[:, None, :]
        sh1, sc1, g1, sh2, sc2, g2 = jnp.split(mod, 6, axis=-1)
        y = mixers[i % N_MIXERS](x * (1.0 + sc1) + sh1)
        x = layer_norm(DN_ALPHA * x + (1.0 + g1) * y, ln_g[i, 0], ln_b[i, 0])
        y = conv_ffn(x * (1.0 + sc2) + sh2, ffn_up[i], ffn_conv_w[i], ffn_conv_b[i], ffn_down[i])
        x = layer_norm(DN_ALPHA * x + (1.0 + g2) * y, ln_g[i, 1], ln_b[i, 1])
    return x
```

```python
import functools
import math

import jax
import jax.numpy as jnp
from jax import lax
from jax.experimental import pallas as pl
from jax.experimental.pallas import tpu as pltpu

F32 = jnp.float32
BF16 = jnp.bfloat16

LANES = 128
SUBLANES = 8
VMEM_LIMIT = 56 * 1024 * 1024

LN_EPS = 1e-5
GDN_HEADS = 8
GDN_CHUNK = 64
GDN_CONV = 4
RET_HEADS = 4
RET_CHUNK = 128
RET_ROPE_BASE = 10000.0
GMLP_CHUNK = 128
GMLP_GROUPS = 8
SB_HEADS = 16
FFN_CONV = 3


def _params(*sem):
    return pltpu.CompilerParams(dimension_semantics=sem, vmem_limit_bytes=VMEM_LIMIT)


def _nn(a, b):
    return jnp.dot(a, b, preferred_element_type=F32)


def _nt(a, b):
    return lax.dot_general(a, b, (((1,), (1,)), ((), ())), preferred_element_type=F32)


def _tn(a, b):
    return lax.dot_general(a, b, (((0,), (0,)), ((), ())), preferred_element_type=F32)


def _mm32(a, b):
    return jnp.dot(a, b, precision=lax.Precision.HIGHEST, preferred_element_type=F32)


def _sigmoid(x):
    return 1.0 / (1.0 + jnp.exp(-x))


def _silu(x):
    return x * _sigmoid(x)


def _softplus(x):
    return jnp.maximum(x, 0.0) + jnp.log(1.0 + jnp.exp(-jnp.abs(x)))


def _split3(x):
    hi = x.astype(BF16)
    r = x - hi.astype(F32)
    mid = r.astype(BF16)
    lo = (r - mid.astype(F32)).astype(BF16)
    return hi, mid, lo


def _modulate(x, mod_ref, shift_row, scale_row):
    return (x * (1.0 + mod_ref[0, scale_row:scale_row + 1, :])
            + mod_ref[0, shift_row:shift_row + 1, :]).astype(BF16)


def _layer_norm_rows(r, g, b, eps):
    mu = jnp.mean(r, axis=-1, keepdims=True)
    rc = r - mu
    var = jnp.mean(rc * rc, axis=-1, keepdims=True)
    return rc * lax.rsqrt(var + eps) * g + b


def _lane_replicated_columns(rows):
    n, k = rows.shape
    padded = jnp.concatenate([rows, jnp.zeros((LANES - n, k), F32)], axis=0)
    t = padded.T
    return [jnp.broadcast_to(t[:, b:b + 1], (k, LANES)) for b in range(n)]


def _rowvec_matmul(col, w):
    n = w.shape[1]
    parts = [jnp.sum(col * w[:, c:c + LANES], axis=0, keepdims=True) for c in range(0, n, LANES)]
    return jnp.concatenate(parts, axis=1)


def _cond_kernel(c_ref, cw_ref, cb_ref, aw_ref, ab_ref, o_ref, ecol_ref):
    nb = c_ref.shape[0]

    @pl.when((pl.program_id(0) == 0) & (pl.program_id(1) == 0))
    def _():
        ccols = _lane_replicated_columns(c_ref[...])
        cw = cw_ref[...]
        e = jnp.concatenate([_rowvec_matmul(col, cw) for col in ccols], axis=0) + cb_ref[...]
        ecols = _lane_replicated_columns(_silu(e))
        for b in range(nb):
            ecol_ref[b] = ecols[b]

    w = aw_ref[0]
    for b in range(nb):
        o_ref[0, b:b + 1, :] = _rowvec_matmul(ecol_ref[b], w) + ab_ref[0]


def _conditioning(c, cond_w, cond_b, ada_w, ada_b):
    nb, d = c.shape
    depth, _, n6 = ada_w.shape
    tn = 512
    return pl.pallas_call(
        _cond_kernel,
        grid=(depth, n6 // tn),
        in_specs=[
            pl.BlockSpec((nb, d), lambda i, j: (0, 0)),
            pl.BlockSpec((d, d), lambda i, j: (0, 0)),
            pl.BlockSpec((1, d), lambda i, j: (0, 0)),
            pl.BlockSpec((1, d, tn), lambda i, j: (i, 0, j)),
            pl.BlockSpec((1, 1, tn), lambda i, j: (i, 0, j)),
        ],
        out_specs=pl.BlockSpec((1, nb, tn), lambda i, j: (i, 0, j)),
        out_shape=jax.ShapeDtypeStruct((depth, nb, n6), F32),
        scratch_shapes=[pltpu.VMEM((nb, d, LANES), F32)],
        compiler_params=_params("arbitrary", "arbitrary"),
        name="conditioning",
    )(c, cond_w, cond_b.reshape(1, d), ada_w, ada_b.reshape(depth, 1, n6))


def _out_ln_kernel(o_ref, w_ref, x_ref, mod_ref, g_ref, b_ref, out_ref, *, gate_row, alpha):
    y = _nn(o_ref[...], w_ref[...])
    r = alpha * x_ref[...] + (1.0 + mod_ref[0, gate_row:gate_row + 1, :]) * y
    out_ref[...] = _layer_norm_rows(r, g_ref[...], b_ref[...], LN_EPS)


def _out_proj_ln(o, w_out, x, mod, ln_g, ln_b, *, seq, gate_row, alpha, tm=512):
    t, kdim = o.shape
    d = x.shape[1]
    tps = seq // tm
    return pl.pallas_call(
        functools.partial(_out_ln_kernel, gate_row=gate_row, alpha=alpha),
        grid=(t // tm,),
        in_specs=[
            pl.BlockSpec((tm, kdim), lambda i: (i, 0)),
            pl.BlockSpec((kdim, d), lambda i: (0, 0)),
            pl.BlockSpec((tm, d), lambda i: (i, 0)),
            pl.BlockSpec((1, 6, d), lambda i: (i // tps, 0, 0)),
            pl.BlockSpec((1, d), lambda i: (0, 0)),
            pl.BlockSpec((1, d), lambda i: (0, 0)),
        ],
        out_specs=pl.BlockSpec((tm, d), lambda i: (i, 0)),
        out_shape=jax.ShapeDtypeStruct((t, d), F32),
        compiler_params=_params("arbitrary"),
        name="out_proj_ln",
    )(o, w_out.astype(BF16), x, mod, ln_g.reshape(1, d), ln_b.reshape(1, d))


def _ffn_kernel(x_ref, mod_ref, wg_ref, wu_ref, cw_ref, cb_ref, wd_ref, g_ref, b_ref, out_ref,
                buf_ref, carry_ref, acc_ref, *, tiles_per_seq, alpha):
    i = pl.program_id(0)
    j = pl.program_id(1)
    tm = x_ref.shape[0]
    x = x_ref[...]
    h = _modulate(x, mod_ref, 3, 4)

    @pl.when(i % tiles_per_seq == 0)
    def _():
        carry_ref[j] = jnp.zeros(carry_ref.shape[1:], F32)

    buf_ref[0:SUBLANES, :] = carry_ref[j]
    buf_ref[SUBLANES:SUBLANES + tm, :] = _nn(h, wg_ref[...])
    conv = cb_ref[...]
    for tap in range(FFN_CONV):
        off = SUBLANES - (FFN_CONV - 1) + tap
        conv = conv + cw_ref[tap:tap + 1, :] * buf_ref[off:off + tm, :]
    carry_ref[j] = buf_ref[tm:tm + SUBLANES, :]
    act = (_silu(conv) * _nn(h, wu_ref[...])).astype(BF16)
    part = _nn(act, wd_ref[...])

    @pl.when(j == 0)
    def _():
        acc_ref[...] = part

    @pl.when(j > 0)
    def _():
        acc_ref[...] += part

    @pl.when(j == pl.num_programs(1) - 1)
    def _():
        r = alpha * x + (1.0 + mod_ref[0, 5:6, :]) * acc_ref[...]
        out_ref[...] = _layer_norm_rows(r, g_ref[...], b_ref[...], LN_EPS)


def _ffn(x, mod, w_up, conv_w, conv_b, w_down, ln_g, ln_b, *, seq, alpha, tm=512):
    t, d = x.shape
    f = w_down.shape[0]
    nf = 2
    tf = f // nf
    tps = seq // tm
    w_up_b = w_up.astype(BF16)
    return pl.pallas_call(
        functools.partial(_ffn_kernel, tiles_per_seq=tps, alpha=alpha),
        grid=(t // tm, nf),
        in_specs=[
            pl.BlockSpec((tm, d), lambda i, j: (i, 0)),
            pl.BlockSpec((1, 6, d), lambda i, j: (i // tps, 0, 0)),
            pl.BlockSpec((d, tf), lambda i, j: (0, j)),
            pl.BlockSpec((d, tf), lambda i, j: (0, nf + j)),
            pl.BlockSpec((FFN_CONV, tf), lambda i, j: (0, j)),
            pl.BlockSpec((1, tf), lambda i, j: (0, j)),
            pl.BlockSpec((tf, d), lambda i, j: (j, 0)),
            pl.BlockSpec((1, d), lambda i, j: (0, 0)),
            pl.BlockSpec((1, d), lambda i, j: (0, 0)),
        ],
        out_specs=pl.BlockSpec((tm, d), lambda i, j: (i, 0)),
        out_shape=jax.ShapeDtypeStruct((t, d), F32),
        scratch_shapes=[
            pltpu.VMEM((tm + SUBLANES, tf), F32),
            pltpu.VMEM((nf, SUBLANES, tf), F32),
            pltpu.VMEM((tm, d), F32),
        ],
        compiler_params=_params("arbitrary", "arbitrary"),
        name="ffn",
    )(x, mod, w_up_b, w_up_b, conv_w, conv_b.reshape(1, f), w_down.astype(BF16),
      ln_g.reshape(1, d), ln_b.reshape(1, d))


def _gdn_in_kernel(x_ref, mod_ref, wqkv_ref, wz_ref, wab_ref, cw_ref, alog_ref, dtb_ref,
                   q_ref, k_ref, v_ref, z_ref, gb_ref, buf_ref, *, tiles_per_seq, chunk):
    i = pl.program_id(0)
    tm, d = x_ref.shape
    nh = GDN_HEADS
    hd = d // nh
    h = _modulate(x_ref[...], mod_ref, 0, 1)

    @pl.when(i % tiles_per_seq == 0)
    def _():
        buf_ref[0:SUBLANES, :] = jnp.zeros((SUBLANES, buf_ref.shape[1]), F32)

    outs = (q_ref, k_ref, v_ref)
    for s in range(3):
        cs = slice(s * d, (s + 1) * d)
        buf_ref[SUBLANES:SUBLANES + tm, cs] = _nn(h, wqkv_ref[:, cs])
        y = None
        for tap in range(GDN_CONV):
            off = SUBLANES - (GDN_CONV - 1) + tap
            term = cw_ref[tap:tap + 1, cs] * buf_ref[off:off + tm, cs]
            y = term if y is None else y + term
        y = _silu(y)
        if s < 2:
            scale = hd ** -0.5 if s == 0 else 1.0
            for hh in range(nh):
                seg = y[:, hh * hd:(hh + 1) * hd]
                inv = lax.rsqrt(jnp.sum(seg * seg, axis=-1, keepdims=True) + 1e-6)
                outs[s][:, hh * hd:(hh + 1) * hd] = (seg * (inv * scale)).astype(BF16)
        else:
            outs[s][...] = y.astype(BF16)
    buf_ref[0:SUBLANES, :] = buf_ref[tm:tm + SUBLANES, :]

    z_ref[...] = _nn(h, wz_ref[...]).astype(BF16)

    pab = _nn(h, wab_ref[...])
    lane = lax.broadcasted_iota(jnp.int32, pab.shape, 1)
    g = -jnp.exp(alog_ref[...]) * _softplus(pab + dtb_ref[...])
    gb = jnp.where(lane < nh, g, jnp.where(lane < 2 * nh, _sigmoid(pab), 0.0))
    row = lax.broadcasted_iota(jnp.int32, (tm, tm), 0)
    col = lax.broadcasted_iota(jnp.int32, (tm, tm), 1)
    tri = jnp.where(col <= row, jnp.where(jnp.bitwise_xor(row, col) < chunk, 1.0, 0.0), 0.0).astype(BF16)
    hi, mid, lo = _split3(gb)
    cum = _nn(tri, hi) + _nn(tri, mid) + _nn(tri, lo)
    gb_ref[...] = jnp.where(lane < nh, cum, gb)


def _gdn_in(x, mod, w_in, conv_w, a_log, dt_bias, *, seq, tm=256):
    t, d = x.shape
    nh = GDN_HEADS
    tps = seq // tm
    w_b = w_in.astype(BF16)
    w_qkv = w_b[:, :3 * d]
    w_z = w_b[:, 3 * d:4 * d]
    w_ab = jnp.pad(w_b[:, 4 * d:], ((0, 0), (0, LANES - 2 * nh)))
    alog = jnp.pad(a_log, (0, LANES - nh)).reshape(1, LANES)
    dtb = jnp.pad(dt_bias, (0, LANES - nh)).reshape(1, LANES)
    row = lambda i: (i, 0)
    fixed = lambda i: (0, 0)
    return pl.pallas_call(
        functools.partial(_gdn_in_kernel, tiles_per_seq=tps, chunk=GDN_CHUNK),
        grid=(t // tm,),
        in_specs=[
            pl.BlockSpec((tm, d), row),
            pl.BlockSpec((1, 6, d), lambda i: (i // tps, 0, 0)),
            pl.BlockSpec((d, 3 * d), fixed),
            pl.BlockSpec((d, d), fixed),
            pl.BlockSpec((d, LANES), fixed),
            pl.BlockSpec((GDN_CONV, 3 * d), fixed),
            pl.BlockSpec((1, LANES), fixed),
            pl.BlockSpec((1, LANES), fixed),
        ],
        out_specs=[pl.BlockSpec((tm, d), row)] * 4 + [pl.BlockSpec((tm, LANES), row)],
        out_shape=[jax.ShapeDtypeStruct((t, d), BF16)] * 4 + [jax.ShapeDtypeStruct((t, LANES), F32)],
        scratch_shapes=[pltpu.VMEM((tm + SUBLANES, 3 * d), F32)],
        compiler_params=_params("arbitrary"),
        name="gdn_in",
    )(x, mod, w_qkv, w_z, w_ab, conv_w, alog, dtb)


def _gdn_core_kernel(q_ref, k_ref, v_ref, z_ref, gb_ref, nw_ref, o_ref, state_ref):
    c, d = q_ref.shape
    nh = GDN_HEADS
    hd = d // nh

    @pl.when(pl.program_id(1) == 0)
    def _():
        state_ref[...] = jnp.zeros(state_ref.shape, F32)

    row = lax.broadcasted_iota(jnp.int32, (c, c), 0)
    col = lax.broadcasted_iota(jnp.int32, (c, c), 1)
    causal = row >= col
    strict = row > col
    eye = jnp.where(row == col, 1.0, 0.0)
    gb = gb_ref[...]
    for hh in range(nh):
        hs = slice(hh * hd, (hh + 1) * hd)
        qh = q_ref[:, hs]
        kh = k_ref[:, hs]
        kf = kh.astype(F32)
        gc = jnp.broadcast_to(gb[:, hh:hh + 1], (c, hd))
        beta = jnp.broadcast_to(gb[:, nh + hh:nh + hh + 1], (c, hd))
        gcc = gc[:, :c]
        gc_row = jnp.sum(jnp.where(row == col, gcc, 0.0), axis=0, keepdims=True)
        decay = jnp.where(causal, jnp.exp(jnp.where(causal, gcc - gc_row, 0.0)), 0.0)
        kb = kf * beta
        a = jnp.where(strict, _nt(kb.astype(BF16), kh) * decay, 0.0)
        inv = eye - a
        power = a
        for _ in range(int(math.log2(c)) - 1):
            power = _mm32(power, power)
            inv = inv + _mm32(inv, power)
        rhs = jnp.concatenate([v_ref[:, hs].astype(F32) * beta, kb * jnp.exp(gc)], axis=1)
        sol = _mm32(inv, rhs)
        u, w = sol[:, :hd], sol[:, hd:]
        qk = jnp.where(causal, _nt(qh, kh) * decay, 0.0)

        state = state_ref[hh]
        state_b = state.astype(BF16)
        v_new = u - _nn(w.astype(BF16), state_b)
        v_new_b = v_new.astype(BF16)
        o = _nn((qh.astype(F32) * jnp.exp(gc)).astype(BF16), state_b) + _nn(qk.astype(BF16), v_new_b)
        g_last = gc[c - 1:c, :]
        state_ref[hh] = (state * jnp.exp(g_last)
                         + _tn((kf * jnp.exp(g_last - gc)).astype(BF16), v_new_b))

        o = o * lax.rsqrt(jnp.mean(o * o, axis=-1, keepdims=True) + 1e-6) * nw_ref[...]
        o_ref[:, hs] = (o * _silu(z_ref[:, hs].astype(F32))).astype(BF16)


def _gdn_core(q, k, v, z, gb, norm_w, *, batch, seq):
    t, d = q.shape
    c = GDN_CHUNK
    n = seq // c
    hd = d // GDN_HEADS
    row = lambda b, j: (b * n + j, 0)
    return pl.pallas_call(
        _gdn_core_kernel,
        grid=(batch, n),
        in_specs=[pl.BlockSpec((c, d), row)] * 4 + [
            pl.BlockSpec((c, LANES), row),
            pl.BlockSpec((1, hd), lambda b, j: (0, 0)),
        ],
        out_specs=pl.BlockSpec((c, d), row),
        out_shape=jax.ShapeDtypeStruct((t, d), BF16),
        scratch_shapes=[pltpu.VMEM((GDN_HEADS, hd, hd), F32)],
        compiler_params=_params("arbitrary", "arbitrary"),
        name="gdn_core",
    )(q, k, v, z, gb, norm_w.reshape(1, hd))


def _rope_table_kernel(cos_ref, sin_ref):
    ts, half = cos_ref.shape
    pos = (lax.broadcasted_iota(jnp.int32, (ts, half), 0) + pl.program_id(0) * ts).astype(F32)
    frac = lax.broadcasted_iota(jnp.int32, (ts, half), 1).astype(F32) / (half - 1.0)
    ang = pos * jnp.exp(-frac * math.log(RET_ROPE_BASE))
    cos_ref[...] = jnp.cos(ang)
    sin_ref[...] = jnp.sin(ang)


def _rope_tables(seq, half, ts=256):
    spec = pl.BlockSpec((ts, half), lambda i: (i, 0))
    return pl.pallas_call(
        _rope_table_kernel,
        grid=(seq // ts,),
        out_specs=[spec, spec],
        out_shape=[jax.ShapeDtypeStruct((seq, half), F32)] * 2,
        compiler_params=_params("arbitrary"),
        name="rope_tables",
    )()


def _ret_in_kernel(x_ref, mod_ref, w_ref, cos_ref, sin_ref, q_ref, k_ref, v_ref, gate_ref):
    d = x_ref.shape[1]
    nh = RET_HEADS
    dk = d // nh
    half = dk // 2
    h = _modulate(x_ref[...], mod_ref, 0, 1)
    cos_a = cos_ref[...]
    sin_a = sin_ref[...]
    for s, (out, scale) in enumerate(((q_ref, 1.0), (k_ref, dk ** -0.5))):
        t = _nn(h, w_ref[:, s * d:(s + 1) * d])
        for hh in range(nh):
            t1 = t[:, hh * dk:hh * dk + half]
            t2 = t[:, hh * dk + half:(hh + 1) * dk]
            out[:, hh * dk:hh * dk + half] = ((t1 * cos_a - t2 * sin_a) * scale).astype(BF16)
            out[:, hh * dk + half:(hh + 1) * dk] = ((t1 * sin_a + t2 * cos_a) * scale).astype(BF16)
    v_ref[...] = _nn(h, w_ref[:, 2 * d:4 * d]).astype(BF16)
    gate_ref[...] = _nn(h, w_ref[:, 4 * d:6 * d])


def _ret_in(x, mod, w_in, cos_t, sin_t, *, seq, tm=256):
    t, d = x.shape
    tps = seq // tm
    half = cos_t.shape[1]
    row = lambda i: (i, 0)
    return pl.pallas_call(
        _ret_in_kernel,
        grid=(t // tm,),
        in_specs=[
            pl.BlockSpec((tm, d), row),
            pl.BlockSpec((1, 6, d), lambda i: (i // tps, 0, 0)),
            pl.BlockSpec((d, 6 * d), lambda i: (0, 0)),
            pl.BlockSpec((tm, half), lambda i: (i % tps, 0)),
            pl.BlockSpec((tm, half), lambda i: (i % tps, 0)),
        ],
        out_specs=[pl.BlockSpec((tm, d), row), pl.BlockSpec((tm, d), row),
                   pl.BlockSpec((tm, 2 * d), row), pl.BlockSpec((tm, 2 * d), row)],
        out_shape=[jax.ShapeDtypeStruct((t, d), BF16), jax.ShapeDtypeStruct((t, d), BF16),
                   jax.ShapeDtypeStruct((t, 2 * d), BF16), jax.ShapeDtypeStruct((t, 2 * d), F32)],
        compiler_params=_params("arbitrary"),
        name="ret_in",
    )(x, mod, w_in.astype(BF16), cos_t, sin_t)


def _ret_core_kernel(q_ref, k_ref, v_ref, gate_ref, o_ref, state_ref):
    c, d = q_ref.shape
    nh = RET_HEADS
    dk = d // nh
    dv = v_ref.shape[1] // nh

    @pl.when(pl.program_id(1) == 0)
    def _():
        state_ref[...] = jnp.zeros(state_ref.shape, F32)

    rel = (lax.broadcasted_iota(jnp.int32, (c, c), 0)
           - lax.broadcasted_iota(jnp.int32, (c, c), 1)).astype(F32)
    idx = lax.broadcasted_iota(jnp.int32, (c, 1), 0).astype(F32)
    for hh in range(nh):
        log_gamma = math.log(1.0 - 2.0 ** (-5.0 - hh))
        dmask = jnp.where(rel >= 0, jnp.exp(jnp.maximum(rel, 0.0) * log_gamma), 0.0)
        zeta = jnp.exp((c - 1.0 - idx) * log_gamma)
        xi = jnp.exp((idx + 1.0) * log_gamma)
        qh = q_ref[:, hh * dk:(hh + 1) * dk]
        kh = k_ref[:, hh * dk:(hh + 1) * dk]
        vh = v_ref[:, hh * dv:(hh + 1) * dv]
        scores = _nt(qh, kh) * dmask
        state = state_ref[hh]
        o = _nn(scores.astype(BF16), vh) + _nn(qh, state.astype(BF16)) * xi
        state_ref[hh] = (state * math.exp(c * log_gamma)
                         + _tn((kh.astype(F32) * zeta).astype(BF16), vh))
        mu = jnp.mean(o, axis=-1, keepdims=True)
        oc = o - mu
        var = jnp.mean(oc * oc, axis=-1, keepdims=True)
        o = oc * lax.rsqrt(var + 1e-6)
        o_ref[:, hh * dv:(hh + 1) * dv] = (o * _silu(gate_ref[:, hh * dv:(hh + 1) * dv])).astype(BF16)


def _ret_core(q, k, v, gate, *, batch, seq):
    t, d = q.shape
    c = RET_CHUNK
    n = seq // c
    dk = d // RET_HEADS
    dv = v.shape[1] // RET_HEADS
    row = lambda b, j: (b * n + j, 0)
    return pl.pallas_call(
        _ret_core_kernel,
        grid=(batch, n),
        in_specs=[pl.BlockSpec((c, d), row), pl.BlockSpec((c, d), row),
                  pl.BlockSpec((c, 2 * d), row), pl.BlockSpec((c, 2 * d), row)],
        out_specs=pl.BlockSpec((c, 2 * d), row),
        out_shape=jax.ShapeDtypeStruct((t, 2 * d), BF16),
        scratch_shapes=[pltpu.VMEM((RET_HEADS, dk, dv), F32)],
        compiler_params=_params("arbitrary", "arbitrary"),
        name="ret_core",
    )(q, k, v, gate)


def _erf(x):
    x = jnp.clip(x, -4.0, 4.0)
    x2 = x * x
    p = -2.72614225801306e-10
    for coef in (2.77068142495902e-08, -2.10102402082508e-06, -5.69250639462346e-05,
                 -7.34990630326855e-04, -2.95459980854025e-03, -1.60960333262415e-02):
        p = p * x2 + coef
    q = -1.45660718464996e-05
    for coef in (-2.13374055278905e-04, -1.68282697438203e-03, -7.37332916720468e-03,
                 -1.42647390514189e-02):
        q = q * x2 + coef
    return x * p / q


def _gmlp_kernel(x_ref, mod_ref, win_ref, lng_ref, lnb_ref, ws_ref, bs_ref, wout_ref, g_ref, b_ref,
                 out_ref, *, alpha):
    tm, d = x_ref.shape
    width = win_ref.shape[1] // 2
    c = GMLP_CHUNK
    ng = GMLP_GROUPS
    gw = width // ng
    x = x_ref[...]
    h = _modulate(x, mod_ref, 0, 1)

    def gelu(t):
        return 0.5 * t * (1.0 + _erf(t * (2.0 ** -0.5)))

    v = gelu(_nn(h, win_ref[:, width:]))
    v = _layer_norm_rows(v, lng_ref[...], lnb_ref[...], LN_EPS).astype(BF16)
    u = gelu(_nn(h, win_ref[:, :width]))
    row = lax.broadcasted_iota(jnp.int32, (c, c), 0)
    col = lax.broadcasted_iota(jnp.int32, (c, c), 1)
    parts = []
    for n in range(tm // c):
        rs = slice(n * c, (n + 1) * c)
        groups = []
        for gi in range(ng):
            ls = slice(gi * gw, (gi + 1) * gw)
            ws = jnp.where(row >= col, ws_ref[gi], 0.0).astype(BF16)
            vs = _nn(ws, v[rs, ls]) + bs_ref[:, gi:gi + 1]
            groups.append((u[rs, ls] * vs).astype(BF16))
        parts.append(jnp.concatenate(groups, axis=1))
    gated = jnp.concatenate(parts, axis=0)
    y = _nn(gated, wout_ref[...])
    r = alpha * x + (1.0 + mod_ref[0, 2:3, :]) * y
    out_ref[...] = _layer_norm_rows(r, g_ref[...], b_ref[...], LN_EPS)


def _gmlp(x, mod, w_in, ln_g, ln_b, w_s, b_s, w_out, res_g, res_b, *, seq, alpha, tm=256):
    t, d = x.shape
    width = w_out.shape[0]
    ng, c, _ = w_s.shape
    tps = seq // tm
    fixed = lambda i: (0, 0)
    return pl.pallas_call(
        functools.partial(_gmlp_kernel, alpha=alpha),
        grid=(t // tm,),
        in_specs=[
            pl.BlockSpec((tm, d), lambda i: (i, 0)),
            pl.BlockSpec((1, 6, d), lambda i: (i // tps, 0, 0)),
            pl.BlockSpec((d, 2 * width), fixed),
            pl.BlockSpec((1, width), fixed),
            pl.BlockSpec((1, width), fixed),
            pl.BlockSpec((ng, c, c), lambda i: (0, 0, 0)),
            pl.BlockSpec((c, ng), fixed),
            pl.BlockSpec((width, d), fixed),
            pl.BlockSpec((1, d), fixed),
            pl.BlockSpec((1, d), fixed),
        ],
        out_specs=pl.BlockSpec((tm, d), lambda i: (i, 0)),
        out_shape=jax.ShapeDtypeStruct((t, d), F32),
        compiler_params=_params("arbitrary"),
        name="gmlp",
    )(x, mod, w_in.astype(BF16), ln_g.reshape(1, width), ln_b.reshape(1, width), w_s, b_s.T,
      w_out.astype(BF16), res_g.reshape(1, d), res_b.reshape(1, d))


def _sb_in_kernel(x_ref, mod_ref, w_ref, q_ref, k_ref, v_ref, *, q_scale):
    d = x_ref.shape[1]
    h = _modulate(x_ref[...], mod_ref, 0, 1)
    q_ref[...] = (_nn(h, w_ref[:, 0:d]) * q_scale).astype(BF16)
    k_ref[...] = _nn(h, w_ref[:, d:2 * d]).astype(BF16)
    v_ref[...] = _nn(h, w_ref[:, 2 * d:3 * d]).astype(BF16)


def _sb_in(x, mod, w_in, *, seq, tm=512):
    t, d = x.shape
    tps = seq // tm
    row = lambda i: (i, 0)
    return pl.pallas_call(
        functools.partial(_sb_in_kernel, q_scale=(d // SB_HEADS) ** -0.5),
        grid=(t // tm,),
        in_specs=[
            pl.BlockSpec((tm, d), row),
            pl.BlockSpec((1, 6, d), lambda i: (i // tps, 0, 0)),
            pl.BlockSpec((d, 3 * d), lambda i: (0, 0)),
        ],
        out_specs=[pl.BlockSpec((tm, d), row)] * 3,
        out_shape=[jax.ShapeDtypeStruct((t, d), BF16)] * 3,
        compiler_params=_params("arbitrary"),
        name="sb_in",
    )(x, mod, w_in.astype(BF16))


def _sb_core_kernel(q_ref, k_ref, v_ref, o_ref, *, tk):
    tq, lanes = q_ref.shape
    dh = lanes // 2
    qi = pl.program_id(2)
    q = q_ref[...]
    lane = lax.broadcasted_iota(jnp.int32, (tq, lanes), 1)
    q_heads = (jnp.where(lane < dh, q, jnp.zeros_like(q)), jnp.where(lane >= dh, q, jnp.zeros_like(q)))
    ones_lower = jnp.where(lax.broadcasted_iota(jnp.int32, (tk, tk), 0)
                           > lax.broadcasted_iota(jnp.int32, (tk, tk), 1), 1.0, 0.0).astype(BF16)

    def tile(kt, carry, acc, masked):
        start = pl.multiple_of(kt * tk, tk)
        k = k_ref[pl.ds(start, tk), :]
        v = v_ref[pl.ds(start, tk), :]
        new_carry, new_acc = [], []
        for hh in range(2):
            z = _nt(q_heads[hh], k)
            sp = _softplus(z)
            neg_l = sp
            log_b = z - sp
            if masked:
                valid = (lax.broadcasted_iota(jnp.int32, (tq, tk), 1) + kt * tk
                         < lax.broadcasted_iota(jnp.int32, (tq, tk), 0) + qi * tq)
                neg_l = jnp.where(valid, neg_l, 0.0)
            hi = neg_l.astype(BF16)
            lo = (neg_l - hi.astype(F32)).astype(BF16)
            suffix = _nn(hi, ones_lower) + _nn(lo, ones_lower)
            after = suffix + carry[hh]
            a = jnp.exp(log_b - after)
            if masked:
                a = jnp.where(valid, a, 0.0)
            new_acc.append(acc[hh] + _nn(a.astype(BF16), v))
            new_carry.append(carry[hh] + jnp.sum(neg_l, axis=-1, keepdims=True))
        return tuple(new_carry), tuple(new_acc)

    zero_c = jnp.zeros((tq, 1), F32)
    zero_a = jnp.zeros((tq, lanes), F32)
    n_diag = tq // tk
    carry, acc = (zero_c, zero_c), (zero_a, zero_a)
    for dgi in range(n_diag):
        carry, acc = tile(qi * n_diag + (n_diag - 1 - dgi), carry, acc, True)

    def body(it, state):
        kt = qi * n_diag - 1 - it
        return tile(kt, state[0], state[1], False)

    carry, acc = lax.fori_loop(0, qi * n_diag, body, (carry, acc))
    o_ref[...] = jnp.where(lane < dh, acc[0], acc[1]).astype(BF16)


def _sb_core(q, k, v, *, batch, seq, tq=256, tk=256):
    t, d = q.shape
    tq = min(tq, seq)
    tk = min(tk, tq)
    nq = seq // tq
    groups = d // LANES
    return pl.pallas_call(
        functools.partial(_sb_core_kernel, tk=tk),
        grid=(batch, groups, nq),
        in_specs=[
            pl.BlockSpec((tq, LANES), lambda b, g, i: (b * nq + i, g)),
            pl.BlockSpec((seq, LANES), lambda b, g, i: (b, g)),
            pl.BlockSpec((seq, LANES), lambda b, g, i: (b, g)),
        ],
        out_specs=pl.BlockSpec((tq, LANES), lambda b, g, i: (b * nq + i, g)),
        out_shape=jax.ShapeDtypeStruct((t, d), BF16),
        compiler_params=_params("arbitrary", "arbitrary", "arbitrary"),
        name="sb_core",
    )(q, k, v)


def kernel(x, c, cond_w, cond_b, ada_w, ada_b, ln_g, ln_b, ffn_up, ffn_conv_w, ffn_conv_b, ffn_down,
           gdn_w_in, gdn_conv_w, gdn_a_log, gdn_dt_bias, gdn_norm_w, gdn_w_out,
           ret_w_in, ret_w_out,
           gmlp_w_in, gmlp_ln_g, gmlp_ln_b, gmlp_w_s, gmlp_b_s, gmlp_w_out,
           sb_w_in, sb_w_out):
    batch, seq, d = x.shape
    depth = ada_w.shape[0]
    alpha = (2.0 * depth) ** 0.25
    mods = _conditioning(c, cond_w, cond_b, ada_w, ada_b).reshape(depth, batch, 6, d)
    xt = x.reshape(batch * seq, d)

    for i in range(depth):
        mod = mods[i]
        res = dict(seq=seq, gate_row=2, alpha=alpha)
        mixer = i % 4
        if mixer == 0:
            q, k, v, z, gb = _gdn_in(xt, mod, gdn_w_in, gdn_conv_w, gdn_a_log, gdn_dt_bias, seq=seq)
            o = _gdn_core(q, k, v, z, gb, gdn_norm_w, batch=batch, seq=seq)
            xt = _out_proj_ln(o, gdn_w_out, xt, mod, ln_g[i, 0], ln_b[i, 0], **res)
        elif mixer == 1:
            cos_t, sin_t = _rope_tables(seq, d // RET_HEADS // 2)
            q, k, v, gate = _ret_in(xt, mod, ret_w_in, cos_t, sin_t, seq=seq)
            o = _ret_core(q, k, v, gate, batch=batch, seq=seq)
            xt = _out_proj_ln(o, ret_w_out, xt, mod, ln_g[i, 0], ln_b[i, 0], **res)
        elif mixer == 2:
            xt = _gmlp(xt, mod, gmlp_w_in, gmlp_ln_g, gmlp_ln_b, gmlp_w_s, gmlp_b_s, gmlp_w_out,
                       ln_g[i, 0], ln_b[i, 0], seq=seq, alpha=alpha)
        else:
            q, k, v = _sb_in(xt, mod, sb_w_in, seq=seq)
            o = _sb_core(q, k, v, batch=batch, seq=seq)
            xt = _out_proj_ln(o, sb_w_out, xt, mod, ln_g[i, 0], ln_b[i, 0], **res)
        xt = _ffn(xt, mod, ffn_up[i], ffn_conv_w[i], ffn_conv_b[i], ffn_down[i],
                  ln_g[i, 1], ln_b[i, 1], seq=seq, alpha=alpha)
    return xt.reshape(batch, seq, d)
```

```python
import functools
import math

import jax
import jax.numpy as jnp
from jax import lax
from jax.experimental import pallas as pl
from jax.experimental.pallas import tpu as pltpu

F32 = jnp.float32
BF16 = jnp.bfloat16

LANES = 128
SUBLANES = 8
VMEM_LIMIT = 56 * 1024 * 1024

LN_EPS = 1e-5
GDN_HEADS = 8
GDN_CHUNK = 64
GDN_CONV = 4
RET_HEADS = 4
RET_CHUNK = 128
RET_ROPE_BASE = 10000.0
GMLP_CHUNK = 128
GMLP_GROUPS = 8
SB_HEADS = 16
FFN_CONV = 3


def _params(*sem):
    return pltpu.CompilerParams(dimension_semantics=sem, vmem_limit_bytes=VMEM_LIMIT)


def _nn(a, b):
    return jnp.dot(a, b, preferred_element_type=F32)


def _nt(a, b):
    return lax.dot_general(a, b, (((1,), (1,)), ((), ())), preferred_element_type=F32)


def _tn(a, b):
    return lax.dot_general(a, b, (((0,), (0,)), ((), ())), preferred_element_type=F32)


def _sigmoid(x):
    return 1.0 / (1.0 + jnp.exp(-x))


def _silu(x):
    return x * _sigmoid(x)


def _softplus(x):
    return jnp.maximum(x, 0.0) + jnp.log(1.0 + jnp.exp(-jnp.abs(x)))


def _split3(x):
    hi = x.astype(BF16)
    r = x - hi.astype(F32)
    mid = r.astype(BF16)
    lo = (r - mid.astype(F32)).astype(BF16)
    return hi, mid, lo


def _modulate(x, mod_ref, shift_row, scale_row):
    return (x * (1.0 + mod_ref[0, scale_row:scale_row + 1, :])
            + mod_ref[0, shift_row:shift_row + 1, :]).astype(BF16)


def _layer_norm_rows(r, g, b, eps):
    mu = jnp.mean(r, axis=-1, keepdims=True)
    rc = r - mu
    var = jnp.mean(rc * rc, axis=-1, keepdims=True)
    return rc * lax.rsqrt(var + eps) * g + b


def _lane_replicated_columns(rows):
    n, k = rows.shape
    padded = jnp.concatenate([rows, jnp.zeros((LANES - n, k), F32)], axis=0)
    t = padded.T
    return [jnp.broadcast_to(t[:, b:b + 1], (k, LANES)) for b in range(n)]


def _rowvec_matmul(col, w):
    n = w.shape[1]
    parts = [jnp.sum(col * w[:, c:c + LANES], axis=0, keepdims=True) for c in range(0, n, LANES)]
    return jnp.concatenate(parts, axis=1)


def _cond_kernel(c_ref, cw_ref, cb_ref, aw_ref, ab_ref, o_ref, ecol_ref):
    nb = c_ref.shape[0]

    @pl.when((pl.program_id(0) == 0) & (pl.program_id(1) == 0))
    def _():
        ccols = _lane_replicated_columns(c_ref[...])
        cw = cw_ref[...]
        e = jnp.concatenate([_rowvec_matmul(col, cw) for col in ccols], axis=0) + cb_ref[...]
        ecols = _lane_replicated_columns(_silu(e))
        for b in range(nb):
            ecol_ref[b] = ecols[b]

    w = aw_ref[0]
    for b in range(nb):
        o_ref[0, b:b + 1, :] = _rowvec_matmul(ecol_ref[b], w) + ab_ref[0]


def _conditioning(c, cond_w, cond_b, ada_w, ada_b):
    nb, d = c.shape
    depth, _, n6 = ada_w.shape
    tn = 512
    return pl.pallas_call(
        _cond_kernel,
        grid=(depth, n6 // tn),
        in_specs=[
            pl.BlockSpec((nb, d), lambda i, j: (0, 0)),
            pl.BlockSpec((d, d), lambda i, j: (0, 0)),
            pl.BlockSpec((1, d), lambda i, j: (0, 0)),
            pl.BlockSpec((1, d, tn), lambda i, j: (i, 0, j)),
            pl.BlockSpec((1, 1, tn), lambda i, j: (i, 0, j)),
        ],
        out_specs=pl.BlockSpec((1, nb, tn), lambda i, j: (i, 0, j)),
        out_shape=jax.ShapeDtypeStruct((depth, nb, n6), F32),
        scratch_shapes=[pltpu.VMEM((nb, d, LANES), F32)],
        compiler_params=_params("arbitrary", "arbitrary"),
        name="conditioning",
    )(c, cond_w, cond_b.reshape(1, d), ada_w, ada_b.reshape(depth, 1, n6))


def _out_ln_kernel(o_ref, w_ref, x_ref, mod_ref, g_ref, b_ref, out_ref, *, gate_row, alpha):
    y = _nn(o_ref[...], w_ref[...])
    r = alpha * x_ref[...] + (1.0 + mod_ref[0, gate_row:gate_row + 1, :]) * y
    out_ref[...] = _layer_norm_rows(r, g_ref[...], b_ref[...], LN_EPS)


def _out_proj_ln(o, w_out, x, mod, ln_g, ln_b, *, seq, gate_row, alpha, tm=512):
    t, kdim = o.shape
    d = x.shape[1]
    tps = seq // tm
    return pl.pallas_call(
        functools.partial(_out_ln_kernel, gate_row=gate_row, alpha=alpha),
        grid=(t // tm,),
        in_specs=[
            pl.BlockSpec((tm, kdim), lambda i: (i, 0)),
            pl.BlockSpec((kdim, d), lambda i: (0, 0)),
            pl.BlockSpec((tm, d), lambda i: (i, 0)),
            pl.BlockSpec((1, 6, d), lambda i: (i // tps, 0, 0)),
            pl.BlockSpec((1, d), lambda i: (0, 0)),
            pl.BlockSpec((1, d), lambda i: (0, 0)),
        ],
        out_specs=pl.BlockSpec((tm, d), lambda i: (i, 0)),
        out_shape=jax.ShapeDtypeStruct((t, d), F32),
        compiler_params=_params("arbitrary"),
        name="out_proj_ln",
    )(o, w_out.astype(BF16), x, mod, ln_g.reshape(1, d), ln_b.reshape(1, d))


def _ffn_kernel(x_ref, mod_ref, wg_ref, wu_ref, cw_ref, cb_ref, wd_ref, g_ref, b_ref, out_ref,
                buf_ref, carry_ref, acc_ref, *, tiles_per_seq, alpha):
    i = pl.program_id(0)
    j = pl.program_id(1)
    tm = x_ref.shape[0]
    x = x_ref[...]
    h = _modulate(x, mod_ref, 3, 4)

    @pl.when(i % tiles_per_seq == 0)
    def _():
        carry_ref[j] = jnp.zeros(carry_ref.shape[1:], F32)

    buf_ref[0:SUBLANES, :] = carry_ref[j]
    buf_ref[SUBLANES:SUBLANES + tm, :] = _nn(h, wg_ref[...])
    conv = cb_ref[...]
    for tap in range(FFN_CONV):
        off = SUBLANES - (FFN_CONV - 1) + tap
        conv = conv + cw_ref[tap:tap + 1, :] * buf_ref[off:off + tm, :]
    carry_ref[j] = buf_ref[tm:tm + SUBLANES, :]
    act = (_silu(conv) * _nn(h, wu_ref[...])).astype(BF16)
    part = _nn(act, wd_ref[...])

    @pl.when(j == 0)
    def _():
        acc_ref[...] = part

    @pl.when(j > 0)
    def _():
        acc_ref[...] += part

    @pl.when(j == pl.num_programs(1) - 1)
    def _():
        r = alpha * x + (1.0 + mod_ref[0, 5:6, :]) * acc_ref[...]
        out_ref[...] = _layer_norm_rows(r, g_ref[...], b_ref[...], LN_EPS)


def _ffn(x, mod, w_up, conv_w, conv_b, w_down, ln_g, ln_b, *, seq, alpha, tm=512):
    t, d = x.shape
    f = w_down.shape[0]
    nf = 2
    tf = f // nf
    tps = seq // tm
    w_up_b = w_up.astype(BF16)
    return pl.pallas_call(
        functools.partial(_ffn_kernel, tiles_per_seq=tps, alpha=alpha),
        grid=(t // tm, nf),
        in_specs=[
            pl.BlockSpec((tm, d), lambda i, j: (i, 0)),
            pl.BlockSpec((1, 6, d), lambda i, j: (i // tps, 0, 0)),
            pl.BlockSpec((d, tf), lambda i, j: (0, j)),
            pl.BlockSpec((d, tf), lambda i, j: (0, nf + j)),
            pl.BlockSpec((FFN_CONV, tf), lambda i, j: (0, j)),
            pl.BlockSpec((1, tf), lambda i, j: (0, j)),
            pl.BlockSpec((tf, d), lambda i, j: (j, 0)),
            pl.BlockSpec((1, d), lambda i, j: (0, 0)),
            pl.BlockSpec((1, d), lambda i, j: (0, 0)),
        ],
        out_specs=pl.BlockSpec((tm, d), lambda i, j: (i, 0)),
        out_shape=jax.ShapeDtypeStruct((t, d), F32),
        scratch_shapes=[
            pltpu.VMEM((tm + SUBLANES, tf), F32),
            pltpu.VMEM((nf, SUBLANES, tf), F32),
            pltpu.VMEM((tm, d), F32),
        ],
        compiler_params=_params("arbitrary", "arbitrary"),
        name="ffn",
    )(x, mod, w_up_b, w_up_b, conv_w, conv_b.reshape(1, f), w_down.astype(BF16),
      ln_g.reshape(1, d), ln_b.reshape(1, d))


def _gdn_in_kernel(x_ref, mod_ref, wqkv_ref, wz_ref, wab_ref, cw_ref, alog_ref, dtb_ref,
                   q_ref, k_ref, v_ref, z_ref, gb_ref, gt_ref, buf_ref, *, tiles_per_seq, chunk):
    i = pl.program_id(0)
    tm, d = x_ref.shape
    nh = GDN_HEADS
    hd = d // nh
    h = _modulate(x_ref[...], mod_ref, 0, 1)

    @pl.when(i % tiles_per_seq == 0)
    def _():
        buf_ref[0:SUBLANES, :] = jnp.zeros((SUBLANES, buf_ref.shape[1]), F32)

    outs = (q_ref, k_ref, v_ref)
    for s in range(3):
        cs = slice(s * d, (s + 1) * d)
        buf_ref[SUBLANES:SUBLANES + tm, cs] = _nn(h, wqkv_ref[:, cs])
        y = None
        for tap in range(GDN_CONV):
            off = SUBLANES - (GDN_CONV - 1) + tap
            term = cw_ref[tap:tap + 1, cs] * buf_ref[off:off + tm, cs]
            y = term if y is None else y + term
        y = _silu(y)
        if s < 2:
            scale = hd ** -0.5 if s == 0 else 1.0
            for hh in range(nh):
                seg = y[:, hh * hd:(hh + 1) * hd]
                inv = lax.rsqrt(jnp.sum(seg * seg, axis=-1, keepdims=True) + 1e-6)
                outs[s][:, hh * hd:(hh + 1) * hd] = (seg * (inv * scale)).astype(BF16)
        else:
            outs[s][...] = y.astype(BF16)
    buf_ref[0:SUBLANES, :] = buf_ref[tm:tm + SUBLANES, :]

    z_ref[...] = _nn(h, wz_ref[...]).astype(BF16)

    pab = _nn(h, wab_ref[...])
    lane = lax.broadcasted_iota(jnp.int32, pab.shape, 1)
    g = -jnp.exp(alog_ref[...]) * _softplus(pab + dtb_ref[...])
    gb = jnp.where(lane < nh, g, jnp.where(lane < 2 * nh, _sigmoid(pab), 0.0))
    row = lax.broadcasted_iota(jnp.int32, (tm, tm), 0)
    col = lax.broadcasted_iota(jnp.int32, (tm, tm), 1)
    tri = jnp.where(col <= row, jnp.where(jnp.bitwise_xor(row, col) < chunk, 1.0, 0.0), 0.0).astype(BF16)
    hi, mid, lo = _split3(gb)
    cum = _nn(tri, hi) + _nn(tri, mid) + _nn(tri, lo)
    gb = jnp.where(lane < nh, cum, gb)
    gb_ref[...] = gb
    gt_ref[...] = gb.T[0:2 * nh, :]


def _gdn_in(x, mod, w_in, conv_w, a_log, dt_bias, *, seq, tm=256):
    t, d = x.shape
    nh = GDN_HEADS
    tps = seq // tm
    w_b = w_in.astype(BF16)
    w_qkv = w_b[:, :3 * d]
    w_z = w_b[:, 3 * d:4 * d]
    w_ab = jnp.pad(w_b[:, 4 * d:], ((0, 0), (0, LANES - 2 * nh)))
    alog = jnp.pad(a_log, (0, LANES - nh)).reshape(1, LANES)
    dtb = jnp.pad(dt_bias, (0, LANES - nh)).reshape(1, LANES)
    row = lambda i: (i, 0)
    fixed = lambda i: (0, 0)
    return pl.pallas_call(
        functools.partial(_gdn_in_kernel, tiles_per_seq=tps, chunk=GDN_CHUNK),
        grid=(t // tm,),
        in_specs=[
            pl.BlockSpec((tm, d), row),
            pl.BlockSpec((1, 6, d), lambda i: (i // tps, 0, 0)),
            pl.BlockSpec((d, 3 * d), fixed),
            pl.BlockSpec((d, d), fixed),
            pl.BlockSpec((d, LANES), fixed),
            pl.BlockSpec((GDN_CONV, 3 * d), fixed),
            pl.BlockSpec((1, LANES), fixed),
            pl.BlockSpec((1, LANES), fixed),
        ],
        out_specs=[pl.BlockSpec((tm, d), row)] * 4 + [pl.BlockSpec((tm, LANES), row),
                                                      pl.BlockSpec((2 * nh, tm), lambda i: (0, i))],
        out_shape=[jax.ShapeDtypeStruct((t, d), BF16)] * 4 + [jax.ShapeDtypeStruct((t, LANES), F32),
                                                              jax.ShapeDtypeStruct((2 * nh, t), F32)],
        scratch_shapes=[pltpu.VMEM((tm + SUBLANES, 3 * d), F32)],
        compiler_params=_params("arbitrary"),
        name="gdn_in",
    )(x, mod, w_qkv, w_z, w_ab, conv_w, alog, dtb)


def _gdn_core_kernel(q_ref, k_ref, v_ref, z_ref, gb_ref, gt_ref, nw_ref, o_ref,
                     state_ref, inv_ref, pw_ref, u_ref, w_ref, qk_ref, qg_ref, kd_ref, dl_ref, vn_ref,
                     oi_ref, *, chunk):
    rows, d = q_ref.shape
    nh = GDN_HEADS
    hd = d // nh
    n_chunks = rows // chunk
    heads = range(nh)
    lanes = [slice(hh * hd, (hh + 1) * hd) for hh in heads]

    @pl.when(pl.program_id(1) == 0)
    def _():
        state_ref[...] = jnp.zeros(state_ref.shape, F32)

    row = lax.broadcasted_iota(jnp.int32, (rows, rows), 0)
    col = lax.broadcasted_iota(jnp.int32, (rows, rows), 1)
    same_chunk = jnp.bitwise_xor(row, col) < chunk
    gb = gb_ref[...]
    for hh in heads:
        kh = k_ref[:, lanes[hh]]
        kf = kh.astype(F32)
        qh = q_ref[:, lanes[hh]]
        gc = jnp.broadcast_to(gb[:, hh:hh + 1], (rows, hd))
        beta = jnp.broadcast_to(gb[:, nh + hh:nh + hh + 1], (rows, hd))
        diff = jnp.broadcast_to(gb[:, hh:hh + 1], (rows, rows)) - gt_ref[hh:hh + 1, :]
        causal = same_chunk & (row >= col)
        decay = jnp.where(causal, jnp.exp(jnp.where(causal, diff, 0.0)), 0.0)
        kb = kf * beta
        a = jnp.where(row > col, _nt(kb.astype(BF16), kh) * decay, 0.0)
        inv_ref[hh] = jnp.where(row == col, 1.0, 0.0) - a
        pw_ref[hh] = a.astype(BF16)
        qk_ref[hh] = (_nt(qh, kh) * decay).astype(BF16)
        u_ref[hh] = v_ref[:, lanes[hh]].astype(F32) * beta
        w_ref[hh] = (kb * jnp.exp(gc)).astype(BF16)
        qg_ref[hh] = (qh.astype(F32) * jnp.exp(gc)).astype(BF16)
        for c in range(n_chunks):
            rs = slice(c * chunk, (c + 1) * chunk)
            g_last = gc[(c + 1) * chunk - 1:(c + 1) * chunk, :]
            kd_ref[hh, rs, :] = (kf[rs] * jnp.exp(g_last - gc[rs])).astype(BF16)
            dl_ref[hh, c] = jnp.exp(g_last)

    for _ in range(int(math.log2(chunk)) - 1):
        for hh in heads:
            power = _nn(pw_ref[hh], pw_ref[hh]).astype(BF16)
            pw_ref[hh] = power
            inv = inv_ref[hh]
            inv_ref[hh] = inv + _nn(inv.astype(BF16), power)
    for hh in heads:
        inv = inv_ref[hh].astype(BF16)
        u_ref[hh] = _nn(inv, u_ref[hh].astype(BF16))
        w_ref[hh] = _nn(inv, w_ref[hh]).astype(BF16)

    for c in range(n_chunks):
        rs = slice(c * chunk, (c + 1) * chunk)
        states = [state_ref[hh].astype(BF16) for hh in heads]
        v_new = [(u_ref[hh, rs, :] - _nn(w_ref[hh, rs, :], states[hh])).astype(BF16) for hh in heads]
        for hh in heads:
            vn_ref[hh, rs, :] = v_new[hh]
            state_ref[hh] = state_ref[hh] * dl_ref[hh, c] + _tn(kd_ref[hh, rs, :], v_new[hh])
            oi_ref[hh, rs, :] = _nn(qg_ref[hh, rs, :], states[hh])

    for hh in heads:
        o = oi_ref[hh] + _nn(qk_ref[hh], vn_ref[hh])
        o = o * lax.rsqrt(jnp.mean(o * o, axis=-1, keepdims=True) + 1e-6) * nw_ref[...]
        o_ref[:, lanes[hh]] = (o * _silu(z_ref[:, lanes[hh]].astype(F32))).astype(BF16)


def _gdn_core(q, k, v, z, gb, gt, norm_w, *, batch, seq, rows=256):
    t, d = q.shape
    n = seq // rows
    nh = GDN_HEADS
    hd = d // nh
    row = lambda b, j: (b * n + j, 0)
    return pl.pallas_call(
        functools.partial(_gdn_core_kernel, chunk=GDN_CHUNK),
        grid=(batch, n),
        in_specs=[pl.BlockSpec((rows, d), row)] * 4 + [
            pl.BlockSpec((rows, LANES), row),
            pl.BlockSpec((2 * nh, rows), lambda b, j: (0, b * n + j)),
            pl.BlockSpec((1, hd), lambda b, j: (0, 0)),
        ],
        out_specs=pl.BlockSpec((rows, d), row),
        out_shape=jax.ShapeDtypeStruct((t, d), BF16),
        scratch_shapes=[
            pltpu.VMEM((nh, hd, hd), F32),
            pltpu.VMEM((nh, rows, rows), F32),
            pltpu.VMEM((nh, rows, rows), BF16),
            pltpu.VMEM((nh, rows, hd), F32),
            pltpu.VMEM((nh, rows, hd), BF16),
            pltpu.VMEM((nh, rows, rows), BF16),
            pltpu.VMEM((nh, rows, hd), BF16),
            pltpu.VMEM((nh, rows, hd), BF16),
            pltpu.VMEM((nh, rows // GDN_CHUNK, 1, hd), F32),
            pltpu.VMEM((nh, rows, hd), BF16),
            pltpu.VMEM((nh, rows, hd), F32),
        ],
        compiler_params=_params("arbitrary", "arbitrary"),
        name="gdn_core",
    )(q, k, v, z, gb, gt, norm_w.reshape(1, hd))


def _rope_table_kernel(cos_ref, sin_ref):
    ts, half = cos_ref.shape
    pos = (lax.broadcasted_iota(jnp.int32, (ts, half), 0) + pl.program_id(0) * ts).astype(F32)
    frac = lax.broadcasted_iota(jnp.int32, (ts, half), 1).astype(F32) / (half - 1.0)
    ang = pos * jnp.exp(-frac * math.log(RET_ROPE_BASE))
    cos_ref[...] = jnp.cos(ang)
    sin_ref[...] = jnp.sin(ang)


def _rope_tables(seq, half, ts=256):
    spec = pl.BlockSpec((ts, half), lambda i: (i, 0))
    return pl.pallas_call(
        _rope_table_kernel,
        grid=(seq // ts,),
        out_specs=[spec, spec],
        out_shape=[jax.ShapeDtypeStruct((seq, half), F32)] * 2,
        compiler_params=_params("arbitrary"),
        name="rope_tables",
    )()


def _ret_in_kernel(x_ref, mod_ref, w_ref, cos_ref, sin_ref, q_ref, k_ref, v_ref, gate_ref):
    d = x_ref.shape[1]
    nh = RET_HEADS
    dk = d // nh
    half = dk // 2
    h = _modulate(x_ref[...], mod_ref, 0, 1)
    cos_a = cos_ref[...]
    sin_a = sin_ref[...]
    for s, (out, scale) in enumerate(((q_ref, 1.0), (k_ref, dk ** -0.5))):
        t = _nn(h, w_ref[:, s * d:(s + 1) * d])
        for hh in range(nh):
            t1 = t[:, hh * dk:hh * dk + half]
            t2 = t[:, hh * dk + half:(hh + 1) * dk]
            out[:, hh * dk:hh * dk + half] = ((t1 * cos_a - t2 * sin_a) * scale).astype(BF16)
            out[:, hh * dk + half:(hh + 1) * dk] = ((t1 * sin_a + t2 * cos_a) * scale).astype(BF16)
    v_ref[...] = _nn(h, w_ref[:, 2 * d:4 * d]).astype(BF16)
    gate_ref[...] = _nn(h, w_ref[:, 4 * d:6 * d])


def _ret_in(x, mod, w_in, cos_t, sin_t, *, seq, tm=256):
    t, d = x.shape
    tps = seq // tm
    half = cos_t.shape[1]
    row = lambda i: (i, 0)
    return pl.pallas_call(
        _ret_in_kernel,
        grid=(t // tm,),
        in_specs=[
            pl.BlockSpec((tm, d), row),
            pl.BlockSpec((1, 6, d), lambda i: (i // tps, 0, 0)),
            pl.BlockSpec((d, 6 * d), lambda i: (0, 0)),
            pl.BlockSpec((tm, half), lambda i: (i % tps, 0)),
            pl.BlockSpec((tm, half), lambda i: (i % tps, 0)),
        ],
        out_specs=[pl.BlockSpec((tm, d), row), pl.BlockSpec((tm, d), row),
                   pl.BlockSpec((tm, 2 * d), row), pl.BlockSpec((tm, 2 * d), row)],
        out_shape=[jax.ShapeDtypeStruct((t, d), BF16), jax.ShapeDtypeStruct((t, d), BF16),
                   jax.ShapeDtypeStruct((t, 2 * d), BF16), jax.ShapeDtypeStruct((t, 2 * d), F32)],
        compiler_params=_params("arbitrary"),
        name="ret_in",
    )(x, mod, w_in.astype(BF16), cos_t, sin_t)


def _ret_core_kernel(q_ref, k_ref, v_ref, gate_ref, o_ref, state_ref):
    c, d = q_ref.shape
    nh = RET_HEADS
    dk = d // nh
    dv = v_ref.shape[1] // nh

    @pl.when(pl.program_id(1) == 0)
    def _():
        state_ref[...] = jnp.zeros(state_ref.shape, F32)

    rel = (lax.broadcasted_iota(jnp.int32, (c, c), 0)
           - lax.broadcasted_iota(jnp.int32, (c, c), 1)).astype(F32)
    idx = lax.broadcasted_iota(jnp.int32, (c, 1), 0).astype(F32)
    for hh in range(nh):
        log_gamma = math.log(1.0 - 2.0 ** (-5.0 - hh))
        dmask = jnp.where(rel >= 0, jnp.exp(jnp.maximum(rel, 0.0) * log_gamma), 0.0)
        zeta = jnp.exp((c - 1.0 - idx) * log_gamma)
        xi = jnp.exp((idx + 1.0) * log_gamma)
        qh = q_ref[:, hh * dk:(hh + 1) * dk]
        kh = k_ref[:, hh * dk:(hh + 1) * dk]
        vh = v_ref[:, hh * dv:(hh + 1) * dv]
        scores = _nt(qh, kh) * dmask
        state = state_ref[hh]
        o = _nn(scores.astype(BF16), vh) + _nn(qh, state.astype(BF16)) * xi
        state_ref[hh] = (state * math.exp(c * log_gamma)
                         + _tn((kh.astype(F32) * zeta).astype(BF16), vh))
        mu = jnp.mean(o, axis=-1, keepdims=True)
        oc = o - mu
        var = jnp.mean(oc * oc, axis=-1, keepdims=True)
        o = oc * lax.rsqrt(var + 1e-6)
        o_ref[:, hh * dv:(hh + 1) * dv] = (o * _silu(gate_ref[:, hh * dv:(hh + 1) * dv])).astype(BF16)


def _ret_core(q, k, v, gate, *, batch, seq):
    t, d = q.shape
    c = RET_CHUNK
    n = seq // c
    dk = d // RET_HEADS
    dv = v.shape[1] // RET_HEADS
    row = lambda b, j: (b * n + j, 0)
    return pl.pallas_call(
        _ret_core_kernel,
        grid=(batch, n),
        in_specs=[pl.BlockSpec((c, d), row), pl.BlockSpec((c, d), row),
                  pl.BlockSpec((c, 2 * d), row), pl.BlockSpec((c, 2 * d), row)],
        out_specs=pl.BlockSpec((c, 2 * d), row),
        out_shape=jax.ShapeDtypeStruct((t, 2 * d), BF16),
        scratch_shapes=[pltpu.VMEM((RET_HEADS, dk, dv), F32)],
        compiler_params=_params("arbitrary", "arbitrary"),
        name="ret_core",
    )(q, k, v, gate)


def _erf(x):
    x = jnp.clip(x, -4.0, 4.0)
    x2 = x * x
    p = -2.72614225801306e-10
    for coef in (2.77068142495902e-08, -2.10102402082508e-06, -5.69250639462346e-05,
                 -7.34990630326855e-04, -2.95459980854025e-03, -1.60960333262415e-02):
        p = p * x2 + coef
    q = -1.45660718464996e-05
    for coef in (-2.13374055278905e-04, -1.68282697438203e-03, -7.37332916720468e-03,
                 -1.42647390514189e-02):
        q = q * x2 + coef
    return x * p / q


def _gmlp_kernel(x_ref, mod_ref, win_ref, lng_ref, lnb_ref, ws_ref, bs_ref, wout_ref, g_ref, b_ref,
                 out_ref, *, alpha):
    tm, d = x_ref.shape
    width = win_ref.shape[1] // 2
    c = GMLP_CHUNK
    ng = GMLP_GROUPS
    gw = width // ng
    x = x_ref[...]
    h = _modulate(x, mod_ref, 0, 1)

    def gelu(t):
        return 0.5 * t * (1.0 + _erf(t * (2.0 ** -0.5)))

    v = gelu(_nn(h, win_ref[:, width:]))
    v = _layer_norm_rows(v, lng_ref[...], lnb_ref[...], LN_EPS).astype(BF16)
    u = gelu(_nn(h, win_ref[:, :width]))
    row = lax.broadcasted_iota(jnp.int32, (c, c), 0)
    col = lax.broadcasted_iota(jnp.int32, (c, c), 1)
    parts = []
    for n in range(tm // c):
        rs = slice(n * c, (n + 1) * c)
        groups = []
        for gi in range(ng):
            ls = slice(gi * gw, (gi + 1) * gw)
            ws = jnp.where(row >= col, ws_ref[gi], 0.0).astype(BF16)
            vs = _nn(ws, v[rs, ls]) + bs_ref[:, gi:gi + 1]
            groups.append((u[rs, ls] * vs).astype(BF16))
        parts.append(jnp.concatenate(groups, axis=1))
    gated = jnp.concatenate(parts, axis=0)
    y = _nn(gated, wout_ref[...])
    r = alpha * x + (1.0 + mod_ref[0, 2:3, :]) * y
    out_ref[...] = _layer_norm_rows(r, g_ref[...], b_ref[...], LN_EPS)


def _gmlp(x, mod, w_in, ln_g, ln_b, w_s, b_s, w_out, res_g, res_b, *, seq, alpha, tm=256):
    t, d = x.shape
    width = w_out.shape[0]
    ng, c, _ = w_s.shape
    tps = seq // tm
    fixed = lambda i: (0, 0)
    return pl.pallas_call(
        functools.partial(_gmlp_kernel, alpha=alpha),
        grid=(t // tm,),
        in_specs=[
            pl.BlockSpec((tm, d), lambda i: (i, 0)),
            pl.BlockSpec((1, 6, d), lambda i: (i // tps, 0, 0)),
            pl.BlockSpec((d, 2 * width), fixed),
            pl.BlockSpec((1, width), fixed),
            pl.BlockSpec((1, width), fixed),
            pl.BlockSpec((ng, c, c), lambda i: (0, 0, 0)),
            pl.BlockSpec((c, ng), fixed),
            pl.BlockSpec((width, d), fixed),
            pl.BlockSpec((1, d), fixed),
            pl.BlockSpec((1, d), fixed),
        ],
        out_specs=pl.BlockSpec((tm, d), lambda i: (i, 0)),
        out_shape=jax.ShapeDtypeStruct((t, d), F32),
        compiler_params=_params("arbitrary"),
        name="gmlp",
    )(x, mod, w_in.astype(BF16), ln_g.reshape(1, width), ln_b.reshape(1, width), w_s, b_s.T,
      w_out.astype(BF16), res_g.reshape(1, d), res_b.reshape(1, d))


def _sb_in_kernel(x_ref, mod_ref, w_ref, q_ref, k_ref, v_ref, *, q_scale):
    d = x_ref.shape[1]
    h = _modulate(x_ref[...], mod_ref, 0, 1)
    q_ref[...] = (_nn(h, w_ref[:, 0:d]) * q_scale).astype(BF16)
    k_ref[...] = _nn(h, w_ref[:, d:2 * d]).astype(BF16)
    v_ref[...] = _nn(h, w_ref[:, 2 * d:3 * d]).astype(BF16)


def _sb_in(x, mod, w_in, *, seq, tm=512):
    t, d = x.shape
    tps = seq // tm
    row = lambda i: (i, 0)
    return pl.pallas_call(
        functools.partial(_sb_in_kernel, q_scale=(d // SB_HEADS) ** -0.5 * math.log2(math.e)),
        grid=(t // tm,),
        in_specs=[
            pl.BlockSpec((tm, d), row),
            pl.BlockSpec((1, 6, d), lambda i: (i // tps, 0, 0)),
            pl.BlockSpec((d, 3 * d), lambda i: (0, 0)),
        ],
        out_specs=[pl.BlockSpec((tm, d), row)] * 3,
        out_shape=[jax.ShapeDtypeStruct((t, d), BF16)] * 3,
        compiler_params=_params("arbitrary"),
        name="sb_in",
    )(x, mod, w_in.astype(BF16))


def _sb_core_kernel(q_ref, k_ref, v_ref, o_ref, qh_ref, sp_ref, zc_ref, carry_ref, acc_ref):
    tq, lanes = q_ref.shape
    dh = lanes // 2
    qi = pl.program_id(2)
    q = q_ref[...]
    lane = lax.broadcasted_iota(jnp.int32, (tq, lanes), 1)
    qh_ref[0] = jnp.where(lane < dh, q, jnp.zeros_like(q))
    qh_ref[1] = jnp.where(lane >= dh, q, jnp.zeros_like(q))
    acc_ref[...] = jnp.zeros(acc_ref.shape, F32)
    row = lax.broadcasted_iota(jnp.int32, (tq, tq), 0)
    col = lax.broadcasted_iota(jnp.int32, (tq, tq), 1)
    ones_lower = jnp.where(row >= col, 1.0, 0.0).astype(BF16)

    carry_ref[...] = jnp.zeros(carry_ref.shape, F32)

    def stage(kt, slot, diagonal=False):
        k = k_ref[pl.ds(pl.multiple_of(kt * tq, tq), tq), :]
        for hh in range(2):
            z = _nt(qh_ref[hh], k)
            sp = jnp.maximum(z, 0.0) + jnp.log2(1.0 + jnp.exp2(-jnp.abs(z)))
            carry = carry_ref[hh]
            zc = z - carry
            if diagonal:
                sp = jnp.where(col < row, sp, 0.0)
                zc = jnp.where(col < row, zc, -1e30)
            sp_ref[slot, hh] = sp.astype(BF16)
            zc_ref[slot, hh] = zc
            carry_ref[hh] = carry + jnp.sum(sp, axis=-1, keepdims=True)

    def consume(kt, slot):
        v = v_ref[pl.ds(pl.multiple_of(kt * tq, tq), tq), :]
        for hh in range(2):
            inclusive = _nn(sp_ref[slot, hh], ones_lower)
            a = jnp.exp2(zc_ref[slot, hh] - inclusive)
            acc_ref[hh] += _nn(a.astype(BF16), v)

    def step(kt, slot):
        stage(kt - 1, 1 - slot)
        consume(kt, slot)

    stage(qi, 0, diagonal=True)

    def pair(p, _):
        kt = qi - 2 * p
        step(kt, 0)
        step(kt - 1, 1)
        return 0

    lax.fori_loop(0, qi // 2, pair, 0)

    @pl.when(qi % 2 == 1)
    def _():
        step(1, 0)
        consume(0, 1)

    @pl.when(qi % 2 == 0)
    def _():
        consume(0, 0)

    o_ref[...] = jnp.where(lane < dh, acc_ref[0], acc_ref[1]).astype(BF16)


def _sb_core(q, k, v, *, batch, seq, tq=256):
    t, d = q.shape
    tq = min(tq, seq)
    nq = seq // tq
    groups = d // LANES
    return pl.pallas_call(
        _sb_core_kernel,
        grid=(batch, groups, nq),
        in_specs=[
            pl.BlockSpec((tq, LANES), lambda b, g, i: (b * nq + i, g)),
            pl.BlockSpec((seq, LANES), lambda b, g, i: (b, g)),
            pl.BlockSpec((seq, LANES), lambda b, g, i: (b, g)),
        ],
        out_specs=pl.BlockSpec((tq, LANES), lambda b, g, i: (b * nq + i, g)),
        out_shape=jax.ShapeDtypeStruct((t, d), BF16),
        scratch_shapes=[
            pltpu.VMEM((2, tq, LANES), BF16),
            pltpu.VMEM((2, 2, tq, tq), BF16),
            pltpu.VMEM((2, 2, tq, tq), F32),
            pltpu.VMEM((2, tq, 1), F32),
            pltpu.VMEM((2, tq, LANES), F32),
        ],
        compiler_params=_params("arbitrary", "arbitrary", "arbitrary"),
        name="sb_core",
    )(q, k, v)


def kernel(x, c, cond_w, cond_b, ada_w, ada_b, ln_g, ln_b, ffn_up, ffn_conv_w, ffn_conv_b, ffn_down,
           gdn_w_in, gdn_conv_w, gdn_a_log, gdn_dt_bias, gdn_norm_w, gdn_w_out,
           ret_w_in, ret_w_out,
           gmlp_w_in, gmlp_ln_g, gmlp_ln_b, gmlp_w_s, gmlp_b_s, gmlp_w_out,
           sb_w_in, sb_w_out):
    batch, seq, d = x.shape
    depth = ada_w.shape[0]
    alpha = (2.0 * depth) ** 0.25
    mods = _conditioning(c, cond_w, cond_b, ada_w, ada_b).reshape(depth, batch, 6, d)
    xt = x.reshape(batch * seq, d)

    for i in range(depth):
        mod = mods[i]
        res = dict(seq=seq, gate_row=2, alpha=alpha)
        mixer = i % 4
        if mixer == 0:
            q, k, v, z, gb, gt = _gdn_in(xt, mod, gdn_w_in, gdn_conv_w, gdn_a_log, gdn_dt_bias, seq=seq)
            o = _gdn_core(q, k, v, z, gb, gt, gdn_norm_w, batch=batch, seq=seq)
            xt = _out_proj_ln(o, gdn_w_out, xt, mod, ln_g[i, 0], ln_b[i, 0], **res)
        elif mixer == 1:
            cos_t, sin_t = _rope_tables(seq, d // RET_HEADS // 2)
            q, k, v, gate = _ret_in(xt, mod, ret_w_in, cos_t, sin_t, seq=seq)
            o = _ret_core(q, k, v, gate, batch=batch, seq=seq)
            xt = _out_proj_ln(o, ret_w_out, xt, mod, ln_g[i, 0], ln_b[i, 0], **res)
        elif mixer == 2:
            xt = _gmlp(xt, mod, gmlp_w_in, gmlp_ln_g, gmlp_ln_b, gmlp_w_s, gmlp_b_s, gmlp_w_out,
                       ln_g[i, 0], ln_b[i, 0], seq=seq, alpha=alpha)
        else:
            q, k, v = _sb_in(xt, mod, sb_w_in, seq=seq)
            o = _sb_core(q, k, v, batch=batch, seq=seq)
            xt = _out_proj_ln(o, sb_w_out, xt, mod, ln_g[i, 0], ln_b[i, 0], **res)
        xt = _ffn(xt, mod, ffn_up[i], ffn_conv_w[i], ffn_conv_b[i], ffn_down[i],
                  ln_g[i, 1], ln_b[i, 1], seq=seq, alpha=alpha)
    return xt.reshape(batch, seq, d)
```

```python
import functools
import math

import jax
import jax.numpy as jnp
from jax import lax
from jax.experimental import pallas as pl
from jax.experimental.pallas import tpu as pltpu

F32 = jnp.float32
BF16 = jnp.bfloat16

LANES = 128
SUBLANES = 8
VMEM_LIMIT = 56 * 1024 * 1024

LN_EPS = 1e-5
GDN_HEADS = 8
GDN_CHUNK = 64
GDN_CONV = 4
RET_HEADS = 4
RET_CHUNK = 128
RET_ROPE_BASE = 10000.0
GMLP_CHUNK = 128
GMLP_GROUPS = 8
SB_HEADS = 16
FFN_CONV = 3
DEAD_EXPONENT = 160.0
NORM_BOUND_SLACK = 1.001


def _params(*sem):
    return pltpu.CompilerParams(dimension_semantics=sem, vmem_limit_bytes=VMEM_LIMIT)


def _nn(a, b):
    return jnp.dot(a, b, preferred_element_type=F32)


def _nt(a, b):
    return lax.dot_general(a, b, (((1,), (1,)), ((), ())), preferred_element_type=F32)


def _tn(a, b):
    return lax.dot_general(a, b, (((0,), (0,)), ((), ())), preferred_element_type=F32)


def _sigmoid(x):
    return 1.0 / (1.0 + jnp.exp(-x))


def _silu(x):
    return x * _sigmoid(x)


def _softplus(x):
    return jnp.maximum(x, 0.0) + jnp.log(1.0 + jnp.exp(-jnp.abs(x)))


def _split3(x):
    hi = x.astype(BF16)
    r = x - hi.astype(F32)
    mid = r.astype(BF16)
    lo = (r - mid.astype(F32)).astype(BF16)
    return hi, mid, lo


def _modulate(x, mod_ref, shift_row, scale_row):
    return (x * (1.0 + mod_ref[0, scale_row:scale_row + 1, :])
            + mod_ref[0, shift_row:shift_row + 1, :]).astype(BF16)


def _layer_norm_rows(r, g, b, eps):
    mu = jnp.mean(r, axis=-1, keepdims=True)
    rc = r - mu
    var = jnp.mean(rc * rc, axis=-1, keepdims=True)
    return rc * lax.rsqrt(var + eps) * g + b


def _lane_replicated_columns(rows):
    n, k = rows.shape
    padded = jnp.concatenate([rows, jnp.zeros((LANES - n, k), F32)], axis=0)
    t = padded.T
    return [jnp.broadcast_to(t[:, b:b + 1], (k, LANES)) for b in range(n)]


def _rowvec_matmul(col, w):
    n = w.shape[1]
    parts = [jnp.sum(col * w[:, c:c + LANES], axis=0, keepdims=True) for c in range(0, n, LANES)]
    return jnp.concatenate(parts, axis=1)


def _cond_kernel(c_ref, cw_ref, cb_ref, aw_ref, ab_ref, o_ref, ecol_ref):
    nb = c_ref.shape[0]

    @pl.when((pl.program_id(0) == 0) & (pl.program_id(1) == 0))
    def _():
        ccols = _lane_replicated_columns(c_ref[...])
        cw = cw_ref[...]
        e = jnp.concatenate([_rowvec_matmul(col, cw) for col in ccols], axis=0) + cb_ref[...]
        ecols = _lane_replicated_columns(_silu(e))
        for b in range(nb):
            ecol_ref[b] = ecols[b]

    w = aw_ref[0]
    for b in range(nb):
        o_ref[0, b:b + 1, :] = _rowvec_matmul(ecol_ref[b], w) + ab_ref[0]


def _conditioning(c, cond_w, cond_b, ada_w, ada_b):
    nb, d = c.shape
    depth, _, n6 = ada_w.shape
    tn = 512
    return pl.pallas_call(
        _cond_kernel,
        grid=(depth, n6 // tn),
        in_specs=[
            pl.BlockSpec((nb, d), lambda i, j: (0, 0)),
            pl.BlockSpec((d, d), lambda i, j: (0, 0)),
            pl.BlockSpec((1, d), lambda i, j: (0, 0)),
            pl.BlockSpec((1, d, tn), lambda i, j: (i, 0, j)),
            pl.BlockSpec((1, 1, tn), lambda i, j: (i, 0, j)),
        ],
        out_specs=pl.BlockSpec((1, nb, tn), lambda i, j: (i, 0, j)),
        out_shape=jax.ShapeDtypeStruct((depth, nb, n6), F32),
        scratch_shapes=[pltpu.VMEM((nb, d, LANES), F32)],
        compiler_params=_params("arbitrary", "arbitrary"),
        name="conditioning",
    )(c, cond_w, cond_b.reshape(1, d), ada_w, ada_b.reshape(depth, 1, n6))


def _out_ln_kernel(o_ref, w_ref, x_ref, mod_ref, g_ref, b_ref, out_ref, *, gate_row, alpha):
    y = _nn(o_ref[...], w_ref[...])
    r = alpha * x_ref[...] + (1.0 + mod_ref[0, gate_row:gate_row + 1, :]) * y
    out_ref[...] = _layer_norm_rows(r, g_ref[...], b_ref[...], LN_EPS)


def _out_proj_ln(o, w_out, x, mod, ln_g, ln_b, *, seq, gate_row, alpha, tm=512):
    t, kdim = o.shape
    d = x.shape[1]
    tps = seq // tm
    return pl.pallas_call(
        functools.partial(_out_ln_kernel, gate_row=gate_row, alpha=alpha),
        grid=(t // tm,),
        in_specs=[
            pl.BlockSpec((tm, kdim), lambda i: (i, 0)),
            pl.BlockSpec((kdim, d), lambda i: (0, 0)),
            pl.BlockSpec((tm, d), lambda i: (i, 0)),
            pl.BlockSpec((1, 6, d), lambda i: (i // tps, 0, 0)),
            pl.BlockSpec((1, d), lambda i: (0, 0)),
            pl.BlockSpec((1, d), lambda i: (0, 0)),
        ],
        out_specs=pl.BlockSpec((tm, d), lambda i: (i, 0)),
        out_shape=jax.ShapeDtypeStruct((t, d), F32),
        compiler_params=_params("arbitrary"),
        name="out_proj_ln",
    )(o, w_out.astype(BF16), x, mod, ln_g.reshape(1, d), ln_b.reshape(1, d))


def _ffn_kernel(x_ref, mod_ref, wg_ref, wu_ref, cw_ref, cb_ref, wd_ref, g_ref, b_ref, out_ref,
                buf_ref, carry_ref, acc_ref, *, tiles_per_seq, alpha):
    i = pl.program_id(0)
    j = pl.program_id(1)
    tm = x_ref.shape[0]
    x = x_ref[...]
    h = _modulate(x, mod_ref, 3, 4)

    @pl.when(i % tiles_per_seq == 0)
    def _():
        carry_ref[j] = jnp.zeros(carry_ref.shape[1:], F32)

    buf_ref[0:SUBLANES, :] = carry_ref[j]
    buf_ref[SUBLANES:SUBLANES + tm, :] = _nn(h, wg_ref[...])
    conv = cb_ref[...]
    for tap in range(FFN_CONV):
        off = SUBLANES - (FFN_CONV - 1) + tap
        conv = conv + cw_ref[tap:tap + 1, :] * buf_ref[off:off + tm, :]
    carry_ref[j] = buf_ref[tm:tm + SUBLANES, :]
    act = (_silu(conv) * _nn(h, wu_ref[...])).astype(BF16)
    part = _nn(act, wd_ref[...])

    @pl.when(j == 0)
    def _():
        acc_ref[...] = part

    @pl.when(j > 0)
    def _():
        acc_ref[...] += part

    @pl.when(j == pl.num_programs(1) - 1)
    def _():
        r = alpha * x + (1.0 + mod_ref[0, 5:6, :]) * acc_ref[...]
        out_ref[...] = _layer_norm_rows(r, g_ref[...], b_ref[...], LN_EPS)


def _ffn(x, mod, w_up, conv_w, conv_b, w_down, ln_g, ln_b, *, seq, alpha, tm=512):
    t, d = x.shape
    f = w_down.shape[0]
    nf = 2
    tf = f // nf
    tps = seq // tm
    w_up_b = w_up.astype(BF16)
    return pl.pallas_call(
        functools.partial(_ffn_kernel, tiles_per_seq=tps, alpha=alpha),
        grid=(t // tm, nf),
        in_specs=[
            pl.BlockSpec((tm, d), lambda i, j: (i, 0)),
            pl.BlockSpec((1, 6, d), lambda i, j: (i // tps, 0, 0)),
            pl.BlockSpec((d, tf), lambda i, j: (0, j)),
            pl.BlockSpec((d, tf), lambda i, j: (0, nf + j)),
            pl.BlockSpec((FFN_CONV, tf), lambda i, j: (0, j)),
            pl.BlockSpec((1, tf), lambda i, j: (0, j)),
            pl.BlockSpec((tf, d), lambda i, j: (j, 0)),
            pl.BlockSpec((1, d), lambda i, j: (0, 0)),
            pl.BlockSpec((1, d), lambda i, j: (0, 0)),
        ],
        out_specs=pl.BlockSpec((tm, d), lambda i, j: (i, 0)),
        out_shape=jax.ShapeDtypeStruct((t, d), F32),
        scratch_shapes=[
            pltpu.VMEM((tm + SUBLANES, tf), F32),
            pltpu.VMEM((nf, SUBLANES, tf), F32),
            pltpu.VMEM((tm, d), F32),
        ],
        compiler_params=_params("arbitrary", "arbitrary"),
        name="ffn",
    )(x, mod, w_up_b, w_up_b, conv_w, conv_b.reshape(1, f), w_down.astype(BF16),
      ln_g.reshape(1, d), ln_b.reshape(1, d))


def _gdn_in_kernel(x_ref, mod_ref, wqkv_ref, wz_ref, wab_ref, cw_ref, alog_ref, dtb_ref,
                   q_ref, k_ref, v_ref, z_ref, gb_ref, gt_ref, buf_ref, *, tiles_per_seq, chunk):
    i = pl.program_id(0)
    tm, d = x_ref.shape
    nh = GDN_HEADS
    hd = d // nh
    h = _modulate(x_ref[...], mod_ref, 0, 1)

    @pl.when(i % tiles_per_seq == 0)
    def _():
        buf_ref[0:SUBLANES, :] = jnp.zeros((SUBLANES, buf_ref.shape[1]), F32)

    outs = (q_ref, k_ref, v_ref)
    for s in range(3):
        cs = slice(s * d, (s + 1) * d)
        buf_ref[SUBLANES:SUBLANES + tm, cs] = _nn(h, wqkv_ref[:, cs])
        y = None
        for tap in range(GDN_CONV):
            off = SUBLANES - (GDN_CONV - 1) + tap
            term = cw_ref[tap:tap + 1, cs] * buf_ref[off:off + tm, cs]
            y = term if y is None else y + term
        y = _silu(y)
        if s < 2:
            scale = hd ** -0.5 if s == 0 else 1.0
            for hh in range(nh):
                seg = y[:, hh * hd:(hh + 1) * hd]
                inv = lax.rsqrt(jnp.sum(seg * seg, axis=-1, keepdims=True) + 1e-6)
                outs[s][:, hh * hd:(hh + 1) * hd] = (seg * (inv * scale)).astype(BF16)
        else:
            outs[s][...] = y.astype(BF16)
    buf_ref[0:SUBLANES, :] = buf_ref[tm:tm + SUBLANES, :]

    z_ref[...] = _nn(h, wz_ref[...]).astype(BF16)

    pab = _nn(h, wab_ref[...])
    lane = lax.broadcasted_iota(jnp.int32, pab.shape, 1)
    g = -jnp.exp(alog_ref[...]) * _softplus(pab + dtb_ref[...])
    gb = jnp.where(lane < nh, g, jnp.where(lane < 2 * nh, _sigmoid(pab), 0.0))
    row = lax.broadcasted_iota(jnp.int32, (tm, tm), 0)
    col = lax.broadcasted_iota(jnp.int32, (tm, tm), 1)
    tri = jnp.where(col <= row, jnp.where(jnp.bitwise_xor(row, col) < chunk, 1.0, 0.0), 0.0).astype(BF16)
    hi, mid, lo = _split3(gb)
    cum = _nn(tri, hi) + _nn(tri, mid) + _nn(tri, lo)
    gb = jnp.where(lane < nh, cum, gb)
    gb_ref[...] = gb
    gt_ref[...] = gb.T[0:2 * nh, :]


def _gdn_in(x, mod, w_in, conv_w, a_log, dt_bias, *, seq, tm=256):
    t, d = x.shape
    nh = GDN_HEADS
    tps = seq // tm
    w_b = w_in.astype(BF16)
    w_qkv = w_b[:, :3 * d]
    w_z = w_b[:, 3 * d:4 * d]
    w_ab = jnp.pad(w_b[:, 4 * d:], ((0, 0), (0, LANES - 2 * nh)))
    alog = jnp.pad(a_log, (0, LANES - nh)).reshape(1, LANES)
    dtb = jnp.pad(dt_bias, (0, LANES - nh)).reshape(1, LANES)
    row = lambda i: (i, 0)
    fixed = lambda i: (0, 0)
    return pl.pallas_call(
        functools.partial(_gdn_in_kernel, tiles_per_seq=tps, chunk=GDN_CHUNK),
        grid=(t // tm,),
        in_specs=[
            pl.BlockSpec((tm, d), row),
            pl.BlockSpec((1, 6, d), lambda i: (i // tps, 0, 0)),
            pl.BlockSpec((d, 3 * d), fixed),
            pl.BlockSpec((d, d), fixed),
            pl.BlockSpec((d, LANES), fixed),
            pl.BlockSpec((GDN_CONV, 3 * d), fixed),
            pl.BlockSpec((1, LANES), fixed),
            pl.BlockSpec((1, LANES), fixed),
        ],
        out_specs=[pl.BlockSpec((tm, d), row)] * 4 + [pl.BlockSpec((tm, LANES), row),
                                                      pl.BlockSpec((2 * nh, tm), lambda i: (0, i))],
        out_shape=[jax.ShapeDtypeStruct((t, d), BF16)] * 4 + [jax.ShapeDtypeStruct((t, LANES), F32),
                                                              jax.ShapeDtypeStruct((2 * nh, t), F32)],
        scratch_shapes=[pltpu.VMEM((tm + SUBLANES, 3 * d), F32)],
        compiler_params=_params("arbitrary"),
        name="gdn_in",
    )(x, mod, w_qkv, w_z, w_ab, conv_w, alog, dtb)


def _gdn_core_kernel(q_ref, k_ref, v_ref, z_ref, gb_ref, gt_ref, nw_ref, o_ref,
                     state_ref, inv_ref, pw_ref, u_ref, w_ref, qk_ref, qg_ref, kd_ref, dl_ref, vn_ref,
                     oi_ref, *, chunk):
    rows, d = q_ref.shape
    nh = GDN_HEADS
    hd = d // nh
    n_chunks = rows // chunk
    heads = range(nh)
    lanes = [slice(hh * hd, (hh + 1) * hd) for hh in heads]

    @pl.when(pl.program_id(1) == 0)
    def _():
        state_ref[...] = jnp.zeros(state_ref.shape, F32)

    row = lax.broadcasted_iota(jnp.int32, (rows, rows), 0)
    col = lax.broadcasted_iota(jnp.int32, (rows, rows), 1)
    same_chunk = jnp.bitwise_xor(row, col) < chunk
    gb = gb_ref[...]
    for hh in heads:
        kh = k_ref[:, lanes[hh]]
        kf = kh.astype(F32)
        qh = q_ref[:, lanes[hh]]
        gc = jnp.broadcast_to(gb[:, hh:hh + 1], (rows, hd))
        beta = jnp.broadcast_to(gb[:, nh + hh:nh + hh + 1], (rows, hd))
        diff = jnp.broadcast_to(gb[:, hh:hh + 1], (rows, rows)) - gt_ref[hh:hh + 1, :]
        causal = same_chunk & (row >= col)
        decay = jnp.where(causal, jnp.exp(jnp.where(causal, diff, 0.0)), 0.0)
        kb = kf * beta
        a = jnp.where(row > col, _nt(kb.astype(BF16), kh) * decay, 0.0)
        inv_ref[hh] = jnp.where(row == col, 1.0, 0.0) - a
        pw_ref[hh] = a.astype(BF16)
        qk_ref[hh] = (_nt(qh, kh) * decay).astype(BF16)
        u_ref[hh] = v_ref[:, lanes[hh]].astype(F32) * beta
        w_ref[hh] = (kb * jnp.exp(gc)).astype(BF16)
        qg_ref[hh] = (qh.astype(F32) * jnp.exp(gc)).astype(BF16)
        for c in range(n_chunks):
            rs = slice(c * chunk, (c + 1) * chunk)
            g_last = gc[(c + 1) * chunk - 1:(c + 1) * chunk, :]
            kd_ref[hh, rs, :] = (kf[rs] * jnp.exp(g_last - gc[rs])).astype(BF16)
            dl_ref[hh, c] = jnp.exp(g_last)

    for _ in range(int(math.log2(chunk)) - 1):
        for hh in heads:
            power = _nn(pw_ref[hh], pw_ref[hh]).astype(BF16)
            pw_ref[hh] = power
            inv = inv_ref[hh]
            inv_ref[hh] = inv + _nn(inv.astype(BF16), power)
    for hh in heads:
        inv = inv_ref[hh].astype(BF16)
        u_ref[hh] = _nn(inv, u_ref[hh].astype(BF16))
        w_ref[hh] = _nn(inv, w_ref[hh]).astype(BF16)

    for c in range(n_chunks):
        rs = slice(c * chunk, (c + 1) * chunk)
        states = [state_ref[hh].astype(BF16) for hh in heads]
        v_new = [(u_ref[hh, rs, :] - _nn(w_ref[hh, rs, :], states[hh])).astype(BF16) for hh in heads]
        for hh in heads:
            vn_ref[hh, rs, :] = v_new[hh]
            state_ref[hh] = state_ref[hh] * dl_ref[hh, c] + _tn(kd_ref[hh, rs, :], v_new[hh])
            oi_ref[hh, rs, :] = _nn(qg_ref[hh, rs, :], states[hh])

    for hh in heads:
        o = oi_ref[hh] + _nn(qk_ref[hh], vn_ref[hh])
        o = o * lax.rsqrt(jnp.mean(o * o, axis=-1, keepdims=True) + 1e-6) * nw_ref[...]
        o_ref[:, lanes[hh]] = (o * _silu(z_ref[:, lanes[hh]].astype(F32))).astype(BF16)


def _gdn_core(q, k, v, z, gb, gt, norm_w, *, batch, seq, rows=256):
    t, d = q.shape
    n = seq // rows
    nh = GDN_HEADS
    hd = d // nh
    row = lambda b, j: (b * n + j, 0)
    return pl.pallas_call(
        functools.partial(_gdn_core_kernel, chunk=GDN_CHUNK),
        grid=(batch, n),
        in_specs=[pl.BlockSpec((rows, d), row)] * 4 + [
            pl.BlockSpec((rows, LANES), row),
            pl.BlockSpec((2 * nh, rows), lambda b, j: (0, b * n + j)),
            pl.BlockSpec((1, hd), lambda b, j: (0, 0)),
        ],
        out_specs=pl.BlockSpec((rows, d), row),
        out_shape=jax.ShapeDtypeStruct((t, d), BF16),
        scratch_shapes=[
            pltpu.VMEM((nh, hd, hd), F32),
            pltpu.VMEM((nh, rows, rows), F32),
            pltpu.VMEM((nh, rows, rows), BF16),
            pltpu.VMEM((nh, rows, hd), F32),
            pltpu.VMEM((nh, rows, hd), BF16),
            pltpu.VMEM((nh, rows, rows), BF16),
            pltpu.VMEM((nh, rows, hd), BF16),
            pltpu.VMEM((nh, rows, hd), BF16),
            pltpu.VMEM((nh, rows // GDN_CHUNK, 1, hd), F32),
            pltpu.VMEM((nh, rows, hd), BF16),
            pltpu.VMEM((nh, rows, hd), F32),
        ],
        compiler_params=_params("arbitrary", "arbitrary"),
        name="gdn_core",
    )(q, k, v, z, gb, gt, norm_w.reshape(1, hd))


def _rope_table_kernel(cos_ref, sin_ref):
    ts, half = cos_ref.shape
    pos = (lax.broadcasted_iota(jnp.int32, (ts, half), 0) + pl.program_id(0) * ts).astype(F32)
    frac = lax.broadcasted_iota(jnp.int32, (ts, half), 1).astype(F32) / (half - 1.0)
    ang = pos * jnp.exp(-frac * math.log(RET_ROPE_BASE))
    cos_ref[...] = jnp.cos(ang)
    sin_ref[...] = jnp.sin(ang)


def _rope_tables(seq, half, ts=256):
    spec = pl.BlockSpec((ts, half), lambda i: (i, 0))
    return pl.pallas_call(
        _rope_table_kernel,
        grid=(seq // ts,),
        out_specs=[spec, spec],
        out_shape=[jax.ShapeDtypeStruct((seq, half), F32)] * 2,
        compiler_params=_params("arbitrary"),
        name="rope_tables",
    )()


def _ret_in_kernel(x_ref, mod_ref, w_ref, cos_ref, sin_ref, q_ref, k_ref, v_ref, gate_ref):
    d = x_ref.shape[1]
    nh = RET_HEADS
    dk = d // nh
    half = dk // 2
    h = _modulate(x_ref[...], mod_ref, 0, 1)
    cos_a = cos_ref[...]
    sin_a = sin_ref[...]
    for s, (out, scale) in enumerate(((q_ref, 1.0), (k_ref, dk ** -0.5))):
        t = _nn(h, w_ref[:, s * d:(s + 1) * d])
        for hh in range(nh):
            t1 = t[:, hh * dk:hh * dk + half]
            t2 = t[:, hh * dk + half:(hh + 1) * dk]
            out[:, hh * dk:hh * dk + half] = ((t1 * cos_a - t2 * sin_a) * scale).astype(BF16)
            out[:, hh * dk + half:(hh + 1) * dk] = ((t1 * sin_a + t2 * cos_a) * scale).astype(BF16)
    v_ref[...] = _nn(h, w_ref[:, 2 * d:4 * d]).astype(BF16)
    gate_ref[...] = _nn(h, w_ref[:, 4 * d:6 * d])


def _ret_in(x, mod, w_in, cos_t, sin_t, *, seq, tm=256):
    t, d = x.shape
    tps = seq // tm
    half = cos_t.shape[1]
    row = lambda i: (i, 0)
    return pl.pallas_call(
        _ret_in_kernel,
        grid=(t // tm,),
        in_specs=[
            pl.BlockSpec((tm, d), row),
            pl.BlockSpec((1, 6, d), lambda i: (i // tps, 0, 0)),
            pl.BlockSpec((d, 6 * d), lambda i: (0, 0)),
            pl.BlockSpec((tm, half), lambda i: (i % tps, 0)),
            pl.BlockSpec((tm, half), lambda i: (i % tps, 0)),
        ],
        out_specs=[pl.BlockSpec((tm, d), row), pl.BlockSpec((tm, d), row),
                   pl.BlockSpec((tm, 2 * d), row), pl.BlockSpec((tm, 2 * d), row)],
        out_shape=[jax.ShapeDtypeStruct((t, d), BF16), jax.ShapeDtypeStruct((t, d), BF16),
                   jax.ShapeDtypeStruct((t, 2 * d), BF16), jax.ShapeDtypeStruct((t, 2 * d), F32)],
        compiler_params=_params("arbitrary"),
        name="ret_in",
    )(x, mod, w_in.astype(BF16), cos_t, sin_t)


def _ret_core_kernel(q_ref, k_ref, v_ref, gate_ref, o_ref, state_ref):
    c, d = q_ref.shape
    nh = RET_HEADS
    dk = d // nh
    dv = v_ref.shape[1] // nh

    @pl.when(pl.program_id(1) == 0)
    def _():
        state_ref[...] = jnp.zeros(state_ref.shape, F32)

    rel = (lax.broadcasted_iota(jnp.int32, (c, c), 0)
           - lax.broadcasted_iota(jnp.int32, (c, c), 1)).astype(F32)
    idx = lax.broadcasted_iota(jnp.int32, (c, 1), 0).astype(F32)
    for hh in range(nh):
        log_gamma = math.log(1.0 - 2.0 ** (-5.0 - hh))
        dmask = jnp.where(rel >= 0, jnp.exp(jnp.maximum(rel, 0.0) * log_gamma), 0.0)
        zeta = jnp.exp((c - 1.0 - idx) * log_gamma)
        xi = jnp.exp((idx + 1.0) * log_gamma)
        qh = q_ref[:, hh * dk:(hh + 1) * dk]
        kh = k_ref[:, hh * dk:(hh + 1) * dk]
        vh = v_ref[:, hh * dv:(hh + 1) * dv]
        scores = _nt(qh, kh) * dmask
        state = state_ref[hh]
        o = _nn(scores.astype(BF16), vh) + _nn(qh, state.astype(BF16)) * xi
        state_ref[hh] = (state * math.exp(c * log_gamma)
                         + _tn((kh.astype(F32) * zeta).astype(BF16), vh))
        mu = jnp.mean(o, axis=-1, keepdims=True)
        oc = o - mu
        var = jnp.mean(oc * oc, axis=-1, keepdims=True)
        o = oc * lax.rsqrt(var + 1e-6)
        o_ref[:, hh * dv:(hh + 1) * dv] = (o * _silu(gate_ref[:, hh * dv:(hh + 1) * dv])).astype(BF16)


def _ret_core(q, k, v, gate, *, batch, seq):
    t, d = q.shape
    c = RET_CHUNK
    n = seq // c
    dk = d // RET_HEADS
    dv = v.shape[1] // RET_HEADS
    row = lambda b, j: (b * n + j, 0)
    return pl.pallas_call(
        _ret_core_kernel,
        grid=(batch, n),
        in_specs=[pl.BlockSpec((c, d), row), pl.BlockSpec((c, d), row),
                  pl.BlockSpec((c, 2 * d), row), pl.BlockSpec((c, 2 * d), row)],
        out_specs=pl.BlockSpec((c, 2 * d), row),
        out_shape=jax.ShapeDtypeStruct((t, 2 * d), BF16),
        scratch_shapes=[pltpu.VMEM((RET_HEADS, dk, dv), F32)],
        compiler_params=_params("arbitrary", "arbitrary"),
        name="ret_core",
    )(q, k, v, gate)


def _erf(x):
    x = jnp.clip(x, -4.0, 4.0)
    x2 = x * x
    p = -2.72614225801306e-10
    for coef in (2.77068142495902e-08, -2.10102402082508e-06, -5.69250639462346e-05,
                 -7.34990630326855e-04, -2.95459980854025e-03, -1.60960333262415e-02):
        p = p * x2 + coef
    q = -1.45660718464996e-05
    for coef in (-2.13374055278905e-04, -1.68282697438203e-03, -7.37332916720468e-03,
                 -1.42647390514189e-02):
        q = q * x2 + coef
    return x * p / q


def _gmlp_kernel(x_ref, mod_ref, win_ref, lng_ref, lnb_ref, ws_ref, bs_ref, wout_ref, g_ref, b_ref,
                 out_ref, *, alpha):
    tm, d = x_ref.shape
    width = win_ref.shape[1] // 2
    c = GMLP_CHUNK
    ng = GMLP_GROUPS
    gw = width // ng
    x = x_ref[...]
    h = _modulate(x, mod_ref, 0, 1)

    def gelu(t):
        return 0.5 * t * (1.0 + _erf(t * (2.0 ** -0.5)))

    v = gelu(_nn(h, win_ref[:, width:]))
    v = _layer_norm_rows(v, lng_ref[...], lnb_ref[...], LN_EPS).astype(BF16)
    u = gelu(_nn(h, win_ref[:, :width]))
    row = lax.broadcasted_iota(jnp.int32, (c, c), 0)
    col = lax.broadcasted_iota(jnp.int32, (c, c), 1)
    parts = []
    for n in range(tm // c):
        rs = slice(n * c, (n + 1) * c)
        groups = []
        for gi in range(ng):
            ls = slice(gi * gw, (gi + 1) * gw)
            ws = jnp.where(row >= col, ws_ref[gi], 0.0).astype(BF16)
            vs = _nn(ws, v[rs, ls]) + bs_ref[:, gi:gi + 1]
            groups.append((u[rs, ls] * vs).astype(BF16))
        parts.append(jnp.concatenate(groups, axis=1))
    gated = jnp.concatenate(parts, axis=0)
    y = _nn(gated, wout_ref[...])
    r = alpha * x + (1.0 + mod_ref[0, 2:3, :]) * y
    out_ref[...] = _layer_norm_rows(r, g_ref[...], b_ref[...], LN_EPS)


def _gmlp(x, mod, w_in, ln_g, ln_b, w_s, b_s, w_out, res_g, res_b, *, seq, alpha, tm=256):
    t, d = x.shape
    width = w_out.shape[0]
    ng, c, _ = w_s.shape
    tps = seq // tm
    fixed = lambda i: (0, 0)
    return pl.pallas_call(
        functools.partial(_gmlp_kernel, alpha=alpha),
        grid=(t // tm,),
        in_specs=[
            pl.BlockSpec((tm, d), lambda i: (i, 0)),
            pl.BlockSpec((1, 6, d), lambda i: (i // tps, 0, 0)),
            pl.BlockSpec((d, 2 * width), fixed),
            pl.BlockSpec((1, width), fixed),
            pl.BlockSpec((1, width), fixed),
            pl.BlockSpec((ng, c, c), lambda i: (0, 0, 0)),
            pl.BlockSpec((c, ng), fixed),
            pl.BlockSpec((width, d), fixed),
            pl.BlockSpec((1, d), fixed),
            pl.BlockSpec((1, d), fixed),
        ],
        out_specs=pl.BlockSpec((tm, d), lambda i: (i, 0)),
        out_shape=jax.ShapeDtypeStruct((t, d), F32),
        compiler_params=_params("arbitrary"),
        name="gmlp",
    )(x, mod, w_in.astype(BF16), ln_g.reshape(1, width), ln_b.reshape(1, width), w_s, b_s.T,
      w_out.astype(BF16), res_g.reshape(1, d), res_b.reshape(1, d))


def _sb_in_kernel(x_ref, mod_ref, w_ref, q_ref, k_ref, v_ref, *, q_scale):
    d = x_ref.shape[1]
    h = _modulate(x_ref[...], mod_ref, 0, 1)
    q_ref[...] = (_nn(h, w_ref[:, 0:d]) * q_scale).astype(BF16)
    k_ref[...] = _nn(h, w_ref[:, d:2 * d]).astype(BF16)
    v_ref[...] = _nn(h, w_ref[:, 2 * d:3 * d]).astype(BF16)


def _sb_in(x, mod, w_in, *, seq, tm=512):
    t, d = x.shape
    tps = seq // tm
    row = lambda i: (i, 0)
    return pl.pallas_call(
        functools.partial(_sb_in_kernel, q_scale=(d // SB_HEADS) ** -0.5 * math.log2(math.e)),
        grid=(t // tm,),
        in_specs=[
            pl.BlockSpec((tm, d), row),
            pl.BlockSpec((1, 6, d), lambda i: (i // tps, 0, 0)),
            pl.BlockSpec((d, 3 * d), lambda i: (0, 0)),
        ],
        out_specs=[pl.BlockSpec((tm, d), row)] * 3,
        out_shape=[jax.ShapeDtypeStruct((t, d), BF16)] * 3,
        compiler_params=_params("arbitrary"),
        name="sb_in",
    )(x, mod, w_in.astype(BF16))


def _sb_core_kernel(q_ref, k_ref, v_ref, o_ref, qh_ref, sp_ref, zc_ref, carry_ref, acc_ref, knorm_ref,
                    bound_ref):
    tq, lanes = q_ref.shape
    seq = k_ref.shape[0]
    dh = lanes // 2
    qi = pl.program_id(2)
    q = q_ref[...]
    lane = lax.broadcasted_iota(jnp.int32, (tq, lanes), 1)
    head_lanes = (lane < dh, lane >= dh)
    row = lax.broadcasted_iota(jnp.int32, (tq, tq), 0)
    col = lax.broadcasted_iota(jnp.int32, (tq, tq), 1)
    ones_lower = jnp.where(row >= col, 1.0, 0.0).astype(BF16)

    @pl.when(qi == 0)
    def _():
        def key_tile(t, best):
            kf = k_ref[pl.ds(pl.multiple_of(t * tq, tq), tq), :].astype(F32)
            ksq = kf * kf
            sums = [jnp.sum(jnp.where(m, ksq, 0.0), axis=-1, keepdims=True) for m in head_lanes]
            return tuple(jnp.maximum(b, jnp.max(s, axis=0, keepdims=True)) for b, s in zip(best, sums))

        zero = jnp.zeros((1, 1), F32)
        best = lax.fori_loop(0, seq // tq, key_tile, (zero, zero))
        for hh in range(2):
            knorm_ref[hh] = jnp.broadcast_to(best[hh], knorm_ref.shape[1:])

    qf = q.astype(F32)
    for hh in range(2):
        qh_ref[hh] = jnp.where(head_lanes[hh], q, jnp.zeros_like(q))
        qsq = jnp.sum(jnp.where(head_lanes[hh], qf * qf, 0.0), axis=-1, keepdims=True)
        bound_ref[hh] = jnp.sqrt(qsq * knorm_ref[hh, 0:1, 0:1]) * NORM_BOUND_SLACK
    acc_ref[...] = jnp.zeros(acc_ref.shape, F32)
    carry_ref[...] = jnp.zeros(carry_ref.shape, F32)

    def stage(kt, slot, diagonal=False):
        k = k_ref[pl.ds(pl.multiple_of(kt * tq, tq), tq), :]
        for hh in range(2):
            z = _nt(qh_ref[hh], k)
            sp = jnp.maximum(z, 0.0) + jnp.log2(1.0 + jnp.exp2(-jnp.abs(z)))
            carry = carry_ref[hh]
            zc = z - carry
            if diagonal:
                sp = jnp.where(col < row, sp, 0.0)
                zc = jnp.where(col < row, zc, -1e30)
            sp_ref[slot, hh] = sp.astype(BF16)
            zc_ref[slot, hh] = zc
            carry_ref[hh] = carry + jnp.sum(sp, axis=-1, keepdims=True)

    def consume(kt, slot):
        v = v_ref[pl.ds(pl.multiple_of(kt * tq, tq), tq), :]
        for hh in range(2):
            inclusive = _nn(sp_ref[slot, hh], ones_lower)
            a = jnp.exp2(zc_ref[slot, hh] - inclusive)
            acc_ref[hh] += _nn(a.astype(BF16), v)

    def step(kt, slot):
        stage(kt - 1, 1 - slot)
        consume(kt, slot)

    stage(qi, 0, diagonal=True)
    n_pairs = qi // 2

    def more(state):
        return (state[0] < n_pairs) & (state[1] > 0)

    def pair(state):
        p = state[0]
        reach = jnp.max(jnp.maximum(bound_ref[0] - carry_ref[0], bound_ref[1] - carry_ref[1]))
        kt = qi - 2 * p
        step(kt, 0)
        step(kt - 1, 1)
        return p + 1, (reach > -DEAD_EXPONENT).astype(jnp.int32)

    pairs_done, alive = lax.while_loop(more, pair, (jnp.int32(0), jnp.int32(1)))
    staged = qi - 2 * pairs_done

    @pl.when(alive == 0)
    def _():
        consume(staged, 0)

    @pl.when((alive > 0) & (staged == 1))
    def _():
        step(1, 0)
        consume(0, 1)

    @pl.when((alive > 0) & (staged == 0))
    def _():
        consume(0, 0)

    o_ref[...] = jnp.where(head_lanes[0], acc_ref[0], acc_ref[1]).astype(BF16)


def _sb_core(q, k, v, *, batch, seq, tq=256):
    t, d = q.shape
    tq = min(tq, seq)
    nq = seq // tq
    groups = d // LANES
    return pl.pallas_call(
        _sb_core_kernel,
        grid=(batch, groups, nq),
        in_specs=[
            pl.BlockSpec((tq, LANES), lambda b, g, i: (b * nq + i, g)),
            pl.BlockSpec((seq, LANES), lambda b, g, i: (b, g)),
            pl.BlockSpec((seq, LANES), lambda b, g, i: (b, g)),
        ],
        out_specs=pl.BlockSpec((tq, LANES), lambda b, g, i: (b * nq + i, g)),
        out_shape=jax.ShapeDtypeStruct((t, d), BF16),
        scratch_shapes=[
            pltpu.VMEM((2, tq, LANES), BF16),
            pltpu.VMEM((2, 2, tq, tq), BF16),
            pltpu.VMEM((2, 2, tq, tq), F32),
            pltpu.VMEM((2, tq, 1), F32),
            pltpu.VMEM((2, tq, LANES), F32),
            pltpu.VMEM((2, SUBLANES, LANES), F32),
            pltpu.VMEM((2, tq, 1), F32),
        ],
        compiler_params=_params("arbitrary", "arbitrary", "arbitrary"),
        name="sb_core",
    )(q, k, v)


def kernel(x, c, cond_w, cond_b, ada_w, ada_b, ln_g, ln_b, ffn_up, ffn_conv_w, ffn_conv_b, ffn_down,
           gdn_w_in, gdn_conv_w, gdn_a_log, gdn_dt_bias, gdn_norm_w, gdn_w_out,
           ret_w_in, ret_w_out,
           gmlp_w_in, gmlp_ln_g, gmlp_ln_b, gmlp_w_s, gmlp_b_s, gmlp_w_out,
           sb_w_in, sb_w_out):
    batch, seq, d = x.shape
    depth = ada_w.shape[0]
    alpha = (2.0 * depth) ** 0.25
    mods = _conditioning(c, cond_w, cond_b, ada_w, ada_b).reshape(depth, batch, 6, d)
    xt = x.reshape(batch * seq, d)

    for i in range(depth):
        mod = mods[i]
        res = dict(seq=seq, gate_row=2, alpha=alpha)
        mixer = i % 4
        if mixer == 0:
            q, k, v, z, gb, gt = _gdn_in(xt, mod, gdn_w_in, gdn_conv_w, gdn_a_log, gdn_dt_bias, seq=seq)
            o = _gdn_core(q, k, v, z, gb, gt, gdn_norm_w, batch=batch, seq=seq)
            xt = _out_proj_ln(o, gdn_w_out, xt, mod, ln_g[i, 0], ln_b[i, 0], **res)
        elif mixer == 1:
            cos_t, sin_t = _rope_tables(seq, d // RET_HEADS // 2)
            q, k, v, gate = _ret_in(xt, mod, ret_w_in, cos_t, sin_t, seq=seq)
            o = _ret_core(q, k, v, gate, batch=batch, seq=seq)
            xt = _out_proj_ln(o, ret_w_out, xt, mod, ln_g[i, 0], ln_b[i, 0], **res)
        elif mixer == 2:
            xt = _gmlp(xt, mod, gmlp_w_in, gmlp_ln_g, gmlp_ln_b, gmlp_w_s, gmlp_b_s, gmlp_w_out,
                       ln_g[i, 0], ln_b[i, 0], seq=seq, alpha=alpha)
        else:
            q, k, v = _sb_in(xt, mod, sb_w_in, seq=seq)
            o = _sb_core(q, k, v, batch=batch, seq=seq)
            xt = _out_proj_ln(o, sb_w_out, xt, mod, ln_g[i, 0], ln_b[i, 0], **res)
        xt = _ffn(xt, mod, ffn_up[i], ffn_conv_w[i], ffn_conv_b[i], ffn_down[i],
                  ln_g[i, 1], ln_b[i, 1], seq=seq, alpha=alpha)
    return xt.reshape(batch, seq, d)
```

```python
import functools
import math

import jax
import jax.numpy as jnp
from jax import lax
from jax.experimental import pallas as pl
from jax.experimental.pallas import tpu as pltpu

F32 = jnp.float32
BF16 = jnp.bfloat16

LANES = 128
SUBLANES = 8
VMEM_LIMIT = 56 * 1024 * 1024

LN_EPS = 1e-5
GDN_HEADS = 8
GDN_CHUNK = 64
GDN_CONV = 4
RET_HEADS = 4
RET_CHUNK = 128
RET_ROPE_BASE = 10000.0
GMLP_CHUNK = 128
GMLP_GROUPS = 8
SB_HEADS = 16
FFN_CONV = 3
DEAD_EXPONENT = 160.0
NORM_BOUND_SLACK = 1.001


def _params(*sem):
    return pltpu.CompilerParams(dimension_semantics=sem, vmem_limit_bytes=VMEM_LIMIT)


def _nn(a, b):
    return jnp.dot(a, b, preferred_element_type=F32)


def _nt(a, b):
    return lax.dot_general(a, b, (((1,), (1,)), ((), ())), preferred_element_type=F32)


def _tn(a, b):
    return lax.dot_general(a, b, (((0,), (0,)), ((), ())), preferred_element_type=F32)


def _sigmoid(x):
    return 1.0 / (1.0 + jnp.exp(-x))


def _silu(x):
    return x * _sigmoid(x)


def _softplus(x):
    return jnp.maximum(x, 0.0) + jnp.log(1.0 + jnp.exp(-jnp.abs(x)))


def _split3(x):
    hi = x.astype(BF16)
    r = x - hi.astype(F32)
    mid = r.astype(BF16)
    lo = (r - mid.astype(F32)).astype(BF16)
    return hi, mid, lo


def _modulate(x, mod_ref, shift_row, scale_row):
    return (x * (1.0 + mod_ref[0, scale_row:scale_row + 1, :])
            + mod_ref[0, shift_row:shift_row + 1, :]).astype(BF16)


def _layer_norm_rows(r, g, b, eps):
    mu = jnp.mean(r, axis=-1, keepdims=True)
    rc = r - mu
    var = jnp.mean(rc * rc, axis=-1, keepdims=True)
    return rc * lax.rsqrt(var + eps) * g + b


def _lane_replicated_columns(rows):
    n, k = rows.shape
    padded = jnp.concatenate([rows, jnp.zeros((LANES - n, k), F32)], axis=0)
    t = padded.T
    return [jnp.broadcast_to(t[:, b:b + 1], (k, LANES)) for b in range(n)]


def _rowvec_matmul(col, w):
    n = w.shape[1]
    parts = [jnp.sum(col * w[:, c:c + LANES], axis=0, keepdims=True) for c in range(0, n, LANES)]
    return jnp.concatenate(parts, axis=1)


def _cond_kernel(c_ref, cw_ref, cb_ref, aw_ref, ab_ref, o_ref, ecol_ref):
    nb = c_ref.shape[0]

    @pl.when((pl.program_id(0) == 0) & (pl.program_id(1) == 0))
    def _():
        ccols = _lane_replicated_columns(c_ref[...])
        cw = cw_ref[...]
        e = jnp.concatenate([_rowvec_matmul(col, cw) for col in ccols], axis=0) + cb_ref[...]
        ecols = _lane_replicated_columns(_silu(e))
        for b in range(nb):
            ecol_ref[b] = ecols[b]

    w = aw_ref[0]
    for b in range(nb):
        o_ref[0, b:b + 1, :] = _rowvec_matmul(ecol_ref[b], w) + ab_ref[0]


def _conditioning(c, cond_w, cond_b, ada_w, ada_b):
    nb, d = c.shape
    depth, _, n6 = ada_w.shape
    tn = 512
    return pl.pallas_call(
        _cond_kernel,
        grid=(depth, n6 // tn),
        in_specs=[
            pl.BlockSpec((nb, d), lambda i, j: (0, 0)),
            pl.BlockSpec((d, d), lambda i, j: (0, 0)),
            pl.BlockSpec((1, d), lambda i, j: (0, 0)),
            pl.BlockSpec((1, d, tn), lambda i, j: (i, 0, j)),
            pl.BlockSpec((1, 1, tn), lambda i, j: (i, 0, j)),
        ],
        out_specs=pl.BlockSpec((1, nb, tn), lambda i, j: (i, 0, j)),
        out_shape=jax.ShapeDtypeStruct((depth, nb, n6), F32),
        scratch_shapes=[pltpu.VMEM((nb, d, LANES), F32)],
        compiler_params=_params("arbitrary", "arbitrary"),
        name="conditioning",
    )(c, cond_w, cond_b.reshape(1, d), ada_w, ada_b.reshape(depth, 1, n6))


def _out_ln_kernel(o_ref, w_ref, x_ref, mod_ref, g_ref, b_ref, out_ref, *, gate_row, alpha):
    y = _nn(o_ref[...], w_ref[...])
    r = alpha * x_ref[...] + (1.0 + mod_ref[0, gate_row:gate_row + 1, :]) * y
    out_ref[...] = _layer_norm_rows(r, g_ref[...], b_ref[...], LN_EPS)


def _out_proj_ln(o, w_out, x, mod, ln_g, ln_b, *, seq, gate_row, alpha, tm=512):
    t, kdim = o.shape
    d = x.shape[1]
    tps = seq // tm
    return pl.pallas_call(
        functools.partial(_out_ln_kernel, gate_row=gate_row, alpha=alpha),
        grid=(t // tm,),
        in_specs=[
            pl.BlockSpec((tm, kdim), lambda i: (i, 0)),
            pl.BlockSpec((kdim, d), lambda i: (0, 0)),
            pl.BlockSpec((tm, d), lambda i: (i, 0)),
            pl.BlockSpec((1, 6, d), lambda i: (i // tps, 0, 0)),
            pl.BlockSpec((1, d), lambda i: (0, 0)),
            pl.BlockSpec((1, d), lambda i: (0, 0)),
        ],
        out_specs=pl.BlockSpec((tm, d), lambda i: (i, 0)),
        out_shape=jax.ShapeDtypeStruct((t, d), F32),
        compiler_params=_params("arbitrary"),
        name="out_proj_ln",
    )(o, w_out.astype(BF16), x, mod, ln_g.reshape(1, d), ln_b.reshape(1, d))


def _ffn_kernel(x_ref, mod_ref, wup_ref, cw_ref, cb_ref, wd_ref, g_ref, b_ref, out_ref,
                h_ref, buf_ref, act_ref, *, tiles_per_seq, alpha):
    i = pl.program_id(0)
    tm = x_ref.shape[0]
    f = wd_ref.shape[0]
    n_chunks, _, tf = buf_ref.shape
    x = x_ref[...]
    h_ref[...] = _modulate(x, mod_ref, 3, 4)

    @pl.when(i % tiles_per_seq == 0)
    def _():
        for c in range(n_chunks):
            buf_ref[c, 0:SUBLANES, :] = jnp.zeros((SUBLANES, tf), F32)

    for c in range(n_chunks):
        cols = slice(c * tf, (c + 1) * tf)
        buf_ref[c, SUBLANES:SUBLANES + tm, :] = _nn(h_ref[...], wup_ref[:, cols])
        conv = cb_ref[:, cols]
        for tap in range(FFN_CONV):
            off = SUBLANES - (FFN_CONV - 1) + tap
            conv = conv + cw_ref[tap:tap + 1, cols] * buf_ref[c, off:off + tm, :]
        buf_ref[c, 0:SUBLANES, :] = buf_ref[c, tm:tm + SUBLANES, :]
        up = _nn(h_ref[...], wup_ref[:, f + c * tf:f + (c + 1) * tf])
        act_ref[:, cols] = (_silu(conv) * up).astype(BF16)

    y = _nn(act_ref[...], wd_ref[...])
    r = alpha * x + (1.0 + mod_ref[0, 5:6, :]) * y
    out_ref[...] = _layer_norm_rows(r, g_ref[...], b_ref[...], LN_EPS)


def _resident(shape, layer=None):
    if layer is None:
        return pl.BlockSpec(shape, lambda *_: (0,) * len(shape), pipeline_mode=pl.Buffered(1))
    return pl.BlockSpec((None,) + tuple(shape), lambda *_: (layer,) + (0,) * len(shape),
                        pipeline_mode=pl.Buffered(1))


def _ffn(x, mod, w_up, conv_w, conv_b, w_down, ln_g, ln_b, *, layer, seq, alpha, tm=512, n_chunks=2):
    t, d = x.shape
    f = w_down.shape[1]
    tf = f // n_chunks
    tps = seq // tm
    return pl.pallas_call(
        functools.partial(_ffn_kernel, tiles_per_seq=tps, alpha=alpha),
        grid=(t // tm,),
        in_specs=[
            pl.BlockSpec((tm, d), lambda i: (i, 0)),
            pl.BlockSpec((1, 6, d), lambda i: (i // tps, 0, 0)),
            _resident((d, 2 * f), layer),
            _resident((FFN_CONV, f)),
            _resident((1, f)),
            _resident((f, d), layer),
            _resident((1, d)),
            _resident((1, d)),
        ],
        out_specs=pl.BlockSpec((tm, d), lambda i: (i, 0)),
        out_shape=jax.ShapeDtypeStruct((t, d), F32),
        scratch_shapes=[
            pltpu.VMEM((tm, d), BF16),
            pltpu.VMEM((n_chunks, tm + SUBLANES, tf), F32),
            pltpu.VMEM((tm, f), BF16),
        ],
        compiler_params=_params("arbitrary"),
        name="ffn",
    )(x, mod, w_up, conv_w, conv_b.reshape(1, f), w_down, ln_g.reshape(1, d), ln_b.reshape(1, d))


def _gdn_in_kernel(x_ref, mod_ref, wqkv_ref, wz_ref, wab_ref, cw_ref, alog_ref, dtb_ref,
                   q_ref, k_ref, v_ref, z_ref, gb_ref, gt_ref, buf_ref, *, tiles_per_seq, chunk):
    i = pl.program_id(0)
    tm, d = x_ref.shape
    nh = GDN_HEADS
    hd = d // nh
    h = _modulate(x_ref[...], mod_ref, 0, 1)

    @pl.when(i % tiles_per_seq == 0)
    def _():
        buf_ref[0:SUBLANES, :] = jnp.zeros((SUBLANES, buf_ref.shape[1]), F32)

    outs = (q_ref, k_ref, v_ref)
    for s in range(3):
        cs = slice(s * d, (s + 1) * d)
        buf_ref[SUBLANES:SUBLANES + tm, cs] = _nn(h, wqkv_ref[:, cs])
        y = None
        for tap in range(GDN_CONV):
            off = SUBLANES - (GDN_CONV - 1) + tap
            term = cw_ref[tap:tap + 1, cs] * buf_ref[off:off + tm, cs]
            y = term if y is None else y + term
        y = _silu(y)
        if s < 2:
            scale = hd ** -0.5 if s == 0 else 1.0
            for hh in range(nh):
                seg = y[:, hh * hd:(hh + 1) * hd]
                inv = lax.rsqrt(jnp.sum(seg * seg, axis=-1, keepdims=True) + 1e-6)
                outs[s][:, hh * hd:(hh + 1) * hd] = (seg * (inv * scale)).astype(BF16)
        else:
            outs[s][...] = y.astype(BF16)
    buf_ref[0:SUBLANES, :] = buf_ref[tm:tm + SUBLANES, :]

    z_ref[...] = _nn(h, wz_ref[...]).astype(BF16)

    pab = _nn(h, wab_ref[...])
    lane = lax.broadcasted_iota(jnp.int32, pab.shape, 1)
    g = -jnp.exp(alog_ref[...]) * _softplus(pab + dtb_ref[...])
    gb = jnp.where(lane < nh, g, jnp.where(lane < 2 * nh, _sigmoid(pab), 0.0))
    row = lax.broadcasted_iota(jnp.int32, (tm, tm), 0)
    col = lax.broadcasted_iota(jnp.int32, (tm, tm), 1)
    tri = jnp.where(col <= row, jnp.where(jnp.bitwise_xor(row, col) < chunk, 1.0, 0.0), 0.0).astype(BF16)
    hi, mid, lo = _split3(gb)
    cum = _nn(tri, hi) + _nn(tri, mid) + _nn(tri, lo)
    gb = jnp.where(lane < nh, cum, gb)
    gb_ref[...] = gb
    gt_ref[...] = gb.T[0:2 * nh, :]


def _gdn_in(x, mod, w_in, conv_w, a_log, dt_bias, *, seq, tm=256):
    t, d = x.shape
    nh = GDN_HEADS
    tps = seq // tm
    w_b = w_in.astype(BF16)
    w_qkv = w_b[:, :3 * d]
    w_z = w_b[:, 3 * d:4 * d]
    w_ab = jnp.pad(w_b[:, 4 * d:], ((0, 0), (0, LANES - 2 * nh)))
    alog = jnp.pad(a_log, (0, LANES - nh)).reshape(1, LANES)
    dtb = jnp.pad(dt_bias, (0, LANES - nh)).reshape(1, LANES)
    row = lambda i: (i, 0)
    fixed = lambda i: (0, 0)
    return pl.pallas_call(
        functools.partial(_gdn_in_kernel, tiles_per_seq=tps, chunk=GDN_CHUNK),
        grid=(t // tm,),
        in_specs=[
            pl.BlockSpec((tm, d), row),
            pl.BlockSpec((1, 6, d), lambda i: (i // tps, 0, 0)),
            pl.BlockSpec((d, 3 * d), fixed),
            pl.BlockSpec((d, d), fixed),
            pl.BlockSpec((d, LANES), fixed),
            pl.BlockSpec((GDN_CONV, 3 * d), fixed),
            pl.BlockSpec((1, LANES), fixed),
            pl.BlockSpec((1, LANES), fixed),
        ],
        out_specs=[pl.BlockSpec((tm, d), row)] * 4 + [pl.BlockSpec((tm, LANES), row),
                                                      pl.BlockSpec((2 * nh, tm), lambda i: (0, i))],
        out_shape=[jax.ShapeDtypeStruct((t, d), BF16)] * 4 + [jax.ShapeDtypeStruct((t, LANES), F32),
                                                              jax.ShapeDtypeStruct((2 * nh, t), F32)],
        scratch_shapes=[pltpu.VMEM((tm + SUBLANES, 3 * d), F32)],
        compiler_params=_params("arbitrary"),
        name="gdn_in",
    )(x, mod, w_qkv, w_z, w_ab, conv_w, alog, dtb)


def _gdn_core_kernel(q_ref, k_ref, v_ref, z_ref, gb_ref, gt_ref, nw_ref, o_ref,
                     state_ref, inv_ref, pw_ref, u_ref, w_ref, qk_ref, qg_ref, kd_ref, dl_ref, vn_ref,
                     oi_ref, *, chunk):
    rows, d = q_ref.shape
    nh = GDN_HEADS
    hd = d // nh
    n_chunks = rows // chunk
    heads = range(nh)
    lanes = [slice(hh * hd, (hh + 1) * hd) for hh in heads]

    @pl.when(pl.program_id(1) == 0)
    def _():
        state_ref[...] = jnp.zeros(state_ref.shape, F32)

    row = lax.broadcasted_iota(jnp.int32, (rows, rows), 0)
    col = lax.broadcasted_iota(jnp.int32, (rows, rows), 1)
    same_chunk = jnp.bitwise_xor(row, col) < chunk
    gb = gb_ref[...]
    for hh in heads:
        kh = k_ref[:, lanes[hh]]
        kf = kh.astype(F32)
        qh = q_ref[:, lanes[hh]]
        gc = jnp.broadcast_to(gb[:, hh:hh + 1], (rows, hd))
        beta = jnp.broadcast_to(gb[:, nh + hh:nh + hh + 1], (rows, hd))
        diff = jnp.broadcast_to(gb[:, hh:hh + 1], (rows, rows)) - gt_ref[hh:hh + 1, :]
        causal = same_chunk & (row >= col)
        decay = jnp.where(causal, jnp.exp(jnp.where(causal, diff, 0.0)), 0.0)
        kb = kf * beta
        a = jnp.where(row > col, _nt(kb.astype(BF16), kh) * decay, 0.0)
        inv_ref[hh] = jnp.where(row == col, 1.0, 0.0) - a
        pw_ref[hh] = a.astype(BF16)
        qk_ref[hh] = (_nt(qh, kh) * decay).astype(BF16)
        u_ref[hh] = v_ref[:, lanes[hh]].astype(F32) * beta
        w_ref[hh] = (kb * jnp.exp(gc)).astype(BF16)
        qg_ref[hh] = (qh.astype(F32) * jnp.exp(gc)).astype(BF16)
        for c in range(n_chunks):
            rs = slice(c * chunk, (c + 1) * chunk)
            g_last = gc[(c + 1) * chunk - 1:(c + 1) * chunk, :]
            kd_ref[hh, rs, :] = (kf[rs] * jnp.exp(g_last - gc[rs])).astype(BF16)
            dl_ref[hh, c] = jnp.exp(g_last)

    for _ in range(int(math.log2(chunk)) - 1):
        for hh in heads:
            power = _nn(pw_ref[hh], pw_ref[hh]).astype(BF16)
            pw_ref[hh] = power
            inv = inv_ref[hh]
            inv_ref[hh] = inv + _nn(inv.astype(BF16), power)
    for hh in heads:
        inv = inv_ref[hh].astype(BF16)
        u_ref[hh] = _nn(inv, u_ref[hh].astype(BF16))
        w_ref[hh] = _nn(inv, w_ref[hh]).astype(BF16)

    for c in range(n_chunks):
        rs = slice(c * chunk, (c + 1) * chunk)
        states = [state_ref[hh].astype(BF16) for hh in heads]
        v_new = [(u_ref[hh, rs, :] - _nn(w_ref[hh, rs, :], states[hh])).astype(BF16) for hh in heads]
        for hh in heads:
            vn_ref[hh, rs, :] = v_new[hh]
            state_ref[hh] = state_ref[hh] * dl_ref[hh, c] + _tn(kd_ref[hh, rs, :], v_new[hh])
            oi_ref[hh, rs, :] = _nn(qg_ref[hh, rs, :], states[hh])

    for hh in heads:
        o = oi_ref[hh] + _nn(qk_ref[hh], vn_ref[hh])
        o = o * lax.rsqrt(jnp.mean(o * o, axis=-1, keepdims=True) + 1e-6) * nw_ref[...]
        o_ref[:, lanes[hh]] = (o * _silu(z_ref[:, lanes[hh]].astype(F32))).astype(BF16)


def _gdn_core(q, k, v, z, gb, gt, norm_w, *, batch, seq, rows=256):
    t, d = q.shape
    n = seq // rows
    nh = GDN_HEADS
    hd = d // nh
    row = lambda b, j: (b * n + j, 0)
    return pl.pallas_call(
        functools.partial(_gdn_core_kernel, chunk=GDN_CHUNK),
        grid=(batch, n),
        in_specs=[pl.BlockSpec((rows, d), row)] * 4 + [
            pl.BlockSpec((rows, LANES), row),
            pl.BlockSpec((2 * nh, rows), lambda b, j: (0, b * n + j)),
            pl.BlockSpec((1, hd), lambda b, j: (0, 0)),
        ],
        out_specs=pl.BlockSpec((rows, d), row),
        out_shape=jax.ShapeDtypeStruct((t, d), BF16),
        scratch_shapes=[
            pltpu.VMEM((nh, hd, hd), F32),
            pltpu.VMEM((nh, rows, rows), F32),
            pltpu.VMEM((nh, rows, rows), BF16),
            pltpu.VMEM((nh, rows, hd), F32),
            pltpu.VMEM((nh, rows, hd), BF16),
            pltpu.VMEM((nh, rows, rows), BF16),
            pltpu.VMEM((nh, rows, hd), BF16),
            pltpu.VMEM((nh, rows, hd), BF16),
            pltpu.VMEM((nh, rows // GDN_CHUNK, 1, hd), F32),
            pltpu.VMEM((nh, rows, hd), BF16),
            pltpu.VMEM((nh, rows, hd), F32),
        ],
        compiler_params=_params("arbitrary", "arbitrary"),
        name="gdn_core",
    )(q, k, v, z, gb, gt, norm_w.reshape(1, hd))


def _rope_table_kernel(cos_ref, sin_ref):
    ts, half = cos_ref.shape
    pos = (lax.broadcasted_iota(jnp.int32, (ts, half), 0) + pl.program_id(0) * ts).astype(F32)
    frac = lax.broadcasted_iota(jnp.int32, (ts, half), 1).astype(F32) / (half - 1.0)
    ang = pos * jnp.exp(-frac * math.log(RET_ROPE_BASE))
    cos_ref[...] = jnp.cos(ang)
    sin_ref[...] = jnp.sin(ang)


def _rope_tables(seq, half, ts=256):
    spec = pl.BlockSpec((ts, half), lambda i: (i, 0))
    return pl.pallas_call(
        _rope_table_kernel,
        grid=(seq // ts,),
        out_specs=[spec, spec],
        out_shape=[jax.ShapeDtypeStruct((seq, half), F32)] * 2,
        compiler_params=_params("arbitrary"),
        name="rope_tables",
    )()


def _ret_in_kernel(x_ref, mod_ref, w_ref, cos_ref, sin_ref, q_ref, k_ref, v_ref, gate_ref):
    d = x_ref.shape[1]
    nh = RET_HEADS
    dk = d // nh
    half = dk // 2
    h = _modulate(x_ref[...], mod_ref, 0, 1)
    cos_a = cos_ref[...]
    sin_a = sin_ref[...]
    for s, (out, scale) in enumerate(((q_ref, 1.0), (k_ref, dk ** -0.5))):
        t = _nn(h, w_ref[:, s * d:(s + 1) * d])
        for hh in range(nh):
            t1 = t[:, hh * dk:hh * dk + half]
            t2 = t[:, hh * dk + half:(hh + 1) * dk]
            out[:, hh * dk:hh * dk + half] = ((t1 * cos_a - t2 * sin_a) * scale).astype(BF16)
            out[:, hh * dk + half:(hh + 1) * dk] = ((t1 * sin_a + t2 * cos_a) * scale).astype(BF16)
    v_ref[...] = _nn(h, w_ref[:, 2 * d:4 * d]).astype(BF16)
    gate_ref[...] = _nn(h, w_ref[:, 4 * d:6 * d])


def _ret_in(x, mod, w_in, cos_t, sin_t, *, seq, tm=256):
    t, d = x.shape
    tps = seq // tm
    half = cos_t.shape[1]
    row = lambda i: (i, 0)
    return pl.pallas_call(
        _ret_in_kernel,
        grid=(t // tm,),
        in_specs=[
            pl.BlockSpec((tm, d), row),
            pl.BlockSpec((1, 6, d), lambda i: (i // tps, 0, 0)),
            pl.BlockSpec((d, 6 * d), lambda i: (0, 0)),
            pl.BlockSpec((tm, half), lambda i: (i % tps, 0)),
            pl.BlockSpec((tm, half), lambda i: (i % tps, 0)),
        ],
        out_specs=[pl.BlockSpec((tm, d), row), pl.BlockSpec((tm, d), row),
                   pl.BlockSpec((tm, 2 * d), row), pl.BlockSpec((tm, 2 * d), row)],
        out_shape=[jax.ShapeDtypeStruct((t, d), BF16), jax.ShapeDtypeStruct((t, d), BF16),
                   jax.ShapeDtypeStruct((t, 2 * d), BF16), jax.ShapeDtypeStruct((t, 2 * d), F32)],
        compiler_params=_params("arbitrary"),
        name="ret_in",
    )(x, mod, w_in.astype(BF16), cos_t, sin_t)


def _ret_core_kernel(q_ref, k_ref, v_ref, gate_ref, o_ref, state_ref):
    c, d = q_ref.shape
    nh = RET_HEADS
    dk = d // nh
    dv = v_ref.shape[1] // nh

    @pl.when(pl.program_id(1) == 0)
    def _():
        state_ref[...] = jnp.zeros(state_ref.shape, F32)

    rel = (lax.broadcasted_iota(jnp.int32, (c, c), 0)
           - lax.broadcasted_iota(jnp.int32, (c, c), 1)).astype(F32)
    idx = lax.broadcasted_iota(jnp.int32, (c, 1), 0).astype(F32)
    for hh in range(nh):
        log_gamma = math.log(1.0 - 2.0 ** (-5.0 - hh))
        dmask = jnp.where(rel >= 0, jnp.exp(jnp.maximum(rel, 0.0) * log_gamma), 0.0)
        zeta = jnp.exp((c - 1.0 - idx) * log_gamma)
        xi = jnp.exp((idx + 1.0) * log_gamma)
        qh = q_ref[:, hh * dk:(hh + 1) * dk]
        kh = k_ref[:, hh * dk:(hh + 1) * dk]
        vh = v_ref[:, hh * dv:(hh + 1) * dv]
        scores = _nt(qh, kh) * dmask
        state = state_ref[hh]
        o = _nn(scores.astype(BF16), vh) + _nn(qh, state.astype(BF16)) * xi
        state_ref[hh] = (state * math.exp(c * log_gamma)
                         + _tn((kh.astype(F32) * zeta).astype(BF16), vh))
        mu = jnp.mean(o, axis=-1, keepdims=True)
        oc = o - mu
        var = jnp.mean(oc * oc, axis=-1, keepdims=True)
        o = oc * lax.rsqrt(var + 1e-6)
        o_ref[:, hh * dv:(hh + 1) * dv] = (o * _silu(gate_ref[:, hh * dv:(hh + 1) * dv])).astype(BF16)


def _ret_core(q, k, v, gate, *, batch, seq):
    t, d = q.shape
    c = RET_CHUNK
    n = seq // c
    dk = d // RET_HEADS
    dv = v.shape[1] // RET_HEADS
    row = lambda b, j: (b * n + j, 0)
    return pl.pallas_call(
        _ret_core_kernel,
        grid=(batch, n),
        in_specs=[pl.BlockSpec((c, d), row), pl.BlockSpec((c, d), row),
                  pl.BlockSpec((c, 2 * d), row), pl.BlockSpec((c, 2 * d), row)],
        out_specs=pl.BlockSpec((c, 2 * d), row),
        out_shape=jax.ShapeDtypeStruct((t, 2 * d), BF16),
        scratch_shapes=[pltpu.VMEM((RET_HEADS, dk, dv), F32)],
        compiler_params=_params("arbitrary", "arbitrary"),
        name="ret_core",
    )(q, k, v, gate)


def _erf(x):
    x = jnp.clip(x, -4.0, 4.0)
    x2 = x * x
    p = -2.72614225801306e-10
    for coef in (2.77068142495902e-08, -2.10102402082508e-06, -5.69250639462346e-05,
                 -7.34990630326855e-04, -2.95459980854025e-03, -1.60960333262415e-02):
        p = p * x2 + coef
    q = -1.45660718464996e-05
    for coef in (-2.13374055278905e-04, -1.68282697438203e-03, -7.37332916720468e-03,
                 -1.42647390514189e-02):
        q = q * x2 + coef
    return x * p / q


def _gmlp_kernel(x_ref, mod_ref, win_ref, lng_ref, lnb_ref, ws_ref, bs_ref, wout_ref, g_ref, b_ref,
                 out_ref, *, alpha):
    tm, d = x_ref.shape
    width = win_ref.shape[1] // 2
    c = GMLP_CHUNK
    ng = GMLP_GROUPS
    gw = width // ng
    x = x_ref[...]
    h = _modulate(x, mod_ref, 0, 1)

    def gelu(t):
        return 0.5 * t * (1.0 + _erf(t * (2.0 ** -0.5)))

    v = gelu(_nn(h, win_ref[:, width:]))
    v = _layer_norm_rows(v, lng_ref[...], lnb_ref[...], LN_EPS).astype(BF16)
    u = gelu(_nn(h, win_ref[:, :width]))
    row = lax.broadcasted_iota(jnp.int32, (c, c), 0)
    col = lax.broadcasted_iota(jnp.int32, (c, c), 1)
    parts = []
    for n in range(tm // c):
        rs = slice(n * c, (n + 1) * c)
        groups = []
        for gi in range(ng):
            ls = slice(gi * gw, (gi + 1) * gw)
            ws = jnp.where(row >= col, ws_ref[gi], 0.0).astype(BF16)
            vs = _nn(ws, v[rs, ls]) + bs_ref[:, gi:gi + 1]
            groups.append((u[rs, ls] * vs).astype(BF16))
        parts.append(jnp.concatenate(groups, axis=1))
    gated = jnp.concatenate(parts, axis=0)
    y = _nn(gated, wout_ref[...])
    r = alpha * x + (1.0 + mod_ref[0, 2:3, :]) * y
    out_ref[...] = _layer_norm_rows(r, g_ref[...], b_ref[...], LN_EPS)


def _gmlp(x, mod, w_in, ln_g, ln_b, w_s, b_s, w_out, res_g, res_b, *, seq, alpha, tm=256):
    t, d = x.shape
    width = w_out.shape[0]
    ng, c, _ = w_s.shape
    tps = seq // tm
    fixed = lambda i: (0, 0)
    return pl.pallas_call(
        functools.partial(_gmlp_kernel, alpha=alpha),
        grid=(t // tm,),
        in_specs=[
            pl.BlockSpec((tm, d), lambda i: (i, 0)),
            pl.BlockSpec((1, 6, d), lambda i: (i // tps, 0, 0)),
            pl.BlockSpec((d, 2 * width), fixed),
            pl.BlockSpec((1, width), fixed),
            pl.BlockSpec((1, width), fixed),
            pl.BlockSpec((ng, c, c), lambda i: (0, 0, 0)),
            pl.BlockSpec((c, ng), fixed),
            pl.BlockSpec((width, d), fixed),
            pl.BlockSpec((1, d), fixed),
            pl.BlockSpec((1, d), fixed),
        ],
        out_specs=pl.BlockSpec((tm, d), lambda i: (i, 0)),
        out_shape=jax.ShapeDtypeStruct((t, d), F32),
        compiler_params=_params("arbitrary"),
        name="gmlp",
    )(x, mod, w_in.astype(BF16), ln_g.reshape(1, width), ln_b.reshape(1, width), w_s, b_s.T,
      w_out.astype(BF16), res_g.reshape(1, d), res_b.reshape(1, d))


def _sb_in_kernel(x_ref, mod_ref, w_ref, q_ref, k_ref, v_ref, *, q_scale):
    d = x_ref.shape[1]
    h = _modulate(x_ref[...], mod_ref, 0, 1)
    q_ref[...] = (_nn(h, w_ref[:, 0:d]) * q_scale).astype(BF16)
    k_ref[...] = _nn(h, w_ref[:, d:2 * d]).astype(BF16)
    v_ref[...] = _nn(h, w_ref[:, 2 * d:3 * d]).astype(BF16)


def _sb_in(x, mod, w_in, *, seq, tm=512):
    t, d = x.shape
    tps = seq // tm
    row = lambda i: (i, 0)
    return pl.pallas_call(
        functools.partial(_sb_in_kernel, q_scale=(d // SB_HEADS) ** -0.5 * math.log2(math.e)),
        grid=(t // tm,),
        in_specs=[
            pl.BlockSpec((tm, d), row),
            pl.BlockSpec((1, 6, d), lambda i: (i // tps, 0, 0)),
            pl.BlockSpec((d, 3 * d), lambda i: (0, 0)),
        ],
        out_specs=[pl.BlockSpec((tm, d), row)] * 3,
        out_shape=[jax.ShapeDtypeStruct((t, d), BF16)] * 3,
        compiler_params=_params("arbitrary"),
        name="sb_in",
    )(x, mod, w_in.astype(BF16))


def _sb_core_kernel(q_ref, k_ref, v_ref, o_ref, qh_ref, sp_ref, zc_ref, carry_ref, acc_ref, knorm_ref,
                    bound_ref):
    tq, lanes = q_ref.shape
    seq = k_ref.shape[0]
    dh = lanes // 2
    qi = pl.program_id(2)
    q = q_ref[...]
    lane = lax.broadcasted_iota(jnp.int32, (tq, lanes), 1)
    head_lanes = (lane < dh, lane >= dh)
    row = lax.broadcasted_iota(jnp.int32, (tq, tq), 0)
    col = lax.broadcasted_iota(jnp.int32, (tq, tq), 1)
    ones_lower = jnp.where(row >= col, 1.0, 0.0).astype(BF16)

    @pl.when(qi == 0)
    def _():
        def key_tile(t, best):
            kf = k_ref[pl.ds(pl.multiple_of(t * tq, tq), tq), :].astype(F32)
            return jnp.maximum(best, jnp.max(kf * kf, axis=0, keepdims=True))

        col_max = lax.fori_loop(0, seq // tq, key_tile, jnp.zeros((1, lanes), F32))
        for hh in range(2):
            bound_sq = jnp.sum(jnp.where(head_lanes[hh][0:1, :], col_max, 0.0), axis=-1, keepdims=True)
            knorm_ref[hh] = jnp.broadcast_to(bound_sq, knorm_ref.shape[1:])

    qf = q.astype(F32)
    for hh in range(2):
        qh_ref[hh] = jnp.where(head_lanes[hh], q, jnp.zeros_like(q))
        qsq = jnp.sum(jnp.where(head_lanes[hh], qf * qf, 0.0), axis=-1, keepdims=True)
        bound_ref[hh] = jnp.sqrt(qsq * knorm_ref[hh, 0:1, 0:1]) * NORM_BOUND_SLACK
    acc_ref[...] = jnp.zeros(acc_ref.shape, F32)
    carry_ref[...] = jnp.zeros(carry_ref.shape, F32)

    def stage(kt, slot, diagonal=False):
        k = k_ref[pl.ds(pl.multiple_of(kt * tq, tq), tq), :]
        for hh in range(2):
            z = _nt(qh_ref[hh], k)
            sp = jnp.maximum(z, 0.0) + jnp.log2(1.0 + jnp.exp2(-jnp.abs(z)))
            carry = carry_ref[hh]
            zc = z - carry
            if diagonal:
                sp = jnp.where(col < row, sp, 0.0)
                zc = jnp.where(col < row, zc, -1e30)
            sp_ref[slot, hh] = sp.astype(BF16)
            zc_ref[slot, hh] = zc
            carry_ref[hh] = carry + jnp.sum(sp, axis=-1, keepdims=True)

    def consume(kt, slot):
        v = v_ref[pl.ds(pl.multiple_of(kt * tq, tq), tq), :]
        for hh in range(2):
            inclusive = _nn(sp_ref[slot, hh], ones_lower)
            a = jnp.exp2(zc_ref[slot, hh] - inclusive)
            acc_ref[hh] += _nn(a.astype(BF16), v)

    def still_alive():
        reach = jnp.max(jnp.maximum(bound_ref[0] - carry_ref[0], bound_ref[1] - carry_ref[1]))
        return (reach > -DEAD_EXPONENT).astype(jnp.int32)

    stage(qi, 0, diagonal=True)

    @pl.when(qi == 0)
    def _():
        consume(0, 0)

    @pl.when(qi > 0)
    def _():
        stage(qi - 1, 1)
        alive = still_alive()
        consume(qi, 0)
        n_pairs = (qi - 1) // 2

        def more(state):
            return (state[0] < n_pairs) & (state[1] > 0)

        def pair(state):
            kt = qi - 1 - 2 * state[0]
            stage(kt - 1, 0)
            consume(kt, 1)
            stage(kt - 2, 1)
            alive = still_alive()
            consume(kt - 1, 0)
            return state[0] + 1, alive

        pairs_done, alive = lax.while_loop(more, pair, (jnp.int32(0), alive))
        staged = qi - 1 - 2 * pairs_done

        @pl.when((alive == 0) | (staged == 0))
        def _():
            consume(staged, 1)

        @pl.when((alive > 0) & (staged == 1))
        def _():
            stage(0, 0)
            consume(1, 1)
            consume(0, 0)

    o_ref[...] = jnp.where(head_lanes[0], acc_ref[0], acc_ref[1]).astype(BF16)


def _sb_core(q, k, v, *, batch, seq, tq=256):
    t, d = q.shape
    tq = min(tq, seq)
    nq = seq // tq
    groups = d // LANES
    return pl.pallas_call(
        _sb_core_kernel,
        grid=(batch, groups, nq),
        in_specs=[
            pl.BlockSpec((tq, LANES), lambda b, g, i: (b * nq + i, g)),
            pl.BlockSpec((seq, LANES), lambda b, g, i: (b, g)),
            pl.BlockSpec((seq, LANES), lambda b, g, i: (b, g)),
        ],
        out_specs=pl.BlockSpec((tq, LANES), lambda b, g, i: (b * nq + i, g)),
        out_shape=jax.ShapeDtypeStruct((t, d), BF16),
        scratch_shapes=[
            pltpu.VMEM((2, tq, LANES), BF16),
            pltpu.VMEM((2, 2, tq, tq), BF16),
            pltpu.VMEM((2, 2, tq, tq), F32),
            pltpu.VMEM((2, tq, 1), F32),
            pltpu.VMEM((2, tq, LANES), F32),
            pltpu.VMEM((2, SUBLANES, LANES), F32),
            pltpu.VMEM((2, tq, 1), F32),
        ],
        compiler_params=_params("arbitrary", "arbitrary", "arbitrary"),
        name="sb_core",
    )(q, k, v)


def kernel(x, c, cond_w, cond_b, ada_w, ada_b, ln_g, ln_b, ffn_up, ffn_conv_w, ffn_conv_b, ffn_down,
           gdn_w_in, gdn_conv_w, gdn_a_log, gdn_dt_bias, gdn_norm_w, gdn_w_out,
           ret_w_in, ret_w_out,
           gmlp_w_in, gmlp_ln_g, gmlp_ln_b, gmlp_w_s, gmlp_b_s, gmlp_w_out,
           sb_w_in, sb_w_out):
    batch, seq, d = x.shape
    depth = ada_w.shape[0]
    alpha = (2.0 * depth) ** 0.25
    mods = _conditioning(c, cond_w, cond_b, ada_w, ada_b).reshape(depth, batch, 6, d)
    xt = x.reshape(batch * seq, d)
    ffn_up_b = ffn_up.astype(BF16)
    ffn_down_b = ffn_down.astype(BF16)

    for i in range(depth):
        mod = mods[i]
        res = dict(seq=seq, gate_row=2, alpha=alpha)
        mixer = i % 4
        if mixer == 0:
            q, k, v, z, gb, gt = _gdn_in(xt, mod, gdn_w_in, gdn_conv_w, gdn_a_log, gdn_dt_bias, seq=seq)
            o = _gdn_core(q, k, v, z, gb, gt, gdn_norm_w, batch=batch, seq=seq)
            xt = _out_proj_ln(o, gdn_w_out, xt, mod, ln_g[i, 0], ln_b[i, 0], **res)
        elif mixer == 1:
            cos_t, sin_t = _rope_tables(seq, d // RET_HEADS // 2)
            q, k, v, gate = _ret_in(xt, mod, ret_w_in, cos_t, sin_t, seq=seq)
            o = _ret_core(q, k, v, gate, batch=batch, seq=seq)
            xt = _out_proj_ln(o, ret_w_out, xt, mod, ln_g[i, 0], ln_b[i, 0], **res)
        elif mixer == 2:
            xt = _gmlp(xt, mod, gmlp_w_in, gmlp_ln_g, gmlp_ln_b, gmlp_w_s, gmlp_b_s, gmlp_w_out,
                       ln_g[i, 0], ln_b[i, 0], seq=seq, alpha=alpha)
        else:
            q, k, v = _sb_in(xt, mod, sb_w_in, seq=seq)
            o = _sb_core(q, k, v, batch=batch, seq=seq)
            xt = _out_proj_ln(o, sb_w_out, xt, mod, ln_g[i, 0], ln_b[i, 0], **res)
        xt = _ffn(xt, mod, ffn_up_b, ffn_conv_w[i], ffn_conv_b[i], ffn_down_b,
                  ln_g[i, 1], ln_b[i, 1], layer=i, seq=seq, alpha=alpha)
    return xt.reshape(batch, seq, d)
```

```python
import functools
import math

import jax
import jax.numpy as jnp
from jax import lax
from jax.experimental import pallas as pl
from jax.experimental.pallas import tpu as pltpu

F32 = jnp.float32
BF16 = jnp.bfloat16

LANES = 128
SUBLANES = 8
VMEM_LIMIT = 56 * 1024 * 1024

LN_EPS = 1e-5
GDN_HEADS = 8
GDN_CHUNK = 64
GDN_CONV = 4
RET_HEADS = 4
RET_CHUNK = 128
RET_ROPE_BASE = 10000.0
GMLP_CHUNK = 128
GMLP_GROUPS = 8
SB_HEADS = 16
FFN_CONV = 3
DEAD_EXPONENT = 160.0
NORM_BOUND_SLACK = 1.001


def _params(*sem):
    return pltpu.CompilerParams(dimension_semantics=sem, vmem_limit_bytes=VMEM_LIMIT)


def _nn(a, b):
    return jnp.dot(a, b, preferred_element_type=F32)


def _nt(a, b):
    return lax.dot_general(a, b, (((1,), (1,)), ((), ())), preferred_element_type=F32)


def _tn(a, b):
    return lax.dot_general(a, b, (((0,), (0,)), ((), ())), preferred_element_type=F32)


def _sigmoid(x):
    return 1.0 / (1.0 + jnp.exp(-x))


def _silu(x):
    return x * _sigmoid(x)


def _softplus(x):
    return jnp.maximum(x, 0.0) + jnp.log(1.0 + jnp.exp(-jnp.abs(x)))


def _split3(x):
    hi = x.astype(BF16)
    r = x - hi.astype(F32)
    mid = r.astype(BF16)
    lo = (r - mid.astype(F32)).astype(BF16)
    return hi, mid, lo


def _modulate(x, mod_ref, shift_row, scale_row):
    return (x * (1.0 + mod_ref[0, scale_row:scale_row + 1, :])
            + mod_ref[0, shift_row:shift_row + 1, :]).astype(BF16)


def _layer_norm_rows(r, g, b, eps):
    mu = jnp.mean(r, axis=-1, keepdims=True)
    rc = r - mu
    var = jnp.mean(rc * rc, axis=-1, keepdims=True)
    return rc * lax.rsqrt(var + eps) * g + b


def _lane_replicated_columns(rows):
    n, k = rows.shape
    padded = jnp.concatenate([rows, jnp.zeros((LANES - n, k), F32)], axis=0)
    t = padded.T
    return [jnp.broadcast_to(t[:, b:b + 1], (k, LANES)) for b in range(n)]


def _rowvec_matmul(col, w):
    n = w.shape[1]
    parts = [jnp.sum(col * w[:, c:c + LANES], axis=0, keepdims=True) for c in range(0, n, LANES)]
    return jnp.concatenate(parts, axis=1)


def _cond_kernel(c_ref, cw_ref, cb_ref, aw_ref, ab_ref, o_ref, ecol_ref):
    nb = c_ref.shape[0]

    @pl.when((pl.program_id(0) == 0) & (pl.program_id(1) == 0))
    def _():
        ccols = _lane_replicated_columns(c_ref[...])
        cw = cw_ref[...]
        e = jnp.concatenate([_rowvec_matmul(col, cw) for col in ccols], axis=0) + cb_ref[...]
        ecols = _lane_replicated_columns(_silu(e))
        for b in range(nb):
            ecol_ref[b] = ecols[b]

    w = aw_ref[0]
    for b in range(nb):
        o_ref[0, b:b + 1, :] = _rowvec_matmul(ecol_ref[b], w) + ab_ref[0]


def _conditioning(c, cond_w, cond_b, ada_w, ada_b):
    nb, d = c.shape
    depth, _, n6 = ada_w.shape
    tn = 512
    return pl.pallas_call(
        _cond_kernel,
        grid=(depth, n6 // tn),
        in_specs=[
            pl.BlockSpec((nb, d), lambda i, j: (0, 0)),
            pl.BlockSpec((d, d), lambda i, j: (0, 0)),
            pl.BlockSpec((1, d), lambda i, j: (0, 0)),
            pl.BlockSpec((1, d, tn), lambda i, j: (i, 0, j)),
            pl.BlockSpec((1, 1, tn), lambda i, j: (i, 0, j)),
        ],
        out_specs=pl.BlockSpec((1, nb, tn), lambda i, j: (i, 0, j)),
        out_shape=jax.ShapeDtypeStruct((depth, nb, n6), F32),
        scratch_shapes=[pltpu.VMEM((nb, d, LANES), F32)],
        compiler_params=_params("arbitrary", "arbitrary"),
        name="conditioning",
    )(c, cond_w, cond_b.reshape(1, d), ada_w, ada_b.reshape(depth, 1, n6))


def _out_ln_kernel(o_ref, w_ref, x_ref, mod_ref, g_ref, b_ref, out_ref, *, gate_row, alpha):
    y = _nn(o_ref[...], w_ref[...])
    r = alpha * x_ref[...] + (1.0 + mod_ref[0, gate_row:gate_row + 1, :]) * y
    out_ref[...] = _layer_norm_rows(r, g_ref[...], b_ref[...], LN_EPS)


def _out_proj_ln(o, w_out, x, mod, ln_g, ln_b, *, seq, gate_row, alpha, tm=512):
    t, kdim = o.shape
    d = x.shape[1]
    tps = seq // tm
    return pl.pallas_call(
        functools.partial(_out_ln_kernel, gate_row=gate_row, alpha=alpha),
        grid=(t // tm,),
        in_specs=[
            pl.BlockSpec((tm, kdim), lambda i: (i, 0)),
            pl.BlockSpec((kdim, d), lambda i: (0, 0)),
            pl.BlockSpec((tm, d), lambda i: (i, 0)),
            pl.BlockSpec((1, 6, d), lambda i: (i // tps, 0, 0)),
            pl.BlockSpec((1, d), lambda i: (0, 0)),
            pl.BlockSpec((1, d), lambda i: (0, 0)),
        ],
        out_specs=pl.BlockSpec((tm, d), lambda i: (i, 0)),
        out_shape=jax.ShapeDtypeStruct((t, d), F32),
        compiler_params=_params("arbitrary"),
        name="out_proj_ln",
    )(o, w_out.astype(BF16), x, mod, ln_g.reshape(1, d), ln_b.reshape(1, d))


def _ffn_kernel(x_ref, mod_ref, wup_ref, cw_ref, cb_ref, wd_ref, g_ref, b_ref, out_ref,
                h_ref, buf_ref, act_ref, *, tiles_per_seq, alpha):
    i = pl.program_id(0)
    tm = x_ref.shape[0]
    f = wd_ref.shape[0]
    n_chunks, _, tf = buf_ref.shape
    x = x_ref[...]
    h_ref[...] = _modulate(x, mod_ref, 3, 4)

    @pl.when(i % tiles_per_seq == 0)
    def _():
        for c in range(n_chunks):
            buf_ref[c, 0:SUBLANES, :] = jnp.zeros((SUBLANES, tf), F32)

    for c in range(n_chunks):
        cols = slice(c * tf, (c + 1) * tf)
        buf_ref[c, SUBLANES:SUBLANES + tm, :] = _nn(h_ref[...], wup_ref[:, cols])
        conv = cb_ref[:, cols]
        for tap in range(FFN_CONV):
            off = SUBLANES - (FFN_CONV - 1) + tap
            conv = conv + cw_ref[tap:tap + 1, cols] * buf_ref[c, off:off + tm, :]
        buf_ref[c, 0:SUBLANES, :] = buf_ref[c, tm:tm + SUBLANES, :]
        up = _nn(h_ref[...], wup_ref[:, f + c * tf:f + (c + 1) * tf])
        act_ref[:, cols] = (_silu(conv) * up).astype(BF16)

    y = _nn(act_ref[...], wd_ref[...])
    r = alpha * x + (1.0 + mod_ref[0, 5:6, :]) * y
    out_ref[...] = _layer_norm_rows(r, g_ref[...], b_ref[...], LN_EPS)


def _resident(shape, layer=None):
    if layer is None:
        return pl.BlockSpec(shape, lambda *_: (0,) * len(shape), pipeline_mode=pl.Buffered(1))
    return pl.BlockSpec((None,) + tuple(shape), lambda *_: (layer,) + (0,) * len(shape),
                        pipeline_mode=pl.Buffered(1))


def _ffn(x, mod, w_up, conv_w, conv_b, w_down, ln_g, ln_b, *, layer, seq, alpha, tm=512, n_chunks=2):
    t, d = x.shape
    f = w_down.shape[1]
    tf = f // n_chunks
    tps = seq // tm
    return pl.pallas_call(
        functools.partial(_ffn_kernel, tiles_per_seq=tps, alpha=alpha),
        grid=(t // tm,),
        in_specs=[
            pl.BlockSpec((tm, d), lambda i: (i, 0)),
            pl.BlockSpec((1, 6, d), lambda i: (i // tps, 0, 0)),
            _resident((d, 2 * f), layer),
            _resident((FFN_CONV, f)),
            _resident((1, f)),
            _resident((f, d), layer),
            _resident((1, d)),
            _resident((1, d)),
        ],
        out_specs=pl.BlockSpec((tm, d), lambda i: (i, 0)),
        out_shape=jax.ShapeDtypeStruct((t, d), F32),
        scratch_shapes=[
            pltpu.VMEM((tm, d), BF16),
            pltpu.VMEM((n_chunks, tm + SUBLANES, tf), F32),
            pltpu.VMEM((tm, f), BF16),
        ],
        compiler_params=_params("arbitrary"),
        name="ffn",
    )(x, mod, w_up, conv_w, conv_b.reshape(1, f), w_down, ln_g.reshape(1, d), ln_b.reshape(1, d))


def _gdn_in_kernel(x_ref, mod_ref, wqkv_ref, wz_ref, wab_ref, cw_ref, alog_ref, dtb_ref,
                   q_ref, k_ref, v_ref, z_ref, gb_ref, gt_ref, buf_ref, *, tiles_per_seq, chunk):
    i = pl.program_id(0)
    tm, d = x_ref.shape
    nh = GDN_HEADS
    hd = d // nh
    h = _modulate(x_ref[...], mod_ref, 0, 1)

    @pl.when(i % tiles_per_seq == 0)
    def _():
        buf_ref[0:SUBLANES, :] = jnp.zeros((SUBLANES, buf_ref.shape[1]), F32)

    outs = (q_ref, k_ref, v_ref)
    for s in range(3):
        cs = slice(s * d, (s + 1) * d)
        buf_ref[SUBLANES:SUBLANES + tm, cs] = _nn(h, wqkv_ref[:, cs])
        y = None
        for tap in range(GDN_CONV):
            off = SUBLANES - (GDN_CONV - 1) + tap
            term = cw_ref[tap:tap + 1, cs] * buf_ref[off:off + tm, cs]
            y = term if y is None else y + term
        y = _silu(y)
        if s < 2:
            scale = hd ** -0.5 if s == 0 else 1.0
            for hh in range(nh):
                seg = y[:, hh * hd:(hh + 1) * hd]
                inv = lax.rsqrt(jnp.sum(seg * seg, axis=-1, keepdims=True) + 1e-6)
                outs[s][:, hh * hd:(hh + 1) * hd] = (seg * (inv * scale)).astype(BF16)
        else:
            outs[s][...] = y.astype(BF16)
    buf_ref[0:SUBLANES, :] = buf_ref[tm:tm + SUBLANES, :]

    z_ref[...] = _nn(h, wz_ref[...]).astype(BF16)

    pab = _nn(h, wab_ref[...])
    lane = lax.broadcasted_iota(jnp.int32, pab.shape, 1)
    g = -jnp.exp(alog_ref[...]) * _softplus(pab + dtb_ref[...])
    gb = jnp.where(lane < nh, g, jnp.where(lane < 2 * nh, _sigmoid(pab), 0.0))
    row = lax.broadcasted_iota(jnp.int32, (tm, tm), 0)
    col = lax.broadcasted_iota(jnp.int32, (tm, tm), 1)
    tri = jnp.where(col <= row, jnp.where(jnp.bitwise_xor(row, col) < chunk, 1.0, 0.0), 0.0).astype(BF16)
    hi, mid, lo = _split3(gb)
    cum = _nn(tri, hi) + _nn(tri, mid) + _nn(tri, lo)
    gb = jnp.where(lane < nh, cum, gb)
    gb_ref[...] = gb
    gt_ref[...] = gb.T[0:2 * nh, :]


def _gdn_in(x, mod, w_in, conv_w, a_log, dt_bias, *, seq, tm=256):
    t, d = x.shape
    nh = GDN_HEADS
    tps = seq // tm
    w_b = w_in.astype(BF16)
    w_ab = jnp.pad(w_b[:, 4 * d:], ((0, 0), (0, LANES - 2 * nh)))
    once = pl.Buffered(1)
    alog = jnp.pad(a_log, (0, LANES - nh)).reshape(1, LANES)
    dtb = jnp.pad(dt_bias, (0, LANES - nh)).reshape(1, LANES)
    row = lambda i: (i, 0)
    fixed = lambda i: (0, 0)
    return pl.pallas_call(
        functools.partial(_gdn_in_kernel, tiles_per_seq=tps, chunk=GDN_CHUNK),
        grid=(t // tm,),
        in_specs=[
            pl.BlockSpec((tm, d), row),
            pl.BlockSpec((1, 6, d), lambda i: (i // tps, 0, 0)),
            pl.BlockSpec((d, 3 * d), fixed, pipeline_mode=once),
            pl.BlockSpec((d, d), lambda i: (0, 3), pipeline_mode=once),
            pl.BlockSpec((d, LANES), fixed, pipeline_mode=once),
            pl.BlockSpec((GDN_CONV, 3 * d), fixed, pipeline_mode=once),
            pl.BlockSpec((1, LANES), fixed),
            pl.BlockSpec((1, LANES), fixed),
        ],
        out_specs=[pl.BlockSpec((tm, d), row)] * 4 + [pl.BlockSpec((tm, LANES), row),
                                                      pl.BlockSpec((2 * nh, tm), lambda i: (0, i))],
        out_shape=[jax.ShapeDtypeStruct((t, d), BF16)] * 4 + [jax.ShapeDtypeStruct((t, LANES), F32),
                                                              jax.ShapeDtypeStruct((2 * nh, t), F32)],
        scratch_shapes=[pltpu.VMEM((tm + SUBLANES, 3 * d), F32)],
        compiler_params=_params("arbitrary"),
        name="gdn_in",
    )(x, mod, w_b, w_b, w_ab, conv_w, alog, dtb)


def _gdn_core_kernel(q_ref, k_ref, v_ref, z_ref, gb_ref, gt_ref, nw_ref, o_ref,
                     state_ref, inv_ref, pw_ref, rhs_ref, u_ref, w_ref, qk_ref, qg_ref, kd_ref, dl_ref,
                     vn_ref, oi_ref, *, chunk):
    rows, d = q_ref.shape
    nh = GDN_HEADS
    hd = d // nh
    n_chunks = rows // chunk
    heads = range(nh)
    lanes = [slice(hh * hd, (hh + 1) * hd) for hh in heads]

    @pl.when(pl.program_id(1) == 0)
    def _():
        state_ref[...] = jnp.zeros(state_ref.shape, F32)

    row = lax.broadcasted_iota(jnp.int32, (rows, rows), 0)
    col = lax.broadcasted_iota(jnp.int32, (rows, rows), 1)
    same_chunk = jnp.bitwise_xor(row, col) < chunk
    gb = gb_ref[...]
    for hh in heads:
        kh = k_ref[:, lanes[hh]]
        kf = kh.astype(F32)
        qh = q_ref[:, lanes[hh]]
        gc = jnp.broadcast_to(gb[:, hh:hh + 1], (rows, hd))
        beta = jnp.broadcast_to(gb[:, nh + hh:nh + hh + 1], (rows, hd))
        diff = jnp.broadcast_to(gb[:, hh:hh + 1], (rows, rows)) - gt_ref[hh:hh + 1, :]
        causal = same_chunk & (row >= col)
        decay = jnp.where(causal, jnp.exp(jnp.where(causal, diff, 0.0)), 0.0)
        kb = kf * beta
        a = jnp.where(row > col, _nt(kb.astype(BF16), kh) * decay, 0.0)
        inv_ref[hh] = jnp.where(row == col, 1.0, 0.0) - a
        pw_ref[hh] = a.astype(BF16)
        qk_ref[hh] = (_nt(qh, kh) * decay).astype(BF16)
        rhs_ref[hh, :, 0:hd] = (v_ref[:, lanes[hh]].astype(F32) * beta).astype(BF16)
        rhs_ref[hh, :, hd:2 * hd] = (kb * jnp.exp(gc)).astype(BF16)
        qg_ref[hh] = (qh.astype(F32) * jnp.exp(gc)).astype(BF16)
        for c in range(n_chunks):
            rs = slice(c * chunk, (c + 1) * chunk)
            g_last = gc[(c + 1) * chunk - 1:(c + 1) * chunk, :]
            kd_ref[hh, rs, :] = (kf[rs] * jnp.exp(g_last - gc[rs])).astype(BF16)
            dl_ref[hh, c] = jnp.exp(g_last)

    for _ in range(int(math.log2(chunk)) - 1):
        for hh in heads:
            power = _nn(pw_ref[hh], pw_ref[hh]).astype(BF16)
            pw_ref[hh] = power
            inv = inv_ref[hh]
            inv_ref[hh] = inv + _nn(inv.astype(BF16), power)
    for hh in heads:
        sol = _nn(inv_ref[hh].astype(BF16), rhs_ref[hh])
        u_ref[hh] = sol[:, 0:hd]
        w_ref[hh] = sol[:, hd:2 * hd].astype(BF16)

    for c in range(n_chunks):
        rs = slice(c * chunk, (c + 1) * chunk)
        states = [state_ref[hh].astype(BF16) for hh in heads]
        v_new = [(u_ref[hh, rs, :] - _nn(w_ref[hh, rs, :], states[hh])).astype(BF16) for hh in heads]
        for hh in heads:
            vn_ref[hh, rs, :] = v_new[hh]
            state_ref[hh] = state_ref[hh] * dl_ref[hh, c] + _tn(kd_ref[hh, rs, :], v_new[hh])
            oi_ref[hh, rs, :] = _nn(qg_ref[hh, rs, :], states[hh])

    for hh in heads:
        o = oi_ref[hh] + _nn(qk_ref[hh], vn_ref[hh])
        o = o * lax.rsqrt(jnp.mean(o * o, axis=-1, keepdims=True) + 1e-6) * nw_ref[...]
        o_ref[:, lanes[hh]] = (o * _silu(z_ref[:, lanes[hh]].astype(F32))).astype(BF16)


def _gdn_core(q, k, v, z, gb, gt, norm_w, *, batch, seq, rows=256):
    t, d = q.shape
    n = seq // rows
    nh = GDN_HEADS
    hd = d // nh
    row = lambda b, j: (b * n + j, 0)
    return pl.pallas_call(
        functools.partial(_gdn_core_kernel, chunk=GDN_CHUNK),
        grid=(batch, n),
        in_specs=[pl.BlockSpec((rows, d), row)] * 4 + [
            pl.BlockSpec((rows, LANES), row),
            pl.BlockSpec((2 * nh, rows), lambda b, j: (0, b * n + j)),
            pl.BlockSpec((1, hd), lambda b, j: (0, 0)),
        ],
        out_specs=pl.BlockSpec((rows, d), row),
        out_shape=jax.ShapeDtypeStruct((t, d), BF16),
        scratch_shapes=[
            pltpu.VMEM((nh, hd, hd), F32),
            pltpu.VMEM((nh, rows, rows), F32),
            pltpu.VMEM((nh, rows, rows), BF16),
            pltpu.VMEM((nh, rows, 2 * hd), BF16),
            pltpu.VMEM((nh, rows, hd), F32),
            pltpu.VMEM((nh, rows, hd), BF16),
            pltpu.VMEM((nh, rows, rows), BF16),
            pltpu.VMEM((nh, rows, hd), BF16),
            pltpu.VMEM((nh, rows, hd), BF16),
            pltpu.VMEM((nh, rows // GDN_CHUNK, 1, hd), F32),
            pltpu.VMEM((nh, rows, hd), BF16),
            pltpu.VMEM((nh, rows, hd), F32),
        ],
        compiler_params=_params("arbitrary", "arbitrary"),
        name="gdn_core",
    )(q, k, v, z, gb, gt, norm_w.reshape(1, hd))


def _rope_table_kernel(cos_ref, sin_ref):
    ts, half = cos_ref.shape
    pos = (lax.broadcasted_iota(jnp.int32, (ts, half), 0) + pl.program_id(0) * ts).astype(F32)
    frac = lax.broadcasted_iota(jnp.int32, (ts, half), 1).astype(F32) / (half - 1.0)
    ang = pos * jnp.exp(-frac * math.log(RET_ROPE_BASE))
    cos_ref[...] = jnp.cos(ang)
    sin_ref[...] = jnp.sin(ang)


def _rope_tables(seq, half, ts=256):
    spec = pl.BlockSpec((ts, half), lambda i: (i, 0))
    return pl.pallas_call(
        _rope_table_kernel,
        grid=(seq // ts,),
        out_specs=[spec, spec],
        out_shape=[jax.ShapeDtypeStruct((seq, half), F32)] * 2,
        compiler_params=_params("arbitrary"),
        name="rope_tables",
    )()


def _ret_in_kernel(x_ref, mod_ref, w_ref, cos_ref, sin_ref, q_ref, k_ref, v_ref, gate_ref):
    d = x_ref.shape[1]
    nh = RET_HEADS
    dk = d // nh
    half = dk // 2
    h = _modulate(x_ref[...], mod_ref, 0, 1)
    cos_a = cos_ref[...]
    sin_a = sin_ref[...]
    for s, (out, scale) in enumerate(((q_ref, 1.0), (k_ref, dk ** -0.5))):
        t = _nn(h, w_ref[:, s * d:(s + 1) * d])
        for hh in range(nh):
            t1 = t[:, hh * dk:hh * dk + half]
            t2 = t[:, hh * dk + half:(hh + 1) * dk]
            out[:, hh * dk:hh * dk + half] = ((t1 * cos_a - t2 * sin_a) * scale).astype(BF16)
            out[:, hh * dk + half:(hh + 1) * dk] = ((t1 * sin_a + t2 * cos_a) * scale).astype(BF16)
    v_ref[...] = _nn(h, w_ref[:, 2 * d:4 * d]).astype(BF16)
    gate_ref[...] = _nn(h, w_ref[:, 4 * d:6 * d])


def _ret_in(x, mod, w_in, cos_t, sin_t, *, seq, tm=256):
    t, d = x.shape
    tps = seq // tm
    half = cos_t.shape[1]
    row = lambda i: (i, 0)
    return pl.pallas_call(
        _ret_in_kernel,
        grid=(t // tm,),
        in_specs=[
            pl.BlockSpec((tm, d), row),
            pl.BlockSpec((1, 6, d), lambda i: (i // tps, 0, 0)),
            pl.BlockSpec((d, 6 * d), lambda i: (0, 0)),
            pl.BlockSpec((tm, half), lambda i: (i % tps, 0)),
            pl.BlockSpec((tm, half), lambda i: (i % tps, 0)),
        ],
        out_specs=[pl.BlockSpec((tm, d), row), pl.BlockSpec((tm, d), row),
                   pl.BlockSpec((tm, 2 * d), row), pl.BlockSpec((tm, 2 * d), row)],
        out_shape=[jax.ShapeDtypeStruct((t, d), BF16), jax.ShapeDtypeStruct((t, d), BF16),
                   jax.ShapeDtypeStruct((t, 2 * d), BF16), jax.ShapeDtypeStruct((t, 2 * d), F32)],
        compiler_params=_params("arbitrary"),
        name="ret_in",
    )(x, mod, w_in.astype(BF16), cos_t, sin_t)


def _ret_core_kernel(q_ref, k_ref, v_ref, gate_ref, o_ref, state_ref):
    c, d = q_ref.shape
    nh = RET_HEADS
    dk = d // nh
    dv = v_ref.shape[1] // nh

    @pl.when(pl.program_id(1) == 0)
    def _():
        state_ref[...] = jnp.zeros(state_ref.shape, F32)

    rel = (lax.broadcasted_iota(jnp.int32, (c, c), 0)
           - lax.broadcasted_iota(jnp.int32, (c, c), 1)).astype(F32)
    idx = lax.broadcasted_iota(jnp.int32, (c, 1), 0).astype(F32)
    for hh in range(nh):
        log_gamma = math.log(1.0 - 2.0 ** (-5.0 - hh))
        dmask = jnp.where(rel >= 0, jnp.exp(jnp.maximum(rel, 0.0) * log_gamma), 0.0)
        zeta = jnp.exp((c - 1.0 - idx) * log_gamma)
        xi = jnp.exp((idx + 1.0) * log_gamma)
        qh = q_ref[:, hh * dk:(hh + 1) * dk]
        kh = k_ref[:, hh * dk:(hh + 1) * dk]
        vh = v_ref[:, hh * dv:(hh + 1) * dv]
        scores = _nt(qh, kh) * dmask
        state = state_ref[hh]
        o = _nn(scores.astype(BF16), vh) + _nn(qh, state.astype(BF16)) * xi
        state_ref[hh] = (state * math.exp(c * log_gamma)
                         + _tn((kh.astype(F32) * zeta).astype(BF16), vh))
        mu = jnp.mean(o, axis=-1, keepdims=True)
        oc = o - mu
        var = jnp.mean(oc * oc, axis=-1, keepdims=True)
        o = oc * lax.rsqrt(var + 1e-6)
        o_ref[:, hh * dv:(hh + 1) * dv] = (o * _silu(gate_ref[:, hh * dv:(hh + 1) * dv])).astype(BF16)


def _ret_core(q, k, v, gate, *, batch, seq):
    t, d = q.shape
    c = RET_CHUNK
    n = seq // c
    dk = d // RET_HEADS
    dv = v.shape[1] // RET_HEADS
    row = lambda b, j: (b * n + j, 0)
    return pl.pallas_call(
        _ret_core_kernel,
        grid=(batch, n),
        in_specs=[pl.BlockSpec((c, d), row), pl.BlockSpec((c, d), row),
                  pl.BlockSpec((c, 2 * d), row), pl.BlockSpec((c, 2 * d), row)],
        out_specs=pl.BlockSpec((c, 2 * d), row),
        out_shape=jax.ShapeDtypeStruct((t, 2 * d), BF16),
        scratch_shapes=[pltpu.VMEM((RET_HEADS, dk, dv), F32)],
        compiler_params=_params("arbitrary", "arbitrary"),
        name="ret_core",
    )(q, k, v, gate)


def _gelu(t):
    p = 0.3275911 / math.sqrt(2.0)
    half_coefs = [0.5 * a for a in (0.254829592, -0.284496736, 1.421413741, -1.453152027, 1.061405429)]
    u = 1.0 / (1.0 + p * jnp.abs(t))
    poly = half_coefs[4]
    for coef in half_coefs[3::-1]:
        poly = poly * u + coef
    e = poly * u * jnp.exp2(t * t * (-0.5 * math.log2(math.e)))
    return t * jnp.where(t >= 0.0, 1.0 - e, e)


def _gmlp_kernel(x_ref, mod_ref, win_ref, lng_ref, lnb_ref, ws_ref, bs_ref, wout_ref, g_ref, b_ref,
                 out_ref, *, alpha):
    tm, d = x_ref.shape
    width = win_ref.shape[1] // 2
    c = GMLP_CHUNK
    ng = GMLP_GROUPS
    gw = width // ng
    x = x_ref[...]
    h = _modulate(x, mod_ref, 0, 1)

    v = _gelu(_nn(h, win_ref[:, width:]))
    v = _layer_norm_rows(v, lng_ref[...], lnb_ref[...], LN_EPS).astype(BF16)
    u = _gelu(_nn(h, win_ref[:, :width]))
    row = lax.broadcasted_iota(jnp.int32, (c, c), 0)
    col = lax.broadcasted_iota(jnp.int32, (c, c), 1)
    parts = []
    for n in range(tm // c):
        rs = slice(n * c, (n + 1) * c)
        groups = []
        for gi in range(ng):
            ls = slice(gi * gw, (gi + 1) * gw)
            ws = jnp.where(row >= col, ws_ref[gi], 0.0).astype(BF16)
            vs = _nn(ws, v[rs, ls]) + bs_ref[:, gi:gi + 1]
            groups.append((u[rs, ls] * vs).astype(BF16))
        parts.append(jnp.concatenate(groups, axis=1))
    gated = jnp.concatenate(parts, axis=0)
    y = _nn(gated, wout_ref[...])
    r = alpha * x + (1.0 + mod_ref[0, 2:3, :]) * y
    out_ref[...] = _layer_norm_rows(r, g_ref[...], b_ref[...], LN_EPS)


def _gmlp(x, mod, w_in, ln_g, ln_b, w_s, b_s, w_out, res_g, res_b, *, seq, alpha, tm=256):
    t, d = x.shape
    width = w_out.shape[0]
    ng, c, _ = w_s.shape
    tps = seq // tm
    fixed = lambda i: (0, 0)
    return pl.pallas_call(
        functools.partial(_gmlp_kernel, alpha=alpha),
        grid=(t // tm,),
        in_specs=[
            pl.BlockSpec((tm, d), lambda i: (i, 0)),
            pl.BlockSpec((1, 6, d), lambda i: (i // tps, 0, 0)),
            pl.BlockSpec((d, 2 * width), fixed),
            pl.BlockSpec((1, width), fixed),
            pl.BlockSpec((1, width), fixed),
            pl.BlockSpec((ng, c, c), lambda i: (0, 0, 0)),
            pl.BlockSpec((c, ng), fixed),
            pl.BlockSpec((width, d), fixed),
            pl.BlockSpec((1, d), fixed),
            pl.BlockSpec((1, d), fixed),
        ],
        out_specs=pl.BlockSpec((tm, d), lambda i: (i, 0)),
        out_shape=jax.ShapeDtypeStruct((t, d), F32),
        compiler_params=_params("arbitrary"),
        name="gmlp",
    )(x, mod, w_in.astype(BF16), ln_g.reshape(1, width), ln_b.reshape(1, width), w_s, b_s.T,
      w_out.astype(BF16), res_g.reshape(1, d), res_b.reshape(1, d))


def _sb_in_kernel(x_ref, mod_ref, w_ref, q_ref, k_ref, v_ref, *, q_scale):
    d = x_ref.shape[1]
    h = _modulate(x_ref[...], mod_ref, 0, 1)
    q_ref[...] = (_nn(h, w_ref[:, 0:d]) * q_scale).astype(BF16)
    k_ref[...] = _nn(h, w_ref[:, d:2 * d]).astype(BF16)
    v_ref[...] = _nn(h, w_ref[:, 2 * d:3 * d]).astype(BF16)


def _sb_in(x, mod, w_in, *, seq, tm=512):
    t, d = x.shape
    tps = seq // tm
    row = lambda i: (i, 0)
    return pl.pallas_call(
        functools.partial(_sb_in_kernel, q_scale=(d // SB_HEADS) ** -0.5 * math.log2(math.e)),
        grid=(t // tm,),
        in_specs=[
            pl.BlockSpec((tm, d), row),
            pl.BlockSpec((1, 6, d), lambda i: (i // tps, 0, 0)),
            pl.BlockSpec((d, 3 * d), lambda i: (0, 0)),
        ],
        out_specs=[pl.BlockSpec((tm, d), row)] * 3,
        out_shape=[jax.ShapeDtypeStruct((t, d), BF16)] * 3,
        compiler_params=_params("arbitrary"),
        name="sb_in",
    )(x, mod, w_in.astype(BF16))


def _sb_core_kernel(q_ref, k_ref, v_ref, o_ref, qh_ref, sp_ref, zc_ref, carry_ref, acc_ref, knorm_ref,
                    bound_ref):
    tq, lanes = q_ref.shape
    seq = k_ref.shape[0]
    dh = lanes // 2
    qi = pl.program_id(2)
    q = q_ref[...]
    lane = lax.broadcasted_iota(jnp.int32, (tq, lanes), 1)
    head_lanes = (lane < dh, lane >= dh)
    row = lax.broadcasted_iota(jnp.int32, (tq, tq), 0)
    col = lax.broadcasted_iota(jnp.int32, (tq, tq), 1)
    ones_lower = jnp.where(row >= col, 1.0, 0.0).astype(BF16)

    @pl.when(qi == 0)
    def _():
        def key_tile(t, best):
            kf = k_ref[pl.ds(pl.multiple_of(t * tq, tq), tq), :].astype(F32)
            return jnp.maximum(best, jnp.max(kf * kf, axis=0, keepdims=True))

        col_max = lax.fori_loop(0, seq // tq, key_tile, jnp.zeros((1, lanes), F32))
        for hh in range(2):
            bound_sq = jnp.sum(jnp.where(head_lanes[hh][0:1, :], col_max, 0.0), axis=-1, keepdims=True)
            knorm_ref[hh] = jnp.broadcast_to(bound_sq, knorm_ref.shape[1:])

    qf = q.astype(F32)
    for hh in range(2):
        qh_ref[hh] = jnp.where(head_lanes[hh], q, jnp.zeros_like(q))
        qsq = jnp.sum(jnp.where(head_lanes[hh], qf * qf, 0.0), axis=-1, keepdims=True)
        bound_ref[hh] = jnp.sqrt(qsq * knorm_ref[hh, 0:1, 0:1]) * NORM_BOUND_SLACK
    acc_ref[...] = jnp.zeros(acc_ref.shape, F32)
    carry_ref[...] = jnp.zeros(carry_ref.shape, F32)

    def stage(kt, slot, diagonal=False):
        k = k_ref[pl.ds(pl.multiple_of(kt * tq, tq), tq), :]
        for hh in range(2):
            z = _nt(qh_ref[hh], k)
            sp = jnp.maximum(z, 0.0) + jnp.log2(1.0 + jnp.exp2(-jnp.abs(z)))
            carry = carry_ref[hh]
            zc = z - carry
            if diagonal:
                sp = jnp.where(col < row, sp, 0.0)
                zc = jnp.where(col < row, zc, -1e30)
            sp_ref[slot, hh] = sp.astype(BF16)
            zc_ref[slot, hh] = zc
            carry_ref[hh] = carry + jnp.sum(sp, axis=-1, keepdims=True)

    def consume(kt, slot):
        v = v_ref[pl.ds(pl.multiple_of(kt * tq, tq), tq), :]
        for hh in range(2):
            inclusive = _nn(sp_ref[slot, hh], ones_lower)
            a = jnp.exp2(zc_ref[slot, hh] - inclusive)
            acc_ref[hh] += _nn(a.astype(BF16), v)

    def still_alive():
        reach = jnp.max(jnp.maximum(bound_ref[0] - carry_ref[0], bound_ref[1] - carry_ref[1]))
        return (reach > -DEAD_EXPONENT).astype(jnp.int32)

    stage(qi, 0, diagonal=True)

    @pl.when(qi == 0)
    def _():
        consume(0, 0)

    @pl.when(qi > 0)
    def _():
        stage(qi - 1, 1)
        alive = still_alive()
        consume(qi, 0)
        n_pairs = (qi - 1) // 2

        def more(state):
            return (state[0] < n_pairs) & (state[1] > 0)

        def pair(state):
            kt = qi - 1 - 2 * state[0]
            stage(kt - 1, 0)
            consume(kt, 1)
            stage(kt - 2, 1)
            alive = still_alive()
            consume(kt - 1, 0)
            return state[0] + 1, alive

        pairs_done, alive = lax.while_loop(more, pair, (jnp.int32(0), alive))
        staged = qi - 1 - 2 * pairs_done

        @pl.when((alive == 0) | (staged == 0))
        def _():
            consume(staged, 1)

        @pl.when((alive > 0) & (staged == 1))
        def _():
            stage(0, 0)
            consume(1, 1)
            consume(0, 0)

    o_ref[...] = jnp.where(head_lanes[0], acc_ref[0], acc_ref[1]).astype(BF16)


def _sb_core(q, k, v, *, batch, seq, tq=256):
    t, d = q.shape
    tq = min(tq, seq)
    nq = seq // tq
    groups = d // LANES
    return pl.pallas_call(
        _sb_core_kernel,
        grid=(batch, groups, nq),
        in_specs=[
            pl.BlockSpec((tq, LANES), lambda b, g, i: (b * nq + i, g)),
            pl.BlockSpec((seq, LANES), lambda b, g, i: (b, g)),
            pl.BlockSpec((seq, LANES), lambda b, g, i: (b, g)),
        ],
        out_specs=pl.BlockSpec((tq, LANES), lambda b, g, i: (b * nq + i, g)),
        out_shape=jax.ShapeDtypeStruct((t, d), BF16),
        scratch_shapes=[
            pltpu.VMEM((2, tq, LANES), BF16),
            pltpu.VMEM((2, 2, tq, tq), BF16),
            pltpu.VMEM((2, 2, tq, tq), F32),
            pltpu.VMEM((2, tq, 1), F32),
            pltpu.VMEM((2, tq, LANES), F32),
            pltpu.VMEM((2, SUBLANES, LANES), F32),
            pltpu.VMEM((2, tq, 1), F32),
        ],
        compiler_params=_params("arbitrary", "arbitrary", "arbitrary"),
        name="sb_core",
    )(q, k, v)


def kernel(x, c, cond_w, cond_b, ada_w, ada_b, ln_g, ln_b, ffn_up, ffn_conv_w, ffn_conv_b, ffn_down,
           gdn_w_in, gdn_conv_w, gdn_a_log, gdn_dt_bias, gdn_norm_w, gdn_w_out,
           ret_w_in, ret_w_out,
           gmlp_w_in, gmlp_ln_g, gmlp_ln_b, gmlp_w_s, gmlp_b_s, gmlp_w_out,
           sb_w_in, sb_w_out):
    batch, seq, d = x.shape
    depth = ada_w.shape[0]
    alpha = (2.0 * depth) ** 0.25
    mods = _conditioning(c, cond_w, cond_b, ada_w, ada_b).reshape(depth, batch, 6, d)
    xt = x.reshape(batch * seq, d)
    ffn_up_b = ffn_up.astype(BF16)
    ffn_down_b = ffn_down.astype(BF16)

    for i in range(depth):
        mod = mods[i]
        res = dict(seq=seq, gate_row=2, alpha=alpha)
        mixer = i % 4
        if mixer == 0:
            q, k, v, z, gb, gt = _gdn_in(xt, mod, gdn_w_in, gdn_conv_w, gdn_a_log, gdn_dt_bias, seq=seq)
            o = _gdn_core(q, k, v, z, gb, gt, gdn_norm_w, batch=batch, seq=seq)
            xt = _out_proj_ln(o, gdn_w_out, xt, mod, ln_g[i, 0], ln_b[i, 0], **res)
        elif mixer == 1:
            cos_t, sin_t = _rope_tables(seq, d // RET_HEADS // 2)
            q, k, v, gate = _ret_in(xt, mod, ret_w_in, cos_t, sin_t, seq=seq)
            o = _ret_core(q, k, v, gate, batch=batch, seq=seq)
            xt = _out_proj_ln(o, ret_w_out, xt, mod, ln_g[i, 0], ln_b[i, 0], **res)
        elif mixer == 2:
            xt = _gmlp(xt, mod, gmlp_w_in, gmlp_ln_g, gmlp_ln_b, gmlp_w_s, gmlp_b_s, gmlp_w_out,
                       ln_g[i, 0], ln_b[i, 0], seq=seq, alpha=alpha)
        else:
            q, k, v = _sb_in(xt, mod, sb_w_in, seq=seq)
            o = _sb_core(q, k, v, batch=batch, seq=seq)
            xt = _out_proj_ln(o, sb_w_out, xt, mod, ln_g[i, 0], ln_b[i, 0], **res)
        xt = _ffn(xt, mod, ffn_up_b, ffn_conv_w[i], ffn_conv_b[i], ffn_down_b,
                  ln_g[i, 1], ln_b[i, 1], layer=i, seq=seq, alpha=alpha)
    return xt.reshape(batch, seq, d)
```

```python
import functools
import math

import jax
import jax.numpy as jnp
from jax import lax
from jax.experimental import pallas as pl
from jax.experimental.pallas import tpu as pltpu

F32 = jnp.float32
BF16 = jnp.bfloat16

LANES = 128
SUBLANES = 8
VMEM_LIMIT = 56 * 1024 * 1024

LN_EPS = 1e-5
GDN_HEADS = 8
GDN_CHUNK = 64
GDN_CONV = 4
RET_HEADS = 4
RET_CHUNK = 128
RET_ROPE_BASE = 10000.0
GMLP_CHUNK = 128
GMLP_GROUPS = 8
SB_HEADS = 16
FFN_CONV = 3
DEAD_EXPONENT = 160.0
NORM_BOUND_SLACK = 1.001


def _params(*sem):
    return pltpu.CompilerParams(dimension_semantics=sem, vmem_limit_bytes=VMEM_LIMIT)


def _nn(a, b):
    return jnp.dot(a, b, preferred_element_type=F32)


def _nt(a, b):
    return lax.dot_general(a, b, (((1,), (1,)), ((), ())), preferred_element_type=F32)


def _tn(a, b):
    return lax.dot_general(a, b, (((0,), (0,)), ((), ())), preferred_element_type=F32)


def _sigmoid(x):
    return 1.0 / (1.0 + jnp.exp(-x))


def _silu(x):
    return x * _sigmoid(x)


def _softplus(x):
    return jnp.maximum(x, 0.0) + jnp.log(1.0 + jnp.exp(-jnp.abs(x)))


def _split3(x):
    hi = x.astype(BF16)
    r = x - hi.astype(F32)
    mid = r.astype(BF16)
    lo = (r - mid.astype(F32)).astype(BF16)
    return hi, mid, lo


def _modulate(x, mod_ref, shift_row, scale_row):
    return (x * (1.0 + mod_ref[0, scale_row:scale_row + 1, :])
            + mod_ref[0, shift_row:shift_row + 1, :]).astype(BF16)


def _layer_norm_rows(r, g, b, eps):
    mu = jnp.mean(r, axis=-1, keepdims=True)
    rc = r - mu
    var = jnp.mean(rc * rc, axis=-1, keepdims=True)
    return rc * lax.rsqrt(var + eps) * g + b


def _lane_replicated_columns(rows):
    n, k = rows.shape
    padded = jnp.concatenate([rows, jnp.zeros((LANES - n, k), F32)], axis=0)
    t = padded.T
    return [jnp.broadcast_to(t[:, b:b + 1], (k, LANES)) for b in range(n)]


def _rowvec_matmul(col, w):
    n = w.shape[1]
    parts = [jnp.sum(col * w[:, c:c + LANES], axis=0, keepdims=True) for c in range(0, n, LANES)]
    return jnp.concatenate(parts, axis=1)


def _cond_kernel(c_ref, cw_ref, cb_ref, aw_ref, ab_ref, o_ref, ecol_ref):
    nb = c_ref.shape[0]

    @pl.when((pl.program_id(0) == 0) & (pl.program_id(1) == 0))
    def _():
        ccols = _lane_replicated_columns(c_ref[...])
        cw = cw_ref[...]
        e = jnp.concatenate([_rowvec_matmul(col, cw) for col in ccols], axis=0) + cb_ref[...]
        ecols = _lane_replicated_columns(_silu(e))
        for b in range(nb):
            ecol_ref[b] = ecols[b]

    w = aw_ref[0]
    for b in range(nb):
        o_ref[0, b:b + 1, :] = _rowvec_matmul(ecol_ref[b], w) + ab_ref[0]


def _conditioning(c, cond_w, cond_b, ada_w, ada_b):
    nb, d = c.shape
    depth, _, n6 = ada_w.shape
    tn = 512
    return pl.pallas_call(
        _cond_kernel,
        grid=(depth, n6 // tn),
        in_specs=[
            pl.BlockSpec((nb, d), lambda i, j: (0, 0)),
            pl.BlockSpec((d, d), lambda i, j: (0, 0)),
            pl.BlockSpec((1, d), lambda i, j: (0, 0)),
            pl.BlockSpec((1, d, tn), lambda i, j: (i, 0, j)),
            pl.BlockSpec((1, 1, tn), lambda i, j: (i, 0, j)),
        ],
        out_specs=pl.BlockSpec((1, nb, tn), lambda i, j: (i, 0, j)),
        out_shape=jax.ShapeDtypeStruct((depth, nb, n6), F32),
        scratch_shapes=[pltpu.VMEM((nb, d, LANES), F32)],
        compiler_params=_params("arbitrary", "arbitrary"),
        name="conditioning",
    )(c, cond_w, cond_b.reshape(1, d), ada_w, ada_b.reshape(depth, 1, n6))


def _out_ln_kernel(o_ref, w_ref, x_ref, mod_ref, g_ref, b_ref, out_ref, *, gate_row, alpha):
    y = _nn(o_ref[...], w_ref[...])
    r = alpha * x_ref[...] + (1.0 + mod_ref[0, gate_row:gate_row + 1, :]) * y
    out_ref[...] = _layer_norm_rows(r, g_ref[...], b_ref[...], LN_EPS)


def _out_proj_ln(o, w_out, x, mod, ln_g, ln_b, *, seq, gate_row, alpha, tm=512):
    t, kdim = o.shape
    d = x.shape[1]
    tps = seq // tm
    return pl.pallas_call(
        functools.partial(_out_ln_kernel, gate_row=gate_row, alpha=alpha),
        grid=(t // tm,),
        in_specs=[
            pl.BlockSpec((tm, kdim), lambda i: (i, 0)),
            pl.BlockSpec((kdim, d), lambda i: (0, 0)),
            pl.BlockSpec((tm, d), lambda i: (i, 0)),
            pl.BlockSpec((1, 6, d), lambda i: (i // tps, 0, 0)),
            pl.BlockSpec((1, d), lambda i: (0, 0)),
            pl.BlockSpec((1, d), lambda i: (0, 0)),
        ],
        out_specs=pl.BlockSpec((tm, d), lambda i: (i, 0)),
        out_shape=jax.ShapeDtypeStruct((t, d), F32),
        compiler_params=_params("arbitrary"),
        name="out_proj_ln",
    )(o, w_out.astype(BF16), x, mod, ln_g.reshape(1, d), ln_b.reshape(1, d))


def _ffn_kernel(x_ref, mod_ref, wup_ref, cw_ref, cb_ref, wd_ref, g_ref, b_ref, out_ref,
                h_ref, buf_ref, act_ref, *, tiles_per_seq, alpha):
    i = pl.program_id(0)
    tm = x_ref.shape[0]
    f = wd_ref.shape[0]
    n_chunks, _, tf = buf_ref.shape
    x = x_ref[...]
    h_ref[...] = _modulate(x, mod_ref, 3, 4)

    @pl.when(i % tiles_per_seq == 0)
    def _():
        for c in range(n_chunks):
            buf_ref[c, 0:SUBLANES, :] = jnp.zeros((SUBLANES, tf), F32)

    for c in range(n_chunks):
        cols = slice(c * tf, (c + 1) * tf)
        buf_ref[c, SUBLANES:SUBLANES + tm, :] = _nn(h_ref[...], wup_ref[:, cols])
        conv = cb_ref[:, cols]
        for tap in range(FFN_CONV):
            off = SUBLANES - (FFN_CONV - 1) + tap
            conv = conv + cw_ref[tap:tap + 1, cols] * buf_ref[c, off:off + tm, :]
        buf_ref[c, 0:SUBLANES, :] = buf_ref[c, tm:tm + SUBLANES, :]
        up = _nn(h_ref[...], wup_ref[:, f + c * tf:f + (c + 1) * tf])
        act_ref[:, cols] = (_silu(conv) * up).astype(BF16)

    y = _nn(act_ref[...], wd_ref[...])
    r = alpha * x + (1.0 + mod_ref[0, 5:6, :]) * y
    out_ref[...] = _layer_norm_rows(r, g_ref[...], b_ref[...], LN_EPS)


def _resident(shape, layer=None):
    if layer is None:
        return pl.BlockSpec(shape, lambda *_: (0,) * len(shape), pipeline_mode=pl.Buffered(1))
    return pl.BlockSpec((None,) + tuple(shape), lambda *_: (layer,) + (0,) * len(shape),
                        pipeline_mode=pl.Buffered(1))


def _ffn(x, mod, w_up, conv_w, conv_b, w_down, ln_g, ln_b, *, layer, seq, alpha, tm=512, n_chunks=2):
    t, d = x.shape
    f = w_down.shape[1]
    tf = f // n_chunks
    tps = seq // tm
    return pl.pallas_call(
        functools.partial(_ffn_kernel, tiles_per_seq=tps, alpha=alpha),
        grid=(t // tm,),
        in_specs=[
            pl.BlockSpec((tm, d), lambda i: (i, 0)),
            pl.BlockSpec((1, 6, d), lambda i: (i // tps, 0, 0)),
            _resident((d, 2 * f), layer),
            _resident((FFN_CONV, f)),
            _resident((1, f)),
            _resident((f, d), layer),
            _resident((1, d)),
            _resident((1, d)),
        ],
        out_specs=pl.BlockSpec((tm, d), lambda i: (i, 0)),
        out_shape=jax.ShapeDtypeStruct((t, d), F32),
        scratch_shapes=[
            pltpu.VMEM((tm, d), BF16),
            pltpu.VMEM((n_chunks, tm + SUBLANES, tf), F32),
            pltpu.VMEM((tm, f), BF16),
        ],
        compiler_params=_params("arbitrary"),
        name="ffn",
    )(x, mod, w_up, conv_w, conv_b.reshape(1, f), w_down, ln_g.reshape(1, d), ln_b.reshape(1, d))


def _gdn_in_kernel(x_ref, mod_ref, wqkv_ref, wz_ref, wab_ref, cw_ref, alog_ref, dtb_ref,
                   q_ref, k_ref, v_ref, z_ref, gb_ref, gt_ref, buf_ref, *, tiles_per_seq, chunk):
    i = pl.program_id(0)
    tm, d = x_ref.shape
    nh = GDN_HEADS
    hd = d // nh
    h = _modulate(x_ref[...], mod_ref, 0, 1)

    @pl.when(i % tiles_per_seq == 0)
    def _():
        buf_ref[0:SUBLANES, :] = jnp.zeros((SUBLANES, buf_ref.shape[1]), F32)

    outs = (q_ref, k_ref, v_ref)
    for s in range(3):
        cs = slice(s * d, (s + 1) * d)
        buf_ref[SUBLANES:SUBLANES + tm, cs] = _nn(h, wqkv_ref[:, cs])
        y = None
        for tap in range(GDN_CONV):
            off = SUBLANES - (GDN_CONV - 1) + tap
            term = cw_ref[tap:tap + 1, cs] * buf_ref[off:off + tm, cs]
            y = term if y is None else y + term
        y = _silu(y)
        if s < 2:
            scale = hd ** -0.5 if s == 0 else 1.0
            for hh in range(nh):
                seg = y[:, hh * hd:(hh + 1) * hd]
                inv = lax.rsqrt(jnp.sum(seg * seg, axis=-1, keepdims=True) + 1e-6)
                outs[s][:, hh * hd:(hh + 1) * hd] = (seg * (inv * scale)).astype(BF16)
        else:
            outs[s][...] = y.astype(BF16)
    buf_ref[0:SUBLANES, :] = buf_ref[tm:tm + SUBLANES, :]

    z_ref[...] = _nn(h, wz_ref[...]).astype(BF16)

    pab = _nn(h, wab_ref[...])
    lane = lax.broadcasted_iota(jnp.int32, pab.shape, 1)
    g = -jnp.exp(alog_ref[...]) * _softplus(pab + dtb_ref[...])
    gb = jnp.where(lane < nh, g, jnp.where(lane < 2 * nh, _sigmoid(pab), 0.0))
    row = lax.broadcasted_iota(jnp.int32, (tm, tm), 0)
    col = lax.broadcasted_iota(jnp.int32, (tm, tm), 1)
    tri = jnp.where(col <= row, jnp.where(jnp.bitwise_xor(row, col) < chunk, 1.0, 0.0), 0.0).astype(BF16)
    hi, mid, lo = _split3(gb)
    cum = _nn(tri, hi) + _nn(tri, mid) + _nn(tri, lo)
    gb = jnp.where(lane < nh, cum, gb)
    gb_ref[...] = gb
    gt_ref[...] = gb.T[0:2 * nh, :]


def _gdn_in(x, mod, w_in, conv_w, a_log, dt_bias, *, seq, tm=256):
    t, d = x.shape
    nh = GDN_HEADS
    tps = seq // tm
    w_b = w_in.astype(BF16)
    w_ab = jnp.pad(w_b[:, 4 * d:], ((0, 0), (0, LANES - 2 * nh)))
    once = pl.Buffered(1)
    alog = jnp.pad(a_log, (0, LANES - nh)).reshape(1, LANES)
    dtb = jnp.pad(dt_bias, (0, LANES - nh)).reshape(1, LANES)
    row = lambda i: (i, 0)
    fixed = lambda i: (0, 0)
    return pl.pallas_call(
        functools.partial(_gdn_in_kernel, tiles_per_seq=tps, chunk=GDN_CHUNK),
        grid=(t // tm,),
        in_specs=[
            pl.BlockSpec((tm, d), row),
            pl.BlockSpec((1, 6, d), lambda i: (i // tps, 0, 0)),
            pl.BlockSpec((d, 3 * d), fixed, pipeline_mode=once),
            pl.BlockSpec((d, d), lambda i: (0, 3), pipeline_mode=once),
            pl.BlockSpec((d, LANES), fixed, pipeline_mode=once),
            pl.BlockSpec((GDN_CONV, 3 * d), fixed, pipeline_mode=once),
            pl.BlockSpec((1, LANES), fixed),
            pl.BlockSpec((1, LANES), fixed),
        ],
        out_specs=[pl.BlockSpec((tm, d), row)] * 4 + [pl.BlockSpec((tm, LANES), row),
                                                      pl.BlockSpec((2 * nh, tm), lambda i: (0, i))],
        out_shape=[jax.ShapeDtypeStruct((t, d), BF16)] * 4 + [jax.ShapeDtypeStruct((t, LANES), F32),
                                                              jax.ShapeDtypeStruct((2 * nh, t), F32)],
        scratch_shapes=[pltpu.VMEM((tm + SUBLANES, 3 * d), F32)],
        compiler_params=_params("arbitrary"),
        name="gdn_in",
    )(x, mod, w_b, w_b, w_ab, conv_w, alog, dtb)


def _gdn_core_kernel(q_ref, k_ref, v_ref, z_ref, gb_ref, gt_ref, nw_ref, o_ref,
                     state_ref, inv_ref, pw_ref, rhs_ref, u_ref, w_ref, qk_ref, qg_ref, kd_ref, dl_ref,
                     vn_ref, oi_ref, *, chunk):
    rows, d = q_ref.shape
    nh = GDN_HEADS
    hd = d // nh
    n_chunks = rows // chunk
    heads = range(nh)
    lanes = [slice(hh * hd, (hh + 1) * hd) for hh in heads]

    @pl.when(pl.program_id(1) == 0)
    def _():
        state_ref[...] = jnp.zeros(state_ref.shape, F32)

    row = lax.broadcasted_iota(jnp.int32, (rows, rows), 0)
    col = lax.broadcasted_iota(jnp.int32, (rows, rows), 1)
    same_chunk = jnp.bitwise_xor(row, col) < chunk
    gb = gb_ref[...]
    for hh in heads:
        kh = k_ref[:, lanes[hh]]
        kf = kh.astype(F32)
        qh = q_ref[:, lanes[hh]]
        gc = jnp.broadcast_to(gb[:, hh:hh + 1], (rows, hd))
        beta = jnp.broadcast_to(gb[:, nh + hh:nh + hh + 1], (rows, hd))
        diff = jnp.broadcast_to(gb[:, hh:hh + 1], (rows, rows)) - gt_ref[hh:hh + 1, :]
        causal = same_chunk & (row >= col)
        decay = jnp.where(causal, jnp.exp(jnp.where(causal, diff, 0.0)), 0.0)
        kb = kf * beta
        a = jnp.where(row > col, _nt(kb.astype(BF16), kh) * decay, 0.0)
        inv_ref[hh] = jnp.where(row == col, 1.0, 0.0) - a
        pw_ref[hh] = a.astype(BF16)
        qk_ref[hh] = (_nt(qh, kh) * decay).astype(BF16)
        rhs_ref[hh, :, 0:hd] = (v_ref[:, lanes[hh]].astype(F32) * beta).astype(BF16)
        rhs_ref[hh, :, hd:2 * hd] = (kb * jnp.exp(gc)).astype(BF16)
        qg_ref[hh] = (qh.astype(F32) * jnp.exp(gc)).astype(BF16)
        for c in range(n_chunks):
            rs = slice(c * chunk, (c + 1) * chunk)
            g_last = gc[(c + 1) * chunk - 1:(c + 1) * chunk, :]
            kd_ref[hh, rs, :] = (kf[rs] * jnp.exp(g_last - gc[rs])).astype(BF16)
            dl_ref[hh, c] = jnp.exp(g_last)

    for _ in range(int(math.log2(chunk)) - 1):
        for hh in heads:
            power = _nn(pw_ref[hh], pw_ref[hh]).astype(BF16)
            pw_ref[hh] = power
            inv = inv_ref[hh]
            inv_ref[hh] = inv + _nn(inv.astype(BF16), power)
    for hh in heads:
        sol = _nn(inv_ref[hh].astype(BF16), rhs_ref[hh])
        u_ref[hh] = sol[:, 0:hd]
        w_ref[hh] = sol[:, hd:2 * hd].astype(BF16)

    for c in range(n_chunks):
        rs = slice(c * chunk, (c + 1) * chunk)
        states = [state_ref[hh].astype(BF16) for hh in heads]
        v_new = [(u_ref[hh, rs, :] - _nn(w_ref[hh, rs, :], states[hh])).astype(BF16) for hh in heads]
        for hh in heads:
            vn_ref[hh, rs, :] = v_new[hh]
            state_ref[hh] = state_ref[hh] * dl_ref[hh, c] + _tn(kd_ref[hh, rs, :], v_new[hh])
            oi_ref[hh, rs, :] = _nn(qg_ref[hh, rs, :], states[hh])

    for hh in heads:
        o = oi_ref[hh] + _nn(qk_ref[hh], vn_ref[hh])
        o = o * lax.rsqrt(jnp.mean(o * o, axis=-1, keepdims=True) + 1e-6) * nw_ref[...]
        o_ref[:, lanes[hh]] = (o * _silu(z_ref[:, lanes[hh]].astype(F32))).astype(BF16)


def _gdn_core(q, k, v, z, gb, gt, norm_w, *, batch, seq, rows=256):
    t, d = q.shape
    n = seq // rows
    nh = GDN_HEADS
    hd = d // nh
    row = lambda b, j: (b * n + j, 0)
    return pl.pallas_call(
        functools.partial(_gdn_core_kernel, chunk=GDN_CHUNK),
        grid=(batch, n),
        in_specs=[pl.BlockSpec((rows, d), row)] * 4 + [
            pl.BlockSpec((rows, LANES), row),
            pl.BlockSpec((2 * nh, rows), lambda b, j: (0, b * n + j)),
            pl.BlockSpec((1, hd), lambda b, j: (0, 0)),
        ],
        out_specs=pl.BlockSpec((rows, d), row),
        out_shape=jax.ShapeDtypeStruct((t, d), BF16),
        scratch_shapes=[
            pltpu.VMEM((nh, hd, hd), F32),
            pltpu.VMEM((nh, rows, rows), F32),
            pltpu.VMEM((nh, rows, rows), BF16),
            pltpu.VMEM((nh, rows, 2 * hd), BF16),
            pltpu.VMEM((nh, rows, hd), F32),
            pltpu.VMEM((nh, rows, hd), BF16),
            pltpu.VMEM((nh, rows, rows), BF16),
            pltpu.VMEM((nh, rows, hd), BF16),
            pltpu.VMEM((nh, rows, hd), BF16),
            pltpu.VMEM((nh, rows // GDN_CHUNK, 1, hd), F32),
            pltpu.VMEM((nh, rows, hd), BF16),
            pltpu.VMEM((nh, rows, hd), F32),
        ],
        compiler_params=_params("arbitrary", "arbitrary"),
        name="gdn_core",
    )(q, k, v, z, gb, gt, norm_w.reshape(1, hd))


def _rope_table_kernel(cos_ref, sin_ref):
    ts, half = cos_ref.shape
    pos = (lax.broadcasted_iota(jnp.int32, (ts, half), 0) + pl.program_id(0) * ts).astype(F32)
    frac = lax.broadcasted_iota(jnp.int32, (ts, half), 1).astype(F32) / (half - 1.0)
    ang = pos * jnp.exp(-frac * math.log(RET_ROPE_BASE))
    cos_ref[...] = jnp.cos(ang)
    sin_ref[...] = jnp.sin(ang)


def _rope_tables(seq, half, ts=256):
    spec = pl.BlockSpec((ts, half), lambda i: (i, 0))
    return pl.pallas_call(
        _rope_table_kernel,
        grid=(seq // ts,),
        out_specs=[spec, spec],
        out_shape=[jax.ShapeDtypeStruct((seq, half), F32)] * 2,
        compiler_params=_params("arbitrary"),
        name="rope_tables",
    )()


def _ret_in_kernel(x_ref, mod_ref, w_ref, cos_ref, sin_ref, q_ref, k_ref, v_ref, gate_ref):
    d = x_ref.shape[1]
    nh = RET_HEADS
    dk = d // nh
    half = dk // 2
    h = _modulate(x_ref[...], mod_ref, 0, 1)
    cos_a = cos_ref[...]
    sin_a = sin_ref[...]
    for s, (out, scale) in enumerate(((q_ref, 1.0), (k_ref, dk ** -0.5))):
        t = _nn(h, w_ref[:, s * d:(s + 1) * d])
        for hh in range(nh):
            t1 = t[:, hh * dk:hh * dk + half]
            t2 = t[:, hh * dk + half:(hh + 1) * dk]
            out[:, hh * dk:hh * dk + half] = ((t1 * cos_a - t2 * sin_a) * scale).astype(BF16)
            out[:, hh * dk + half:(hh + 1) * dk] = ((t1 * sin_a + t2 * cos_a) * scale).astype(BF16)
    v_ref[...] = _nn(h, w_ref[:, 2 * d:4 * d]).astype(BF16)
    gate_ref[...] = _nn(h, w_ref[:, 4 * d:6 * d])


def _ret_in(x, mod, w_in, cos_t, sin_t, *, seq, tm=256):
    t, d = x.shape
    tps = seq // tm
    half = cos_t.shape[1]
    row = lambda i: (i, 0)
    return pl.pallas_call(
        _ret_in_kernel,
        grid=(t // tm,),
        in_specs=[
            pl.BlockSpec((tm, d), row),
            pl.BlockSpec((1, 6, d), lambda i: (i // tps, 0, 0)),
            pl.BlockSpec((d, 6 * d), lambda i: (0, 0)),
            pl.BlockSpec((tm, half), lambda i: (i % tps, 0)),
            pl.BlockSpec((tm, half), lambda i: (i % tps, 0)),
        ],
        out_specs=[pl.BlockSpec((tm, d), row), pl.BlockSpec((tm, d), row),
                   pl.BlockSpec((tm, 2 * d), row), pl.BlockSpec((tm, 2 * d), row)],
        out_shape=[jax.ShapeDtypeStruct((t, d), BF16), jax.ShapeDtypeStruct((t, d), BF16),
                   jax.ShapeDtypeStruct((t, 2 * d), BF16), jax.ShapeDtypeStruct((t, 2 * d), F32)],
        compiler_params=_params("arbitrary"),
        name="ret_in",
    )(x, mod, w_in.astype(BF16), cos_t, sin_t)


def _ret_core_kernel(q_ref, k_ref, v_ref, gate_ref, o_ref, state_ref):
    c, d = q_ref.shape
    nh = RET_HEADS
    dk = d // nh
    dv = v_ref.shape[1] // nh

    @pl.when(pl.program_id(1) == 0)
    def _():
        state_ref[...] = jnp.zeros(state_ref.shape, F32)

    rel = (lax.broadcasted_iota(jnp.int32, (c, c), 0)
           - lax.broadcasted_iota(jnp.int32, (c, c), 1)).astype(F32)
    idx = lax.broadcasted_iota(jnp.int32, (c, 1), 0).astype(F32)
    for hh in range(nh):
        log_gamma = math.log(1.0 - 2.0 ** (-5.0 - hh))
        dmask = jnp.where(rel >= 0, jnp.exp(jnp.maximum(rel, 0.0) * log_gamma), 0.0)
        zeta = jnp.exp((c - 1.0 - idx) * log_gamma)
        xi = jnp.exp((idx + 1.0) * log_gamma)
        qh = q_ref[:, hh * dk:(hh + 1) * dk]
        kh = k_ref[:, hh * dk:(hh + 1) * dk]
        vh = v_ref[:, hh * dv:(hh + 1) * dv]
        scores = _nt(qh, kh) * dmask
        state = state_ref[hh]
        o = _nn(scores.astype(BF16), vh) + _nn(qh, state.astype(BF16)) * xi
        state_ref[hh] = (state * math.exp(c * log_gamma)
                         + _tn((kh.astype(F32) * zeta).astype(BF16), vh))
        mu = jnp.mean(o, axis=-1, keepdims=True)
        oc = o - mu
        var = jnp.mean(oc * oc, axis=-1, keepdims=True)
        o = oc * lax.rsqrt(var + 1e-6)
        o_ref[:, hh * dv:(hh + 1) * dv] = (o * _silu(gate_ref[:, hh * dv:(hh + 1) * dv])).astype(BF16)


def _ret_core(q, k, v, gate, *, batch, seq):
    t, d = q.shape
    c = RET_CHUNK
    n = seq // c
    dk = d // RET_HEADS
    dv = v.shape[1] // RET_HEADS
    row = lambda b, j: (b * n + j, 0)
    return pl.pallas_call(
        _ret_core_kernel,
        grid=(batch, n),
        in_specs=[pl.BlockSpec((c, d), row), pl.BlockSpec((c, d), row),
                  pl.BlockSpec((c, 2 * d), row), pl.BlockSpec((c, 2 * d), row)],
        out_specs=pl.BlockSpec((c, 2 * d), row),
        out_shape=jax.ShapeDtypeStruct((t, 2 * d), BF16),
        scratch_shapes=[pltpu.VMEM((RET_HEADS, dk, dv), F32)],
        compiler_params=_params("arbitrary", "arbitrary"),
        name="ret_core",
    )(q, k, v, gate)


def _gelu(t):
    p = 0.3275911 / math.sqrt(2.0)
    half_coefs = [0.5 * a for a in (0.254829592, -0.284496736, 1.421413741, -1.453152027, 1.061405429)]
    u = 1.0 / (1.0 + p * jnp.abs(t))
    poly = half_coefs[4]
    for coef in half_coefs[3::-1]:
        poly = poly * u + coef
    e = poly * u * jnp.exp2(t * t * (-0.5 * math.log2(math.e)))
    return t * jnp.where(t >= 0.0, 1.0 - e, e)


def _gmlp_kernel(x_ref, mod_ref, win_ref, lng_ref, lnb_ref, ws_ref, bs_ref, wout_ref, g_ref, b_ref,
                 out_ref, *, alpha):
    tm, d = x_ref.shape
    width = win_ref.shape[1] // 2
    c = GMLP_CHUNK
    ng = GMLP_GROUPS
    gw = width // ng
    x = x_ref[...]
    h = _modulate(x, mod_ref, 0, 1)

    v = _gelu(_nn(h, win_ref[:, width:]))
    v = _layer_norm_rows(v, lng_ref[...], lnb_ref[...], LN_EPS).astype(BF16)
    u = _gelu(_nn(h, win_ref[:, :width]))
    row = lax.broadcasted_iota(jnp.int32, (c, c), 0)
    col = lax.broadcasted_iota(jnp.int32, (c, c), 1)
    parts = []
    for n in range(tm // c):
        rs = slice(n * c, (n + 1) * c)
        groups = []
        for gi in range(ng):
            ls = slice(gi * gw, (gi + 1) * gw)
            ws = jnp.where(row >= col, ws_ref[gi], 0.0).astype(BF16)
            vs = _nn(ws, v[rs, ls]) + bs_ref[:, gi:gi + 1]
            groups.append((u[rs, ls] * vs).astype(BF16))
        parts.append(jnp.concatenate(groups, axis=1))
    gated = jnp.concatenate(parts, axis=0)
    y = _nn(gated, wout_ref[...])
    r = alpha * x + (1.0 + mod_ref[0, 2:3, :]) * y
    out_ref[...] = _layer_norm_rows(r, g_ref[...], b_ref[...], LN_EPS)


def _gmlp(x, mod, w_in, ln_g, ln_b, w_s, b_s, w_out, res_g, res_b, *, seq, alpha, tm=256):
    t, d = x.shape
    width = w_out.shape[0]
    ng, c, _ = w_s.shape
    tps = seq // tm
    fixed = lambda i: (0, 0)
    return pl.pallas_call(
        functools.partial(_gmlp_kernel, alpha=alpha),
        grid=(t // tm,),
        in_specs=[
            pl.BlockSpec((tm, d), lambda i: (i, 0)),
            pl.BlockSpec((1, 6, d), lambda i: (i // tps, 0, 0)),
            pl.BlockSpec((d, 2 * width), fixed),
            pl.BlockSpec((1, width), fixed),
            pl.BlockSpec((1, width), fixed),
            pl.BlockSpec((ng, c, c), lambda i: (0, 0, 0)),
            pl.BlockSpec((c, ng), fixed),
            pl.BlockSpec((width, d), fixed),
            pl.BlockSpec((1, d), fixed),
            pl.BlockSpec((1, d), fixed),
        ],
        out_specs=pl.BlockSpec((tm, d), lambda i: (i, 0)),
        out_shape=jax.ShapeDtypeStruct((t, d), F32),
        compiler_params=_params("arbitrary"),
        name="gmlp",
    )(x, mod, w_in.astype(BF16), ln_g.reshape(1, width), ln_b.reshape(1, width), w_s, b_s.T,
      w_out.astype(BF16), res_g.reshape(1, d), res_b.reshape(1, d))


def _sb_in_kernel(x_ref, mod_ref, w_ref, q_ref, k_ref, v_ref, *, q_scale):
    d = x_ref.shape[1]
    h = _modulate(x_ref[...], mod_ref, 0, 1)
    q_ref[...] = (_nn(h, w_ref[:, 0:d]) * q_scale).astype(BF16)
    k_ref[...] = _nn(h, w_ref[:, d:2 * d]).astype(BF16)
    v_ref[...] = _nn(h, w_ref[:, 2 * d:3 * d]).astype(BF16)


def _sb_in(x, mod, w_in, *, seq, tm=512):
    t, d = x.shape
    tps = seq // tm
    row = lambda i: (i, 0)
    return pl.pallas_call(
        functools.partial(_sb_in_kernel, q_scale=(d // SB_HEADS) ** -0.5 * math.log2(math.e)),
        grid=(t // tm,),
        in_specs=[
            pl.BlockSpec((tm, d), row),
            pl.BlockSpec((1, 6, d), lambda i: (i // tps, 0, 0)),
            pl.BlockSpec((d, 3 * d), lambda i: (0, 0)),
        ],
        out_specs=[pl.BlockSpec((tm, d), row)] * 3,
        out_shape=[jax.ShapeDtypeStruct((t, d), BF16)] * 3,
        compiler_params=_params("arbitrary"),
        name="sb_in",
    )(x, mod, w_in.astype(BF16))


def _sb_core_kernel(q_ref, k_ref, v_ref, o_ref, qh_ref, sp_ref, zc_ref, carry_ref, acc_ref, knorm_ref,
                    bound_ref):
    tq, width = q_ref.shape
    seq = k_ref.shape[0]
    n_heads = qh_ref.shape[0]
    dh = LANES // 2
    qi = pl.program_id(2)
    lane = lax.broadcasted_iota(jnp.int32, (tq, LANES), 1)
    half_lanes = (lane < dh, lane >= dh)
    group = [slice((hh // 2) * LANES, (hh // 2 + 1) * LANES) for hh in range(n_heads)]
    row = lax.broadcasted_iota(jnp.int32, (tq, tq), 0)
    col = lax.broadcasted_iota(jnp.int32, (tq, tq), 1)
    ones_lower = jnp.where(row >= col, 1.0, 0.0).astype(BF16)

    @pl.when(qi == 0)
    def _():
        def key_tile(t, best):
            kf = k_ref[pl.ds(pl.multiple_of(t * tq, tq), tq), :].astype(F32)
            return jnp.maximum(best, jnp.max(kf * kf, axis=0, keepdims=True))

        col_max = lax.fori_loop(0, seq // tq, key_tile, jnp.zeros((1, width), F32))
        for hh in range(n_heads):
            bound_sq = jnp.sum(jnp.where(half_lanes[hh % 2][0:1, :], col_max[:, group[hh]], 0.0),
                               axis=-1, keepdims=True)
            knorm_ref[hh] = jnp.broadcast_to(bound_sq, knorm_ref.shape[1:])

    for hh in range(n_heads):
        q = q_ref[:, group[hh]]
        qf = q.astype(F32)
        qh_ref[hh] = jnp.where(half_lanes[hh % 2], q, jnp.zeros_like(q))
        qsq = jnp.sum(jnp.where(half_lanes[hh % 2], qf * qf, 0.0), axis=-1, keepdims=True)
        bound_ref[hh] = jnp.sqrt(qsq * knorm_ref[hh, 0:1, 0:1]) * NORM_BOUND_SLACK
    acc_ref[...] = jnp.zeros(acc_ref.shape, F32)
    carry_ref[...] = jnp.zeros(carry_ref.shape, F32)

    def stage(kt, slot, diagonal=False):
        rows = pl.ds(pl.multiple_of(kt * tq, tq), tq)
        for hh in range(n_heads):
            z = _nt(qh_ref[hh], k_ref[rows, group[hh]])
            sp = jnp.maximum(z, 0.0) + jnp.log2(1.0 + jnp.exp2(-jnp.abs(z)))
            carry = carry_ref[hh]
            zc = z - carry
            if diagonal:
                sp = jnp.where(col < row, sp, 0.0)
                zc = jnp.where(col < row, zc, -1e30)
            sp_ref[slot, hh] = sp.astype(BF16)
            zc_ref[slot, hh] = zc
            carry_ref[hh] = carry + jnp.sum(sp, axis=-1, keepdims=True)

    def consume(kt, slot):
        rows = pl.ds(pl.multiple_of(kt * tq, tq), tq)
        for hh in range(n_heads):
            inclusive = _nn(sp_ref[slot, hh], ones_lower)
            a = jnp.exp2(zc_ref[slot, hh] - inclusive)
            acc_ref[hh] += _nn(a.astype(BF16), v_ref[rows, group[hh]])

    def still_alive():
        slack = bound_ref[0] - carry_ref[0]
        for hh in range(1, n_heads):
            slack = jnp.maximum(slack, bound_ref[hh] - carry_ref[hh])
        return (jnp.max(slack) > -DEAD_EXPONENT).astype(jnp.int32)

    stage(qi, 0, diagonal=True)

    @pl.when(qi == 0)
    def _():
        consume(0, 0)

    @pl.when(qi > 0)
    def _():
        stage(qi - 1, 1)
        alive = still_alive()
        consume(qi, 0)
        n_pairs = (qi - 1) // 2

        def more(state):
            return (state[0] < n_pairs) & (state[1] > 0)

        def pair(state):
            kt = qi - 1 - 2 * state[0]
            stage(kt - 1, 0)
            consume(kt, 1)
            stage(kt - 2, 1)
            alive = still_alive()
            consume(kt - 1, 0)
            return state[0] + 1, alive

        pairs_done, alive = lax.while_loop(more, pair, (jnp.int32(0), alive))
        staged = qi - 1 - 2 * pairs_done

        @pl.when((alive == 0) | (staged == 0))
        def _():
            consume(staged, 1)

        @pl.when((alive > 0) & (staged == 1))
        def _():
            stage(0, 0)
            consume(1, 1)
            consume(0, 0)

    for hh in range(0, n_heads, 2):
        o_ref[:, group[hh]] = jnp.where(half_lanes[0], acc_ref[hh], acc_ref[hh + 1]).astype(BF16)


def _sb_core(q, k, v, *, batch, seq, tq=256, heads_per_step=8):
    t, d = q.shape
    tq = min(tq, seq)
    nq = seq // tq
    nh = heads_per_step
    width = nh // 2 * LANES
    return pl.pallas_call(
        _sb_core_kernel,
        grid=(batch, d // width, nq),
        in_specs=[
            pl.BlockSpec((tq, width), lambda b, g, i: (b * nq + i, g)),
            pl.BlockSpec((seq, width), lambda b, g, i: (b, g)),
            pl.BlockSpec((seq, width), lambda b, g, i: (b, g)),
        ],
        out_specs=pl.BlockSpec((tq, width), lambda b, g, i: (b * nq + i, g)),
        out_shape=jax.ShapeDtypeStruct((t, d), BF16),
        scratch_shapes=[
            pltpu.VMEM((nh, tq, LANES), BF16),
            pltpu.VMEM((2, nh, tq, tq), BF16),
            pltpu.VMEM((2, nh, tq, tq), F32),
            pltpu.VMEM((nh, tq, 1), F32),
            pltpu.VMEM((nh, tq, LANES), F32),
            pltpu.VMEM((nh, SUBLANES, LANES), F32),
            pltpu.VMEM((nh, tq, 1), F32),
        ],
        compiler_params=_params("arbitrary", "arbitrary", "arbitrary"),
        name="sb_core",
    )(q, k, v)


def kernel(x, c, cond_w, cond_b, ada_w, ada_b, ln_g, ln_b, ffn_up, ffn_conv_w, ffn_conv_b, ffn_down,
           gdn_w_in, gdn_conv_w, gdn_a_log, gdn_dt_bias, gdn_norm_w, gdn_w_out,
           ret_w_in, ret_w_out,
           gmlp_w_in, gmlp_ln_g, gmlp_ln_b, gmlp_w_s, gmlp_b_s, gmlp_w_out,
           sb_w_in, sb_w_out):
    batch, seq, d = x.shape
    depth = ada_w.shape[0]
    alpha = (2.0 * depth) ** 0.25
    mods = _conditioning(c, cond_w, cond_b, ada_w, ada_b).reshape(depth, batch, 6, d)
    xt = x.reshape(batch * seq, d)
    ffn_up_b = ffn_up.astype(BF16)
    ffn_down_b = ffn_down.astype(BF16)

    for i in range(depth):
        mod = mods[i]
        res = dict(seq=seq, gate_row=2, alpha=alpha)
        mixer = i % 4
        if mixer == 0:
            q, k, v, z, gb, gt = _gdn_in(xt, mod, gdn_w_in, gdn_conv_w, gdn_a_log, gdn_dt_bias, seq=seq)
            o = _gdn_core(q, k, v, z, gb, gt, gdn_norm_w, batch=batch, seq=seq)
            xt = _out_proj_ln(o, gdn_w_out, xt, mod, ln_g[i, 0], ln_b[i, 0], **res)
        elif mixer == 1:
            cos_t, sin_t = _rope_tables(seq, d // RET_HEADS // 2)
            q, k, v, gate = _ret_in(xt, mod, ret_w_in, cos_t, sin_t, seq=seq)
            o = _ret_core(q, k, v, gate, batch=batch, seq=seq)
            xt = _out_proj_ln(o, ret_w_out, xt, mod, ln_g[i, 0], ln_b[i, 0], **res)
        elif mixer == 2:
            xt = _gmlp(xt, mod, gmlp_w_in, gmlp_ln_g, gmlp_ln_b, gmlp_w_s, gmlp_b_s, gmlp_w_out,
                       ln_g[i, 0], ln_b[i, 0], seq=seq, alpha=alpha)
        else:
            q, k, v = _sb_in(xt, mod, sb_w_in, seq=seq)
            o = _sb_core(q, k, v, batch=batch, seq=seq)
            xt = _out_proj_ln(o, sb_w_out, xt, mod, ln_g[i, 0], ln_b[i, 0], **res)
        xt = _ffn(xt, mod, ffn_up_b, ffn_conv_w[i], ffn_conv_b[i], ffn_down_b,
                  ln_g[i, 1], ln_b[i, 1], layer=i, seq=seq, alpha=alpha)
    return xt.reshape(batch, seq, d)
```

```python
import functools
import math

import jax
import jax.numpy as jnp
from jax import lax
from jax.experimental import pallas as pl
from jax.experimental.pallas import tpu as pltpu

F32 = jnp.float32
BF16 = jnp.bfloat16

LANES = 128
SUBLANES = 8
VMEM_LIMIT = 56 * 1024 * 1024

LN_EPS = 1e-5
GDN_HEADS = 8
GDN_CHUNK = 64
GDN_CONV = 4
RET_HEADS = 4
RET_CHUNK = 128
RET_ROPE_BASE = 10000.0
GMLP_CHUNK = 128
GMLP_GROUPS = 8
SB_HEADS = 16
FFN_CONV = 3
DEAD_EXPONENT = 160.0
NORM_BOUND_SLACK = 1.001
EXP2_CLAMP = 64.0


def _params(*sem):
    return pltpu.CompilerParams(dimension_semantics=sem, vmem_limit_bytes=VMEM_LIMIT)


def _nn(a, b):
    return jnp.dot(a, b, preferred_element_type=F32)


def _nt(a, b):
    return lax.dot_general(a, b, (((1,), (1,)), ((), ())), preferred_element_type=F32)


def _tn(a, b):
    return lax.dot_general(a, b, (((0,), (0,)), ((), ())), preferred_element_type=F32)


def _sigmoid(x):
    return 1.0 / (1.0 + jnp.exp(-x))


def _silu(x):
    return x * _sigmoid(x)


def _softplus(x):
    return jnp.maximum(x, 0.0) + jnp.log(1.0 + jnp.exp(-jnp.abs(x)))


def _split3(x):
    hi = x.astype(BF16)
    r = x - hi.astype(F32)
    mid = r.astype(BF16)
    lo = (r - mid.astype(F32)).astype(BF16)
    return hi, mid, lo


def _modulate(x, mod_ref, shift_row, scale_row):
    return (x * (1.0 + mod_ref[0, scale_row:scale_row + 1, :])
            + mod_ref[0, shift_row:shift_row + 1, :]).astype(BF16)


def _layer_norm_rows(r, g, b, eps):
    mu = jnp.mean(r, axis=-1, keepdims=True)
    rc = r - mu
    var = jnp.mean(rc * rc, axis=-1, keepdims=True)
    return rc * lax.rsqrt(var + eps) * g + b


def _lane_replicated_columns(rows):
    n, k = rows.shape
    padded = jnp.concatenate([rows, jnp.zeros((LANES - n, k), F32)], axis=0)
    t = padded.T
    return [jnp.broadcast_to(t[:, b:b + 1], (k, LANES)) for b in range(n)]


def _rowvec_matmul(col, w):
    n = w.shape[1]
    parts = [jnp.sum(col * w[:, c:c + LANES], axis=0, keepdims=True) for c in range(0, n, LANES)]
    return jnp.concatenate(parts, axis=1)


def _cond_kernel(c_ref, cw_ref, cb_ref, aw_ref, ab_ref, o_ref, ecol_ref):
    nb = c_ref.shape[0]

    @pl.when((pl.program_id(0) == 0) & (pl.program_id(1) == 0))
    def _():
        ccols = _lane_replicated_columns(c_ref[...])
        cw = cw_ref[...]
        e = jnp.concatenate([_rowvec_matmul(col, cw) for col in ccols], axis=0) + cb_ref[...]
        ecols = _lane_replicated_columns(_silu(e))
        for b in range(nb):
            ecol_ref[b] = ecols[b]

    w = aw_ref[0]
    for b in range(nb):
        o_ref[0, b:b + 1, :] = _rowvec_matmul(ecol_ref[b], w) + ab_ref[0]


def _conditioning(c, cond_w, cond_b, ada_w, ada_b):
    nb, d = c.shape
    depth, _, n6 = ada_w.shape
    tn = 1536
    return pl.pallas_call(
        _cond_kernel,
        grid=(depth, n6 // tn),
        in_specs=[
            pl.BlockSpec((nb, d), lambda i, j: (0, 0)),
            pl.BlockSpec((d, d), lambda i, j: (0, 0)),
            pl.BlockSpec((1, d), lambda i, j: (0, 0)),
            pl.BlockSpec((1, d, tn), lambda i, j: (i, 0, j)),
            pl.BlockSpec((1, 1, tn), lambda i, j: (i, 0, j)),
        ],
        out_specs=pl.BlockSpec((1, nb, tn), lambda i, j: (i, 0, j)),
        out_shape=jax.ShapeDtypeStruct((depth, nb, n6), F32),
        scratch_shapes=[pltpu.VMEM((nb, d, LANES), F32)],
        compiler_params=_params("arbitrary", "arbitrary"),
        name="conditioning",
    )(c, cond_w, cond_b.reshape(1, d), ada_w, ada_b.reshape(depth, 1, n6))


def _out_ln_kernel(o_ref, w_ref, x_ref, mod_ref, g_ref, b_ref, out_ref, *, gate_row, alpha):
    y = _nn(o_ref[...], w_ref[...])
    r = alpha * x_ref[...] + (1.0 + mod_ref[0, gate_row:gate_row + 1, :]) * y
    out_ref[...] = _layer_norm_rows(r, g_ref[...], b_ref[...], LN_EPS)


def _out_proj_ln(o, w_out, x, mod, ln_g, ln_b, *, seq, gate_row, alpha, tm=512):
    t, kdim = o.shape
    d = x.shape[1]
    tps = seq // tm
    return pl.pallas_call(
        functools.partial(_out_ln_kernel, gate_row=gate_row, alpha=alpha),
        grid=(t // tm,),
        in_specs=[
            pl.BlockSpec((tm, kdim), lambda i: (i, 0)),
            pl.BlockSpec((kdim, d), lambda i: (0, 0)),
            pl.BlockSpec((tm, d), lambda i: (i, 0)),
            pl.BlockSpec((1, 6, d), lambda i: (i // tps, 0, 0)),
            pl.BlockSpec((1, d), lambda i: (0, 0)),
            pl.BlockSpec((1, d), lambda i: (0, 0)),
        ],
        out_specs=pl.BlockSpec((tm, d), lambda i: (i, 0)),
        out_shape=jax.ShapeDtypeStruct((t, d), F32),
        compiler_params=_params("arbitrary"),
        name="out_proj_ln",
    )(o, w_out.astype(BF16), x, mod, ln_g.reshape(1, d), ln_b.reshape(1, d))


def _ffn_kernel(x_ref, mod_ref, wup_ref, cw_ref, cb_ref, wd_ref, g_ref, b_ref, out_ref,
                h_ref, buf_ref, act_ref, *, tiles_per_seq, alpha):
    i = pl.program_id(0)
    tm = x_ref.shape[0]
    f = wd_ref.shape[0]
    n_chunks, _, tf = buf_ref.shape
    x = x_ref[...]
    h_ref[...] = _modulate(x, mod_ref, 3, 4)

    @pl.when(i % tiles_per_seq == 0)
    def _():
        for c in range(n_chunks):
            buf_ref[c, 0:SUBLANES, :] = jnp.zeros((SUBLANES, tf), F32)

    for c in range(n_chunks):
        cols = slice(c * tf, (c + 1) * tf)
        buf_ref[c, SUBLANES:SUBLANES + tm, :] = _nn(h_ref[...], wup_ref[:, cols])
        conv = cb_ref[:, cols]
        for tap in range(FFN_CONV):
            off = SUBLANES - (FFN_CONV - 1) + tap
            conv = conv + cw_ref[tap:tap + 1, cols] * buf_ref[c, off:off + tm, :]
        buf_ref[c, 0:SUBLANES, :] = buf_ref[c, tm:tm + SUBLANES, :]
        up = _nn(h_ref[...], wup_ref[:, f + c * tf:f + (c + 1) * tf])
        act_ref[:, cols] = (_silu(conv) * up).astype(BF16)

    half = tm // 2
    for rs in (slice(0, half), slice(half, tm)):
        y = _nn(act_ref[rs, :], wd_ref[...])
        r = alpha * x_ref[rs, :] + (1.0 + mod_ref[0, 5:6, :]) * y
        out_ref[rs, :] = _layer_norm_rows(r, g_ref[...], b_ref[...], LN_EPS)


def _resident(shape, layer=None):
    if layer is None:
        return pl.BlockSpec(shape, lambda *_: (0,) * len(shape), pipeline_mode=pl.Buffered(1))
    return pl.BlockSpec((None,) + tuple(shape), lambda *_: (layer,) + (0,) * len(shape),
                        pipeline_mode=pl.Buffered(1))


def _ffn(x, mod, w_up, conv_w, conv_b, w_down, ln_g, ln_b, *, layer, seq, alpha, tm=512, n_chunks=2):
    t, d = x.shape
    f = w_down.shape[1]
    tf = f // n_chunks
    tps = seq // tm
    return pl.pallas_call(
        functools.partial(_ffn_kernel, tiles_per_seq=tps, alpha=alpha),
        grid=(t // tm,),
        in_specs=[
            pl.BlockSpec((tm, d), lambda i: (i, 0)),
            pl.BlockSpec((1, 6, d), lambda i: (i // tps, 0, 0)),
            _resident((d, 2 * f), layer),
            _resident((FFN_CONV, f)),
            _resident((1, f)),
            _resident((f, d), layer),
            _resident((1, d)),
            _resident((1, d)),
        ],
        out_specs=pl.BlockSpec((tm, d), lambda i: (i, 0)),
        out_shape=jax.ShapeDtypeStruct((t, d), F32),
        scratch_shapes=[
            pltpu.VMEM((tm, d), BF16),
            pltpu.VMEM((n_chunks, tm + SUBLANES, tf), F32),
            pltpu.VMEM((tm, f), BF16),
        ],
        compiler_params=_params("arbitrary"),
        name="ffn",
    )(x, mod, w_up, conv_w, conv_b.reshape(1, f), w_down, ln_g.reshape(1, d), ln_b.reshape(1, d))


def _gdn_in_kernel(x_ref, mod_ref, wqkv_ref, wz_ref, wab_ref, cw_ref, alog_ref, dtb_ref,
                   q_ref, k_ref, v_ref, z_ref, gb_ref, gt_ref, buf_ref, *, tiles_per_seq, chunk):
    i = pl.program_id(0)
    tm, d = x_ref.shape
    nh = GDN_HEADS
    hd = d // nh
    h = _modulate(x_ref[...], mod_ref, 0, 1)

    @pl.when(i % tiles_per_seq == 0)
    def _():
        buf_ref[0:SUBLANES, :] = jnp.zeros((SUBLANES, buf_ref.shape[1]), F32)

    outs = (q_ref, k_ref, v_ref)
    for s in range(3):
        cs = slice(s * d, (s + 1) * d)
        buf_ref[SUBLANES:SUBLANES + tm, cs] = _nn(h, wqkv_ref[:, cs])
        y = None
        for tap in range(GDN_CONV):
            off = SUBLANES - (GDN_CONV - 1) + tap
            term = cw_ref[tap:tap + 1, cs] * buf_ref[off:off + tm, cs]
            y = term if y is None else y + term
        y = _silu(y)
        if s < 2:
            scale = hd ** -0.5 if s == 0 else 1.0
            for hh in range(nh):
                seg = y[:, hh * hd:(hh + 1) * hd]
                inv = lax.rsqrt(jnp.sum(seg * seg, axis=-1, keepdims=True) + 1e-6)
                outs[s][:, hh * hd:(hh + 1) * hd] = (seg * (inv * scale)).astype(BF16)
        else:
            outs[s][...] = y.astype(BF16)
    buf_ref[0:SUBLANES, :] = buf_ref[tm:tm + SUBLANES, :]

    z_ref[...] = _nn(h, wz_ref[...]).astype(BF16)

    pab = _nn(h, wab_ref[...])
    lane = lax.broadcasted_iota(jnp.int32, pab.shape, 1)
    g = -jnp.exp(alog_ref[...]) * _softplus(pab + dtb_ref[...])
    gb = jnp.where(lane < nh, g, jnp.where(lane < 2 * nh, _sigmoid(pab), 0.0))
    row = lax.broadcasted_iota(jnp.int32, (tm, tm), 0)
    col = lax.broadcasted_iota(jnp.int32, (tm, tm), 1)
    tri = jnp.where(col <= row, jnp.where(jnp.bitwise_xor(row, col) < chunk, 1.0, 0.0), 0.0).astype(BF16)
    hi, mid, lo = _split3(gb)
    cum = _nn(tri, hi) + _nn(tri, mid) + _nn(tri, lo)
    gb = jnp.where(lane < nh, cum, gb)
    gb_ref[...] = gb
    gt_ref[...] = gb.T[0:2 * nh, :]


def _gdn_in(x, mod, w_in, conv_w, a_log, dt_bias, *, seq, tm=256):
    t, d = x.shape
    nh = GDN_HEADS
    tps = seq // tm
    w_b = w_in.astype(BF16)
    w_ab = jnp.pad(w_b[:, 4 * d:], ((0, 0), (0, LANES - 2 * nh)))
    once = pl.Buffered(1)
    alog = jnp.pad(a_log, (0, LANES - nh)).reshape(1, LANES)
    dtb = jnp.pad(dt_bias, (0, LANES - nh)).reshape(1, LANES)
    row = lambda i: (i, 0)
    fixed = lambda i: (0, 0)
    return pl.pallas_call(
        functools.partial(_gdn_in_kernel, tiles_per_seq=tps, chunk=GDN_CHUNK),
        grid=(t // tm,),
        in_specs=[
            pl.BlockSpec((tm, d), row),
            pl.BlockSpec((1, 6, d), lambda i: (i // tps, 0, 0)),
            pl.BlockSpec((d, 3 * d), fixed, pipeline_mode=once),
            pl.BlockSpec((d, d), lambda i: (0, 3), pipeline_mode=once),
            pl.BlockSpec((d, LANES), fixed, pipeline_mode=once),
            pl.BlockSpec((GDN_CONV, 3 * d), fixed, pipeline_mode=once),
            pl.BlockSpec((1, LANES), fixed),
            pl.BlockSpec((1, LANES), fixed),
        ],
        out_specs=[pl.BlockSpec((tm, d), row)] * 4 + [pl.BlockSpec((tm, LANES), row),
                                                      pl.BlockSpec((2 * nh, tm), lambda i: (0, i))],
        out_shape=[jax.ShapeDtypeStruct((t, d), BF16)] * 4 + [jax.ShapeDtypeStruct((t, LANES), F32),
                                                              jax.ShapeDtypeStruct((2 * nh, t), F32)],
        scratch_shapes=[pltpu.VMEM((tm + SUBLANES, 3 * d), F32)],
        compiler_params=_params("arbitrary"),
        name="gdn_in",
    )(x, mod, w_b, w_b, w_ab, conv_w, alog, dtb)


def _gdn_core_kernel(q_ref, k_ref, v_ref, z_ref, gb_ref, gt_ref, nw_ref, o_ref,
                     state_ref, inv_ref, pw_ref, rhs_ref, u_ref, w_ref, qk_ref, qg_ref, kd_ref, dl_ref,
                     vn_ref, oi_ref, *, chunk):
    rows, d = q_ref.shape
    nh = GDN_HEADS
    hd = d // nh
    n_chunks = rows // chunk
    heads = range(nh)
    lanes = [slice(hh * hd, (hh + 1) * hd) for hh in heads]

    @pl.when(pl.program_id(1) == 0)
    def _():
        state_ref[...] = jnp.zeros(state_ref.shape, F32)

    row = lax.broadcasted_iota(jnp.int32, (rows, rows), 0)
    col = lax.broadcasted_iota(jnp.int32, (rows, rows), 1)
    same_chunk = jnp.bitwise_xor(row, col) < chunk
    gb = gb_ref[...]
    for hh in heads:
        kh = k_ref[:, lanes[hh]]
        kf = kh.astype(F32)
        qh = q_ref[:, lanes[hh]]
        gc = jnp.broadcast_to(gb[:, hh:hh + 1], (rows, hd))
        beta = jnp.broadcast_to(gb[:, nh + hh:nh + hh + 1], (rows, hd))
        diff = jnp.broadcast_to(gb[:, hh:hh + 1], (rows, rows)) - gt_ref[hh:hh + 1, :]
        causal = same_chunk & (row >= col)
        decay = jnp.where(causal, jnp.exp(jnp.where(causal, diff, 0.0)), 0.0)
        kb = kf * beta
        a = jnp.where(row > col, _nt(kb.astype(BF16), kh) * decay, 0.0)
        inv_ref[hh] = jnp.where(row == col, 1.0, 0.0) - a
        pw_ref[hh] = a.astype(BF16)
        qk_ref[hh] = (_nt(qh, kh) * decay).astype(BF16)
        rhs_ref[hh, :, 0:hd] = (v_ref[:, lanes[hh]].astype(F32) * beta).astype(BF16)
        rhs_ref[hh, :, hd:2 * hd] = (kb * jnp.exp(gc)).astype(BF16)
        qg_ref[hh] = (qh.astype(F32) * jnp.exp(gc)).astype(BF16)
        for c in range(n_chunks):
            rs = slice(c * chunk, (c + 1) * chunk)
            g_last = gc[(c + 1) * chunk - 1:(c + 1) * chunk, :]
            kd_ref[hh, rs, :] = (kf[rs] * jnp.exp(g_last - gc[rs])).astype(BF16)
            dl_ref[hh, c] = jnp.exp(g_last)

    for _ in range(int(math.log2(chunk)) - 1):
        for hh in heads:
            power = _nn(pw_ref[hh], pw_ref[hh]).astype(BF16)
            pw_ref[hh] = power
            inv = inv_ref[hh]
            inv_ref[hh] = inv + _nn(inv.astype(BF16), power)
    for hh in heads:
        sol = _nn(inv_ref[hh].astype(BF16), rhs_ref[hh])
        u_ref[hh] = sol[:, 0:hd]
        w_ref[hh] = sol[:, hd:2 * hd].astype(BF16)

    for c in range(n_chunks):
        rs = slice(c * chunk, (c + 1) * chunk)
        states = [state_ref[hh].astype(BF16) for hh in heads]
        v_new = [(u_ref[hh, rs, :] - _nn(w_ref[hh, rs, :], states[hh])).astype(BF16) for hh in heads]
        for hh in heads:
            vn_ref[hh, rs, :] = v_new[hh]
            state_ref[hh] = state_ref[hh] * dl_ref[hh, c] + _tn(kd_ref[hh, rs, :], v_new[hh])
            oi_ref[hh, rs, :] = _nn(qg_ref[hh, rs, :], states[hh])

    for hh in heads:
        o = oi_ref[hh] + _nn(qk_ref[hh], vn_ref[hh])
        o = o * lax.rsqrt(jnp.mean(o * o, axis=-1, keepdims=True) + 1e-6) * nw_ref[...]
        o_ref[:, lanes[hh]] = (o * _silu(z_ref[:, lanes[hh]].astype(F32))).astype(BF16)


def _gdn_core(q, k, v, z, gb, gt, norm_w, *, batch, seq, rows=256):
    t, d = q.shape
    n = seq // rows
    nh = GDN_HEADS
    hd = d // nh
    row = lambda b, j: (b * n + j, 0)
    return pl.pallas_call(
        functools.partial(_gdn_core_kernel, chunk=GDN_CHUNK),
        grid=(batch, n),
        in_specs=[pl.BlockSpec((rows, d), row)] * 4 + [
            pl.BlockSpec((rows, LANES), row),
            pl.BlockSpec((2 * nh, rows), lambda b, j: (0, b * n + j)),
            pl.BlockSpec((1, hd), lambda b, j: (0, 0)),
        ],
        out_specs=pl.BlockSpec((rows, d), row),
        out_shape=jax.ShapeDtypeStruct((t, d), BF16),
        scratch_shapes=[
            pltpu.VMEM((nh, hd, hd), F32),
            pltpu.VMEM((nh, rows, rows), F32),
            pltpu.VMEM((nh, rows, rows), BF16),
            pltpu.VMEM((nh, rows, 2 * hd), BF16),
            pltpu.VMEM((nh, rows, hd), F32),
            pltpu.VMEM((nh, rows, hd), BF16),
            pltpu.VMEM((nh, rows, rows), BF16),
            pltpu.VMEM((nh, rows, hd), BF16),
            pltpu.VMEM((nh, rows, hd), BF16),
            pltpu.VMEM((nh, rows // GDN_CHUNK, 1, hd), F32),
            pltpu.VMEM((nh, rows, hd), BF16),
            pltpu.VMEM((nh, rows, hd), F32),
        ],
        compiler_params=_params("arbitrary", "arbitrary"),
        name="gdn_core",
    )(q, k, v, z, gb, gt, norm_w.reshape(1, hd))


def _rope_table_kernel(cos_ref, sin_ref):
    ts, half = cos_ref.shape
    pos = (lax.broadcasted_iota(jnp.int32, (ts, half), 0) + pl.program_id(0) * ts).astype(F32)
    frac = lax.broadcasted_iota(jnp.int32, (ts, half), 1).astype(F32) / (half - 1.0)
    ang = pos * jnp.exp(-frac * math.log(RET_ROPE_BASE))
    cos_ref[...] = jnp.cos(ang)
    sin_ref[...] = jnp.sin(ang)


def _rope_tables(seq, half, ts=256):
    spec = pl.BlockSpec((ts, half), lambda i: (i, 0))
    return pl.pallas_call(
        _rope_table_kernel,
        grid=(seq // ts,),
        out_specs=[spec, spec],
        out_shape=[jax.ShapeDtypeStruct((seq, half), F32)] * 2,
        compiler_params=_params("arbitrary"),
        name="rope_tables",
    )()


def _ret_in_kernel(x_ref, mod_ref, w_ref, cos_ref, sin_ref, q_ref, k_ref, v_ref, gate_ref):
    d = x_ref.shape[1]
    nh = RET_HEADS
    dk = d // nh
    half = dk // 2
    h = _modulate(x_ref[...], mod_ref, 0, 1)
    cos_a = cos_ref[...]
    sin_a = sin_ref[...]
    for s, (out, scale) in enumerate(((q_ref, 1.0), (k_ref, dk ** -0.5))):
        t = _nn(h, w_ref[:, s * d:(s + 1) * d])
        for hh in range(nh):
            t1 = t[:, hh * dk:hh * dk + half]
            t2 = t[:, hh * dk + half:(hh + 1) * dk]
            out[:, hh * dk:hh * dk + half] = ((t1 * cos_a - t2 * sin_a) * scale).astype(BF16)
            out[:, hh * dk + half:(hh + 1) * dk] = ((t1 * sin_a + t2 * cos_a) * scale).astype(BF16)
    v_ref[...] = _nn(h, w_ref[:, 2 * d:4 * d]).astype(BF16)
    gate_ref[...] = _nn(h, w_ref[:, 4 * d:6 * d])


def _ret_in(x, mod, w_in, cos_t, sin_t, *, seq, tm=256):
    t, d = x.shape
    tps = seq // tm
    half = cos_t.shape[1]
    row = lambda i: (i, 0)
    return pl.pallas_call(
        _ret_in_kernel,
        grid=(t // tm,),
        in_specs=[
            pl.BlockSpec((tm, d), row),
            pl.BlockSpec((1, 6, d), lambda i: (i // tps, 0, 0)),
            pl.BlockSpec((d, 6 * d), lambda i: (0, 0)),
            pl.BlockSpec((tm, half), lambda i: (i % tps, 0)),
            pl.BlockSpec((tm, half), lambda i: (i % tps, 0)),
        ],
        out_specs=[pl.BlockSpec((tm, d), row), pl.BlockSpec((tm, d), row),
                   pl.BlockSpec((tm, 2 * d), row), pl.BlockSpec((tm, 2 * d), row)],
        out_shape=[jax.ShapeDtypeStruct((t, d), BF16), jax.ShapeDtypeStruct((t, d), BF16),
                   jax.ShapeDtypeStruct((t, 2 * d), BF16), jax.ShapeDtypeStruct((t, 2 * d), F32)],
        compiler_params=_params("arbitrary"),
        name="ret_in",
    )(x, mod, w_in.astype(BF16), cos_t, sin_t)


def _ret_core_kernel(q_ref, k_ref, v_ref, gate_ref, o_ref, state_ref, *, chunk):
    rows, d = q_ref.shape
    c = chunk
    nh = RET_HEADS
    dk = d // nh
    dv = v_ref.shape[1] // nh

    @pl.when(pl.program_id(1) == 0)
    def _():
        state_ref[...] = jnp.zeros(state_ref.shape, F32)

    rel = (lax.broadcasted_iota(jnp.int32, (c, c), 0)
           - lax.broadcasted_iota(jnp.int32, (c, c), 1)).astype(F32)
    idx = lax.broadcasted_iota(jnp.int32, (c, 1), 0).astype(F32)
    for hh in range(nh):
        log_gamma = math.log(1.0 - 2.0 ** (-5.0 - hh))
        dmask = jnp.where(rel >= 0, jnp.exp(jnp.maximum(rel, 0.0) * log_gamma), 0.0)
        zeta = jnp.exp((c - 1.0 - idx) * log_gamma)
        xi = jnp.exp((idx + 1.0) * log_gamma)
        state = state_ref[hh]
        for ci in range(rows // c):
            rs = slice(ci * c, (ci + 1) * c)
            qh = q_ref[rs, hh * dk:(hh + 1) * dk]
            kh = k_ref[rs, hh * dk:(hh + 1) * dk]
            vh = v_ref[rs, hh * dv:(hh + 1) * dv]
            scores = _nt(qh, kh) * dmask
            o = _nn(scores.astype(BF16), vh) + _nn(qh, state.astype(BF16)) * xi
            state = state * math.exp(c * log_gamma) + _tn((kh.astype(F32) * zeta).astype(BF16), vh)
            mu = jnp.mean(o, axis=-1, keepdims=True)
            oc = o - mu
            var = jnp.mean(oc * oc, axis=-1, keepdims=True)
            o = oc * lax.rsqrt(var + 1e-6)
            gate = gate_ref[rs, hh * dv:(hh + 1) * dv]
            o_ref[rs, hh * dv:(hh + 1) * dv] = (o * _silu(gate)).astype(BF16)
        state_ref[hh] = state


def _ret_core(q, k, v, gate, *, batch, seq, rows=256):
    t, d = q.shape
    c = rows
    n = seq // rows
    dk = d // RET_HEADS
    dv = v.shape[1] // RET_HEADS
    row = lambda b, j: (b * n + j, 0)
    return pl.pallas_call(
        functools.partial(_ret_core_kernel, chunk=RET_CHUNK),
        grid=(batch, n),
        in_specs=[pl.BlockSpec((c, d), row), pl.BlockSpec((c, d), row),
                  pl.BlockSpec((c, 2 * d), row), pl.BlockSpec((c, 2 * d), row)],
        out_specs=pl.BlockSpec((c, 2 * d), row),
        out_shape=jax.ShapeDtypeStruct((t, 2 * d), BF16),
        scratch_shapes=[pltpu.VMEM((RET_HEADS, dk, dv), F32)],
        compiler_params=_params("arbitrary", "arbitrary"),
        name="ret_core",
    )(q, k, v, gate)


def _gelu(t):
    p = 0.3275911 / math.sqrt(2.0)
    half_coefs = [0.5 * a for a in (0.254829592, -0.284496736, 1.421413741, -1.453152027, 1.061405429)]
    u = 1.0 / (1.0 + p * jnp.abs(t))
    poly = half_coefs[4]
    for coef in half_coefs[3::-1]:
        poly = poly * u + coef
    e = poly * u * jnp.exp2(t * t * (-0.5 * math.log2(math.e)))
    return t * jnp.where(t >= 0.0, 1.0 - e, e)


def _gmlp_kernel(x_ref, mod_ref, win_ref, lng_ref, lnb_ref, ws_ref, bs_ref, wout_ref, g_ref, b_ref,
                 out_ref, *, alpha):
    tm, d = x_ref.shape
    width = win_ref.shape[1] // 2
    c = GMLP_CHUNK
    ng = GMLP_GROUPS
    gw = width // ng
    x = x_ref[...]
    h = _modulate(x, mod_ref, 0, 1)

    v = _gelu(_nn(h, win_ref[:, width:]))
    v = _layer_norm_rows(v, lng_ref[...], lnb_ref[...], LN_EPS).astype(BF16)
    u = _gelu(_nn(h, win_ref[:, :width]))
    row = lax.broadcasted_iota(jnp.int32, (c, c), 0)
    col = lax.broadcasted_iota(jnp.int32, (c, c), 1)
    parts = []
    for n in range(tm // c):
        rs = slice(n * c, (n + 1) * c)
        groups = []
        for gi in range(ng):
            ls = slice(gi * gw, (gi + 1) * gw)
            ws = jnp.where(row >= col, ws_ref[gi], 0.0).astype(BF16)
            vs = _nn(ws, v[rs, ls]) + bs_ref[:, gi:gi + 1]
            groups.append((u[rs, ls] * vs).astype(BF16))
        parts.append(jnp.concatenate(groups, axis=1))
    gated = jnp.concatenate(parts, axis=0)
    y = _nn(gated, wout_ref[...])
    r = alpha * x + (1.0 + mod_ref[0, 2:3, :]) * y
    out_ref[...] = _layer_norm_rows(r, g_ref[...], b_ref[...], LN_EPS)


def _gmlp(x, mod, w_in, ln_g, ln_b, w_s, b_s, w_out, res_g, res_b, *, seq, alpha, tm=256):
    t, d = x.shape
    width = w_out.shape[0]
    ng, c, _ = w_s.shape
    tps = seq // tm
    fixed = lambda i: (0, 0)
    return pl.pallas_call(
        functools.partial(_gmlp_kernel, alpha=alpha),
        grid=(t // tm,),
        in_specs=[
            pl.BlockSpec((tm, d), lambda i: (i, 0)),
            pl.BlockSpec((1, 6, d), lambda i: (i // tps, 0, 0)),
            pl.BlockSpec((d, 2 * width), fixed),
            pl.BlockSpec((1, width), fixed),
            pl.BlockSpec((1, width), fixed),
            pl.BlockSpec((ng, c, c), lambda i: (0, 0, 0)),
            pl.BlockSpec((c, ng), fixed),
            pl.BlockSpec((width, d), fixed),
            pl.BlockSpec((1, d), fixed),
            pl.BlockSpec((1, d), fixed),
        ],
        out_specs=pl.BlockSpec((tm, d), lambda i: (i, 0)),
        out_shape=jax.ShapeDtypeStruct((t, d), F32),
        compiler_params=_params("arbitrary"),
        name="gmlp",
    )(x, mod, w_in.astype(BF16), ln_g.reshape(1, width), ln_b.reshape(1, width), w_s, b_s.T,
      w_out.astype(BF16), res_g.reshape(1, d), res_b.reshape(1, d))


def _sb_in_kernel(x_ref, mod_ref, w_ref, q_ref, k_ref, v_ref, *, q_scale):
    d = x_ref.shape[1]
    h = _modulate(x_ref[...], mod_ref, 0, 1)
    q_ref[...] = (_nn(h, w_ref[:, 0:d]) * q_scale).astype(BF16)
    k_ref[...] = _nn(h, w_ref[:, d:2 * d]).astype(BF16)
    v_ref[...] = _nn(h, w_ref[:, 2 * d:3 * d]).astype(BF16)


def _sb_in(x, mod, w_in, *, seq, tm=512):
    t, d = x.shape
    tps = seq // tm
    row = lambda i: (i, 0)
    return pl.pallas_call(
        functools.partial(_sb_in_kernel, q_scale=(d // SB_HEADS) ** -0.5 * math.log2(math.e)),
        grid=(t // tm,),
        in_specs=[
            pl.BlockSpec((tm, d), row),
            pl.BlockSpec((1, 6, d), lambda i: (i // tps, 0, 0)),
            pl.BlockSpec((d, 3 * d), lambda i: (0, 0)),
        ],
        out_specs=[pl.BlockSpec((tm, d), row)] * 3,
        out_shape=[jax.ShapeDtypeStruct((t, d), BF16)] * 3,
        compiler_params=_params("arbitrary"),
        name="sb_in",
    )(x, mod, w_in.astype(BF16))


def _sb_core_kernel(q_ref, k_ref, v_ref, o_ref, qh_ref, sp_ref, zc_ref, carry_ref, acc_ref, knorm_ref,
                    bound_ref):
    tq, width = q_ref.shape
    seq = k_ref.shape[0]
    n_heads = qh_ref.shape[0]
    dh = LANES // 2
    qi = pl.program_id(2)
    lane = lax.broadcasted_iota(jnp.int32, (tq, LANES), 1)
    half_lanes = (lane < dh, lane >= dh)
    group = [slice((hh // 2) * LANES, (hh // 2 + 1) * LANES) for hh in range(n_heads)]
    row = lax.broadcasted_iota(jnp.int32, (tq, tq), 0)
    col = lax.broadcasted_iota(jnp.int32, (tq, tq), 1)
    ones_lower = jnp.where(row >= col, 1.0, 0.0).astype(BF16)

    @pl.when(qi == 0)
    def _():
        def key_tile(t, best):
            kf = k_ref[pl.ds(pl.multiple_of(t * tq, tq), tq), :].astype(F32)
            return jnp.maximum(best, jnp.max(kf * kf, axis=0, keepdims=True))

        col_max = lax.fori_loop(0, seq // tq, key_tile, jnp.zeros((1, width), F32))
        for hh in range(n_heads):
            bound_sq = jnp.sum(jnp.where(half_lanes[hh % 2][0:1, :], col_max[:, group[hh]], 0.0),
                               axis=-1, keepdims=True)
            knorm_ref[hh] = jnp.broadcast_to(bound_sq, knorm_ref.shape[1:])

    for hh in range(n_heads):
        q = q_ref[:, group[hh]]
        qf = q.astype(F32)
        qh_ref[hh] = jnp.where(half_lanes[hh % 2], q, jnp.zeros_like(q))
        qsq = jnp.sum(jnp.where(half_lanes[hh % 2], qf * qf, 0.0), axis=-1, keepdims=True)
        bound_ref[hh] = jnp.sqrt(qsq * knorm_ref[hh, 0:1, 0:1]) * NORM_BOUND_SLACK
    acc_ref[...] = jnp.zeros(acc_ref.shape, F32)
    carry_ref[...] = jnp.zeros(carry_ref.shape, F32)

    def stage(kt, slot, diagonal=False):
        rows = pl.ds(pl.multiple_of(kt * tq, tq), tq)
        for hh in range(n_heads):
            z = _nt(qh_ref[hh], k_ref[rows, group[hh]])
            sp = jnp.maximum(z, jnp.log2(1.0 + jnp.exp2(jnp.minimum(z, EXP2_CLAMP))))
            carry = carry_ref[hh]
            zc = z - carry
            if diagonal:
                sp = jnp.where(col < row, sp, 0.0)
                zc = jnp.where(col < row, zc, -1e30)
            sp_ref[slot, hh] = sp.astype(BF16)
            zc_ref[slot, hh] = zc
            carry_ref[hh] = carry + jnp.sum(sp, axis=-1, keepdims=True)

    def consume(kt, slot):
        rows = pl.ds(pl.multiple_of(kt * tq, tq), tq)
        for hh in range(n_heads):
            inclusive = _nn(sp_ref[slot, hh], ones_lower)
            a = jnp.exp2(zc_ref[slot, hh] - inclusive)
            acc_ref[hh] += _nn(a.astype(BF16), v_ref[rows, group[hh]])

    def still_alive():
        slack = bound_ref[0] - carry_ref[0]
        for hh in range(1, n_heads):
            slack = jnp.maximum(slack, bound_ref[hh] - carry_ref[hh])
        return (jnp.max(slack) > -DEAD_EXPONENT).astype(jnp.int32)

    stage(qi, 0, diagonal=True)

    @pl.when(qi == 0)
    def _():
        consume(0, 0)

    @pl.when(qi > 0)
    def _():
        stage(qi - 1, 1)
        alive = still_alive()
        consume(qi, 0)
        n_pairs = (qi - 1) // 2

        def more(state):
            return (state[0] < n_pairs) & (state[1] > 0)

        def pair(state):
            kt = qi - 1 - 2 * state[0]
            stage(kt - 1, 0)
            consume(kt, 1)
            stage(kt - 2, 1)
            alive = still_alive()
            consume(kt - 1, 0)
            return state[0] + 1, alive

        pairs_done, alive = lax.while_loop(more, pair, (jnp.int32(0), alive))
        staged = qi - 1 - 2 * pairs_done

        @pl.when((alive == 0) | (staged == 0))
        def _():
            consume(staged, 1)

        @pl.when((alive > 0) & (staged == 1))
        def _():
            stage(0, 0)
            consume(1, 1)
            consume(0, 0)

    for hh in range(0, n_heads, 2):
        o_ref[:, group[hh]] = jnp.where(half_lanes[0], acc_ref[hh], acc_ref[hh + 1]).astype(BF16)


def _sb_core(q, k, v, *, batch, seq, tq=256, heads_per_step=8):
    t, d = q.shape
    tq = min(tq, seq)
    nq = seq // tq
    nh = heads_per_step
    width = nh // 2 * LANES
    return pl.pallas_call(
        _sb_core_kernel,
        grid=(batch, d // width, nq),
        in_specs=[
            pl.BlockSpec((tq, width), lambda b, g, i: (b * nq + i, g)),
            pl.BlockSpec((seq, width), lambda b, g, i: (b, g)),
            pl.BlockSpec((seq, width), lambda b, g, i: (b, g)),
        ],
        out_specs=pl.BlockSpec((tq, width), lambda b, g, i: (b * nq + i, g)),
        out_shape=jax.ShapeDtypeStruct((t, d), BF16),
        scratch_shapes=[
            pltpu.VMEM((nh, tq, LANES), BF16),
            pltpu.VMEM((2, nh, tq, tq), BF16),
            pltpu.VMEM((2, nh, tq, tq), F32),
            pltpu.VMEM((nh, tq, 1), F32),
            pltpu.VMEM((nh, tq, LANES), F32),
            pltpu.VMEM((nh, SUBLANES, LANES), F32),
            pltpu.VMEM((nh, tq, 1), F32),
        ],
        compiler_params=_params("arbitrary", "arbitrary", "arbitrary"),
        name="sb_core",
    )(q, k, v)


def kernel(x, c, cond_w, cond_b, ada_w, ada_b, ln_g, ln_b, ffn_up, ffn_conv_w, ffn_conv_b, ffn_down,
           gdn_w_in, gdn_conv_w, gdn_a_log, gdn_dt_bias, gdn_norm_w, gdn_w_out,
           ret_w_in, ret_w_out,
           gmlp_w_in, gmlp_ln_g, gmlp_ln_b, gmlp_w_s, gmlp_b_s, gmlp_w_out,
           sb_w_in, sb_w_out):
    batch, seq, d = x.shape
    depth = ada_w.shape[0]
    alpha = (2.0 * depth) ** 0.25
    mods = _conditioning(c, cond_w, cond_b, ada_w, ada_b).reshape(depth, batch, 6, d)
    xt = x.reshape(batch * seq, d)
    ffn_up_b = ffn_up.astype(BF16)
    ffn_down_b = ffn_down.astype(BF16)

    for i in range(depth):
        mod = mods[i]
        res = dict(seq=seq, gate_row=2, alpha=alpha)
        mixer = i % 4
        if mixer == 0:
            q, k, v, z, gb, gt = _gdn_in(xt, mod, gdn_w_in, gdn_conv_w, gdn_a_log, gdn_dt_bias, seq=seq)
            o = _gdn_core(q, k, v, z, gb, gt, gdn_norm_w, batch=batch, seq=seq)
            xt = _out_proj_ln(o, gdn_w_out, xt, mod, ln_g[i, 0], ln_b[i, 0], **res)
        elif mixer == 1:
            cos_t, sin_t = _rope_tables(seq, d // RET_HEADS // 2)
            q, k, v, gate = _ret_in(xt, mod, ret_w_in, cos_t, sin_t, seq=seq)
            o = _ret_core(q, k, v, gate, batch=batch, seq=seq)
            xt = _out_proj_ln(o, ret_w_out, xt, mod, ln_g[i, 0], ln_b[i, 0], **res)
        elif mixer == 2:
            xt = _gmlp(xt, mod, gmlp_w_in, gmlp_ln_g, gmlp_ln_b, gmlp_w_s, gmlp_b_s, gmlp_w_out,
                       ln_g[i, 0], ln_b[i, 0], seq=seq, alpha=alpha)
        else:
            q, k, v = _sb_in(xt, mod, sb_w_in, seq=seq)
            o = _sb_core(q, k, v, batch=batch, seq=seq)
            xt = _out_proj_ln(o, sb_w_out, xt, mod, ln_g[i, 0], ln_b[i, 0], **res)
        xt = _ffn(xt, mod, ffn_up_b, ffn_conv_w[i], ffn_conv_b[i], ffn_down_b,
                  ln_g[i, 1], ln_b[i, 1], layer=i, seq=seq, alpha=alpha)
    return xt.reshape(batch, seq, d)
```

```python
import functools
import math

import jax
import jax.numpy as jnp
from jax import lax
from jax.experimental import pallas as pl
from jax.experimental.pallas import tpu as pltpu

F32 = jnp.float32
BF16 = jnp.bfloat16

LANES = 128
SUBLANES = 8
VMEM_LIMIT = 56 * 1024 * 1024

LN_EPS = 1e-5
GDN_HEADS = 8
GDN_CHUNK = 64
GDN_CONV = 4
RET_HEADS = 4
RET_CHUNK = 128
RET_ROPE_BASE = 10000.0
GMLP_CHUNK = 128
GMLP_GROUPS = 8
SB_HEADS = 16
FFN_CONV = 3
DEAD_EXPONENT = 160.0
NORM_BOUND_SLACK = 1.001
EXP2_CLAMP = 64.0


def _params(*sem):
    return pltpu.CompilerParams(dimension_semantics=sem, vmem_limit_bytes=VMEM_LIMIT)


def _nn(a, b):
    return jnp.dot(a, b, preferred_element_type=F32)


def _nt(a, b):
    return lax.dot_general(a, b, (((1,), (1,)), ((), ())), preferred_element_type=F32)


def _tn(a, b):
    return lax.dot_general(a, b, (((0,), (0,)), ((), ())), preferred_element_type=F32)


def _sigmoid(x):
    return 1.0 / (1.0 + jnp.exp(-x))


def _silu(x):
    return x * _sigmoid(x)


def _softplus(x):
    return jnp.maximum(x, 0.0) + jnp.log(1.0 + jnp.exp(-jnp.abs(x)))


def _split3(x):
    hi = x.astype(BF16)
    r = x - hi.astype(F32)
    mid = r.astype(BF16)
    lo = (r - mid.astype(F32)).astype(BF16)
    return hi, mid, lo


def _modulate(x, mod_ref, shift_row, scale_row):
    return (x * (1.0 + mod_ref[0, scale_row:scale_row + 1, :])
            + mod_ref[0, shift_row:shift_row + 1, :]).astype(BF16)


def _layer_norm_rows(r, g, b, eps):
    mu = jnp.mean(r, axis=-1, keepdims=True)
    rc = r - mu
    var = jnp.mean(rc * rc, axis=-1, keepdims=True)
    return rc * lax.rsqrt(var + eps) * g + b


def _lane_replicated_columns(rows):
    n, k = rows.shape
    padded = jnp.concatenate([rows, jnp.zeros((LANES - n, k), F32)], axis=0)
    t = padded.T
    return [jnp.broadcast_to(t[:, b:b + 1], (k, LANES)) for b in range(n)]


def _rowvec_matmul(col, w):
    n = w.shape[1]
    parts = [jnp.sum(col * w[:, c:c + LANES], axis=0, keepdims=True) for c in range(0, n, LANES)]
    return jnp.concatenate(parts, axis=1)


def _cond_kernel(c_ref, cw_ref, cb_ref, aw_ref, ab_ref, o_ref, ecol_ref):
    nb = c_ref.shape[0]

    @pl.when((pl.program_id(0) == 0) & (pl.program_id(1) == 0))
    def _():
        ccols = _lane_replicated_columns(c_ref[...])
        cw = cw_ref[...]
        e = jnp.concatenate([_rowvec_matmul(col, cw) for col in ccols], axis=0) + cb_ref[...]
        ecols = _lane_replicated_columns(_silu(e))
        for b in range(nb):
            ecol_ref[b] = ecols[b]

    w = aw_ref[0]
    for b in range(nb):
        o_ref[0, b:b + 1, :] = _rowvec_matmul(ecol_ref[b], w) + ab_ref[0]


def _conditioning(c, cond_w, cond_b, ada_w, ada_b):
    nb, d = c.shape
    depth, _, n6 = ada_w.shape
    tn = 1536
    return pl.pallas_call(
        _cond_kernel,
        grid=(depth, n6 // tn),
        in_specs=[
            pl.BlockSpec((nb, d), lambda i, j: (0, 0)),
            pl.BlockSpec((d, d), lambda i, j: (0, 0)),
            pl.BlockSpec((1, d), lambda i, j: (0, 0)),
            pl.BlockSpec((1, d, tn), lambda i, j: (i, 0, j)),
            pl.BlockSpec((1, 1, tn), lambda i, j: (i, 0, j)),
        ],
        out_specs=pl.BlockSpec((1, nb, tn), lambda i, j: (i, 0, j)),
        out_shape=jax.ShapeDtypeStruct((depth, nb, n6), F32),
        scratch_shapes=[pltpu.VMEM((nb, d, LANES), F32)],
        compiler_params=_params("arbitrary", "arbitrary"),
        name="conditioning",
    )(c, cond_w, cond_b.reshape(1, d), ada_w, ada_b.reshape(depth, 1, n6))


def _ffn_kernel(o_ref, wo_ref, g1_ref, b1_ref, x_ref, mod_ref, wup_ref, cw_ref, cb_ref, wd_ref,
                g_ref, b_ref, out_ref, x1_ref, h_ref, buf_ref, act_ref, *, tiles_per_seq, alpha):
    i = pl.program_id(0)
    tm = x_ref.shape[0]
    f = wd_ref.shape[0]
    n_chunks, _, tf = buf_ref.shape
    half = tm // 2
    halves = (slice(0, half), slice(half, tm))
    for rs in halves:
        mixed = alpha * x_ref[rs, :] + (1.0 + mod_ref[0, 2:3, :]) * _nn(o_ref[rs, :], wo_ref[...])
        x1 = _layer_norm_rows(mixed, g1_ref[...], b1_ref[...], LN_EPS)
        x1_ref[rs, :] = x1
        h_ref[rs, :] = _modulate(x1, mod_ref, 3, 4)

    @pl.when(i % tiles_per_seq == 0)
    def _():
        for c in range(n_chunks):
            buf_ref[c, 0:SUBLANES, :] = jnp.zeros((SUBLANES, tf), F32)

    for c in range(n_chunks):
        cols = slice(c * tf, (c + 1) * tf)
        buf_ref[c, SUBLANES:SUBLANES + tm, :] = _nn(h_ref[...], wup_ref[:, cols])
        conv = cb_ref[:, cols]
        for tap in range(FFN_CONV):
            off = SUBLANES - (FFN_CONV - 1) + tap
            conv = conv + cw_ref[tap:tap + 1, cols] * buf_ref[c, off:off + tm, :]
        buf_ref[c, 0:SUBLANES, :] = buf_ref[c, tm:tm + SUBLANES, :]
        up = _nn(h_ref[...], wup_ref[:, f + c * tf:f + (c + 1) * tf])
        act_ref[:, cols] = (_silu(conv) * up).astype(BF16)

    for rs in halves:
        y = _nn(act_ref[rs, :], wd_ref[...])
        r = alpha * x1_ref[rs, :] + (1.0 + mod_ref[0, 5:6, :]) * y
        out_ref[rs, :] = _layer_norm_rows(r, g_ref[...], b_ref[...], LN_EPS)


def _resident(shape, layer=None):
    if layer is None:
        return pl.BlockSpec(shape, lambda *_: (0,) * len(shape), pipeline_mode=pl.Buffered(1))
    return pl.BlockSpec((None,) + tuple(shape), lambda *_: (layer,) + (0,) * len(shape),
                        pipeline_mode=pl.Buffered(1))


def _sublayer_tail(o, w_out, x, mod, w_up, conv_w, conv_b, w_down, ln_g, ln_b, *, layer, seq, alpha,
                   tm=512, n_chunks=2):
    t, d = x.shape
    kdim = o.shape[1]
    f = w_down.shape[1]
    tf = f // n_chunks
    tps = seq // tm
    rows = lambda i: (i, 0)
    return pl.pallas_call(
        functools.partial(_ffn_kernel, tiles_per_seq=tps, alpha=alpha),
        grid=(t // tm,),
        in_specs=[
            pl.BlockSpec((tm, kdim), rows),
            _resident((kdim, d)),
            _resident((1, d)),
            _resident((1, d)),
            pl.BlockSpec((tm, d), rows),
            pl.BlockSpec((1, 6, d), lambda i: (i // tps, 0, 0)),
            _resident((d, 2 * f), layer),
            _resident((FFN_CONV, f)),
            _resident((1, f)),
            _resident((f, d), layer),
            _resident((1, d)),
            _resident((1, d)),
        ],
        out_specs=pl.BlockSpec((tm, d), rows),
        out_shape=jax.ShapeDtypeStruct((t, d), F32),
        scratch_shapes=[
            pltpu.VMEM((tm, d), F32),
            pltpu.VMEM((tm, d), BF16),
            pltpu.VMEM((n_chunks, tm + SUBLANES, tf), F32),
            pltpu.VMEM((tm, f), BF16),
        ],
        compiler_params=_params("arbitrary"),
        name="sublayer_tail",
    )(o, w_out.astype(BF16), ln_g[layer, 0].reshape(1, d), ln_b[layer, 0].reshape(1, d), x, mod,
      w_up, conv_w, conv_b.reshape(1, f), w_down, ln_g[layer, 1].reshape(1, d), ln_b[layer, 1].reshape(1, d))


def _gdn_in_kernel(x_ref, mod_ref, wqkv_ref, wz_ref, wab_ref, cw_ref, alog_ref, dtb_ref,
                   q_ref, k_ref, v_ref, z_ref, gb_ref, gt_ref, buf_ref, *, tiles_per_seq, chunk):
    i = pl.program_id(0)
    tm, d = x_ref.shape
    nh = GDN_HEADS
    hd = d // nh
    h = _modulate(x_ref[...], mod_ref, 0, 1)

    @pl.when(i % tiles_per_seq == 0)
    def _():
        buf_ref[0:SUBLANES, :] = jnp.zeros((SUBLANES, buf_ref.shape[1]), F32)

    outs = (q_ref, k_ref, v_ref)
    for s in range(3):
        cs = slice(s * d, (s + 1) * d)
        buf_ref[SUBLANES:SUBLANES + tm, cs] = _nn(h, wqkv_ref[:, cs])
        y = None
        for tap in range(GDN_CONV):
            off = SUBLANES - (GDN_CONV - 1) + tap
            term = cw_ref[tap:tap + 1, cs] * buf_ref[off:off + tm, cs]
            y = term if y is None else y + term
        y = _silu(y)
        if s < 2:
            scale = hd ** -0.5 if s == 0 else 1.0
            for hh in range(nh):
                seg = y[:, hh * hd:(hh + 1) * hd]
                inv = lax.rsqrt(jnp.sum(seg * seg, axis=-1, keepdims=True) + 1e-6)
                outs[s][:, hh * hd:(hh + 1) * hd] = (seg * (inv * scale)).astype(BF16)
        else:
            outs[s][...] = y.astype(BF16)
    buf_ref[0:SUBLANES, :] = buf_ref[tm:tm + SUBLANES, :]

    z_ref[...] = _nn(h, wz_ref[...]).astype(BF16)

    pab = _nn(h, wab_ref[...])
    lane = lax.broadcasted_iota(jnp.int32, pab.shape, 1)
    g = -jnp.exp(alog_ref[...]) * _softplus(pab + dtb_ref[...])
    gb = jnp.where(lane < nh, g, jnp.where(lane < 2 * nh, _sigmoid(pab), 0.0))
    row = lax.broadcasted_iota(jnp.int32, (tm, tm), 0)
    col = lax.broadcasted_iota(jnp.int32, (tm, tm), 1)
    tri = jnp.where(col <= row, jnp.where(jnp.bitwise_xor(row, col) < chunk, 1.0, 0.0), 0.0).astype(BF16)
    hi, mid, lo = _split3(gb)
    cum = _nn(tri, hi) + _nn(tri, mid) + _nn(tri, lo)
    gb = jnp.where(lane < nh, cum, gb)
    gb_ref[...] = gb
    gt_ref[...] = gb.T[0:2 * nh, :]


def _gdn_in(x, mod, w_in, conv_w, a_log, dt_bias, *, seq, tm=256):
    t, d = x.shape
    nh = GDN_HEADS
    tps = seq // tm
    w_b = w_in.astype(BF16)
    w_ab = jnp.pad(w_b[:, 4 * d:], ((0, 0), (0, LANES - 2 * nh)))
    once = pl.Buffered(1)
    alog = jnp.pad(a_log, (0, LANES - nh)).reshape(1, LANES)
    dtb = jnp.pad(dt_bias, (0, LANES - nh)).reshape(1, LANES)
    row = lambda i: (i, 0)
    fixed = lambda i: (0, 0)
    return pl.pallas_call(
        functools.partial(_gdn_in_kernel, tiles_per_seq=tps, chunk=GDN_CHUNK),
        grid=(t // tm,),
        in_specs=[
            pl.BlockSpec((tm, d), row),
            pl.BlockSpec((1, 6, d), lambda i: (i // tps, 0, 0)),
            pl.BlockSpec((d, 3 * d), fixed, pipeline_mode=once),
            pl.BlockSpec((d, d), lambda i: (0, 3), pipeline_mode=once),
            pl.BlockSpec((d, LANES), fixed, pipeline_mode=once),
            pl.BlockSpec((GDN_CONV, 3 * d), fixed, pipeline_mode=once),
            pl.BlockSpec((1, LANES), fixed),
            pl.BlockSpec((1, LANES), fixed),
        ],
        out_specs=[pl.BlockSpec((tm, d), row)] * 4 + [pl.BlockSpec((tm, LANES), row),
                                                      pl.BlockSpec((2 * nh, tm), lambda i: (0, i))],
        out_shape=[jax.ShapeDtypeStruct((t, d), BF16)] * 4 + [jax.ShapeDtypeStruct((t, LANES), F32),
                                                              jax.ShapeDtypeStruct((2 * nh, t), F32)],
        scratch_shapes=[pltpu.VMEM((tm + SUBLANES, 3 * d), F32)],
        compiler_params=_params("arbitrary"),
        name="gdn_in",
    )(x, mod, w_b, w_b, w_ab, conv_w, alog, dtb)


def _gdn_core_kernel(q_ref, k_ref, v_ref, z_ref, gb_ref, gt_ref, nw_ref, o_ref,
                     state_ref, inv_ref, pw_ref, rhs_ref, u_ref, w_ref, qk_ref, qg_ref, kd_ref, dl_ref,
                     vn_ref, oi_ref, *, chunk):
    rows, d = q_ref.shape
    nh = GDN_HEADS
    hd = d // nh
    n_chunks = rows // chunk
    heads = range(nh)
    lanes = [slice(hh * hd, (hh + 1) * hd) for hh in heads]

    @pl.when(pl.program_id(1) == 0)
    def _():
        state_ref[...] = jnp.zeros(state_ref.shape, F32)

    row = lax.broadcasted_iota(jnp.int32, (rows, rows), 0)
    col = lax.broadcasted_iota(jnp.int32, (rows, rows), 1)
    same_chunk = jnp.bitwise_xor(row, col) < chunk
    gb = gb_ref[...]
    for hh in heads:
        kh = k_ref[:, lanes[hh]]
        kf = kh.astype(F32)
        qh = q_ref[:, lanes[hh]]
        gc = jnp.broadcast_to(gb[:, hh:hh + 1], (rows, hd))
        beta = jnp.broadcast_to(gb[:, nh + hh:nh + hh + 1], (rows, hd))
        diff = jnp.broadcast_to(gb[:, hh:hh + 1], (rows, rows)) - gt_ref[hh:hh + 1, :]
        causal = same_chunk & (row >= col)
        decay = jnp.where(causal, jnp.exp(jnp.where(causal, diff, 0.0)), 0.0)
        kb = kf * beta
        a = jnp.where(row > col, _nt(kb.astype(BF16), kh) * decay, 0.0)
        inv_ref[hh] = jnp.where(row == col, 1.0, 0.0) - a
        pw_ref[hh] = a.astype(BF16)
        qk_ref[hh] = (_nt(qh, kh) * decay).astype(BF16)
        rhs_ref[hh, :, 0:hd] = (v_ref[:, lanes[hh]].astype(F32) * beta).astype(BF16)
        rhs_ref[hh, :, hd:2 * hd] = (kb * jnp.exp(gc)).astype(BF16)
        qg_ref[hh] = (qh.astype(F32) * jnp.exp(gc)).astype(BF16)
        for c in range(n_chunks):
            rs = slice(c * chunk, (c + 1) * chunk)
            g_last = gc[(c + 1) * chunk - 1:(c + 1) * chunk, :]
            kd_ref[hh, rs, :] = (kf[rs] * jnp.exp(g_last - gc[rs])).astype(BF16)
            dl_ref[hh, c] = jnp.exp(g_last)

    for _ in range(int(math.log2(chunk)) - 1):
        for hh in heads:
            power = _nn(pw_ref[hh], pw_ref[hh]).astype(BF16)
            pw_ref[hh] = power
            inv = inv_ref[hh]
            inv_ref[hh] = inv + _nn(inv.astype(BF16), power)
    for hh in heads:
        sol = _nn(inv_ref[hh].astype(BF16), rhs_ref[hh])
        u_ref[hh] = sol[:, 0:hd]
        w_ref[hh] = sol[:, hd:2 * hd].astype(BF16)

    for c in range(n_chunks):
        rs = slice(c * chunk, (c + 1) * chunk)
        states = [state_ref[hh].astype(BF16) for hh in heads]
        v_new = [(u_ref[hh, rs, :] - _nn(w_ref[hh, rs, :], states[hh])).astype(BF16) for hh in heads]
        for hh in heads:
            vn_ref[hh, rs, :] = v_new[hh]
            state_ref[hh] = state_ref[hh] * dl_ref[hh, c] + _tn(kd_ref[hh, rs, :], v_new[hh])
            oi_ref[hh, rs, :] = _nn(qg_ref[hh, rs, :], states[hh])

    for hh in heads:
        o = oi_ref[hh] + _nn(qk_ref[hh], vn_ref[hh])
        o = o * lax.rsqrt(jnp.mean(o * o, axis=-1, keepdims=True) + 1e-6) * nw_ref[...]
        o_ref[:, lanes[hh]] = (o * _silu(z_ref[:, lanes[hh]].astype(F32))).astype(BF16)


def _gdn_core(q, k, v, z, gb, gt, norm_w, *, batch, seq, rows=256):
    t, d = q.shape
    n = seq // rows
    nh = GDN_HEADS
    hd = d // nh
    row = lambda b, j: (b * n + j, 0)
    return pl.pallas_call(
        functools.partial(_gdn_core_kernel, chunk=GDN_CHUNK),
        grid=(batch, n),
        in_specs=[pl.BlockSpec((rows, d), row)] * 4 + [
            pl.BlockSpec((rows, LANES), row),
            pl.BlockSpec((2 * nh, rows), lambda b, j: (0, b * n + j)),
            pl.BlockSpec((1, hd), lambda b, j: (0, 0)),
        ],
        out_specs=pl.BlockSpec((rows, d), row),
        out_shape=jax.ShapeDtypeStruct((t, d), BF16),
        scratch_shapes=[
            pltpu.VMEM((nh, hd, hd), F32),
            pltpu.VMEM((nh, rows, rows), F32),
            pltpu.VMEM((nh, rows, rows), BF16),
            pltpu.VMEM((nh, rows, 2 * hd), BF16),
            pltpu.VMEM((nh, rows, hd), F32),
            pltpu.VMEM((nh, rows, hd), BF16),
            pltpu.VMEM((nh, rows, rows), BF16),
            pltpu.VMEM((nh, rows, hd), BF16),
            pltpu.VMEM((nh, rows, hd), BF16),
            pltpu.VMEM((nh, rows // GDN_CHUNK, 1, hd), F32),
            pltpu.VMEM((nh, rows, hd), BF16),
            pltpu.VMEM((nh, rows, hd), F32),
        ],
        compiler_params=_params("arbitrary", "arbitrary"),
        name="gdn_core",
    )(q, k, v, z, gb, gt, norm_w.reshape(1, hd))


def _rope_table_kernel(cos_ref, sin_ref):
    ts, half = cos_ref.shape
    pos = (lax.broadcasted_iota(jnp.int32, (ts, half), 0) + pl.program_id(0) * ts).astype(F32)
    frac = lax.broadcasted_iota(jnp.int32, (ts, half), 1).astype(F32) / (half - 1.0)
    ang = pos * jnp.exp(-frac * math.log(RET_ROPE_BASE))
    cos_ref[...] = jnp.cos(ang)
    sin_ref[...] = jnp.sin(ang)


def _rope_tables(seq, half, ts=256):
    spec = pl.BlockSpec((ts, half), lambda i: (i, 0))
    return pl.pallas_call(
        _rope_table_kernel,
        grid=(seq // ts,),
        out_specs=[spec, spec],
        out_shape=[jax.ShapeDtypeStruct((seq, half), F32)] * 2,
        compiler_params=_params("arbitrary"),
        name="rope_tables",
    )()


def _ret_in_kernel(x_ref, mod_ref, w_ref, cos_ref, sin_ref, q_ref, k_ref, v_ref, gate_ref):
    d = x_ref.shape[1]
    nh = RET_HEADS
    dk = d // nh
    half = dk // 2
    h = _modulate(x_ref[...], mod_ref, 0, 1)
    cos_a = cos_ref[...]
    sin_a = sin_ref[...]
    for s, (out, scale) in enumerate(((q_ref, 1.0), (k_ref, dk ** -0.5))):
        t = _nn(h, w_ref[:, s * d:(s + 1) * d])
        for hh in range(nh):
            t1 = t[:, hh * dk:hh * dk + half]
            t2 = t[:, hh * dk + half:(hh + 1) * dk]
            out[:, hh * dk:hh * dk + half] = ((t1 * cos_a - t2 * sin_a) * scale).astype(BF16)
            out[:, hh * dk + half:(hh + 1) * dk] = ((t1 * sin_a + t2 * cos_a) * scale).astype(BF16)
    v_ref[...] = _nn(h, w_ref[:, 2 * d:4 * d]).astype(BF16)
    gate_ref[...] = _nn(h, w_ref[:, 4 * d:6 * d])


def _ret_in(x, mod, w_in, cos_t, sin_t, *, seq, tm=256):
    t, d = x.shape
    tps = seq // tm
    half = cos_t.shape[1]
    row = lambda i: (i, 0)
    return pl.pallas_call(
        _ret_in_kernel,
        grid=(t // tm,),
        in_specs=[
            pl.BlockSpec((tm, d), row),
            pl.BlockSpec((1, 6, d), lambda i: (i // tps, 0, 0)),
            pl.BlockSpec((d, 6 * d), lambda i: (0, 0)),
            pl.BlockSpec((tm, half), lambda i: (i % tps, 0)),
            pl.BlockSpec((tm, half), lambda i: (i % tps, 0)),
        ],
        out_specs=[pl.BlockSpec((tm, d), row), pl.BlockSpec((tm, d), row),
                   pl.BlockSpec((tm, 2 * d), row), pl.BlockSpec((tm, 2 * d), row)],
        out_shape=[jax.ShapeDtypeStruct((t, d), BF16), jax.ShapeDtypeStruct((t, d), BF16),
                   jax.ShapeDtypeStruct((t, 2 * d), BF16), jax.ShapeDtypeStruct((t, 2 * d), F32)],
        compiler_params=_params("arbitrary"),
        name="ret_in",
    )(x, mod, w_in.astype(BF16), cos_t, sin_t)


def _ret_core_kernel(q_ref, k_ref, v_ref, gate_ref, o_ref, state_ref, *, chunk):
    rows, d = q_ref.shape
    c = chunk
    nh = RET_HEADS
    dk = d // nh
    dv = v_ref.shape[1] // nh

    @pl.when(pl.program_id(1) == 0)
    def _():
        state_ref[...] = jnp.zeros(state_ref.shape, F32)

    rel = (lax.broadcasted_iota(jnp.int32, (c, c), 0)
           - lax.broadcasted_iota(jnp.int32, (c, c), 1)).astype(F32)
    idx = lax.broadcasted_iota(jnp.int32, (c, 1), 0).astype(F32)
    for hh in range(nh):
        log_gamma = math.log(1.0 - 2.0 ** (-5.0 - hh))
        dmask = jnp.where(rel >= 0, jnp.exp(jnp.maximum(rel, 0.0) * log_gamma), 0.0)
        zeta = jnp.exp((c - 1.0 - idx) * log_gamma)
        xi = jnp.exp((idx + 1.0) * log_gamma)
        state = state_ref[hh]
        for ci in range(rows // c):
            rs = slice(ci * c, (ci + 1) * c)
            qh = q_ref[rs, hh * dk:(hh + 1) * dk]
            kh = k_ref[rs, hh * dk:(hh + 1) * dk]
            vh = v_ref[rs, hh * dv:(hh + 1) * dv]
            scores = _nt(qh, kh) * dmask
            o = _nn(scores.astype(BF16), vh) + _nn(qh, state.astype(BF16)) * xi
            state = state * math.exp(c * log_gamma) + _tn((kh.astype(F32) * zeta).astype(BF16), vh)
            mu = jnp.mean(o, axis=-1, keepdims=True)
            oc = o - mu
            var = jnp.mean(oc * oc, axis=-1, keepdims=True)
            o = oc * lax.rsqrt(var + 1e-6)
            gate = gate_ref[rs, hh * dv:(hh + 1) * dv]
            o_ref[rs, hh * dv:(hh + 1) * dv] = (o * _silu(gate)).astype(BF16)
        state_ref[hh] = state


def _ret_core(q, k, v, gate, *, batch, seq, rows=256):
    t, d = q.shape
    c = rows
    n = seq // rows
    dk = d // RET_HEADS
    dv = v.shape[1] // RET_HEADS
    row = lambda b, j: (b * n + j, 0)
    return pl.pallas_call(
        functools.partial(_ret_core_kernel, chunk=RET_CHUNK),
        grid=(batch, n),
        in_specs=[pl.BlockSpec((c, d), row), pl.BlockSpec((c, d), row),
                  pl.BlockSpec((c, 2 * d), row), pl.BlockSpec((c, 2 * d), row)],
        out_specs=pl.BlockSpec((c, 2 * d), row),
        out_shape=jax.ShapeDtypeStruct((t, 2 * d), BF16),
        scratch_shapes=[pltpu.VMEM((RET_HEADS, dk, dv), F32)],
        compiler_params=_params("arbitrary", "arbitrary"),
        name="ret_core",
    )(q, k, v, gate)


def _gelu(t):
    p = 0.3275911 / math.sqrt(2.0)
    half_coefs = [0.5 * a for a in (0.254829592, -0.284496736, 1.421413741, -1.453152027, 1.061405429)]
    u = 1.0 / (1.0 + p * jnp.abs(t))
    poly = half_coefs[4]
    for coef in half_coefs[3::-1]:
        poly = poly * u + coef
    e = poly * u * jnp.exp2(t * t * (-0.5 * math.log2(math.e)))
    return t * jnp.where(t >= 0.0, 1.0 - e, e)


def _gmlp_kernel(x_ref, mod_ref, win_ref, lng_ref, lnb_ref, ws_ref, bs_ref, out_ref):
    tm, d = x_ref.shape
    width = win_ref.shape[1] // 2
    c = GMLP_CHUNK
    ng = GMLP_GROUPS
    gw = width // ng
    h = _modulate(x_ref[...], mod_ref, 0, 1)

    v = _gelu(_nn(h, win_ref[:, width:]))
    v = _layer_norm_rows(v, lng_ref[...], lnb_ref[...], LN_EPS).astype(BF16)
    u = _gelu(_nn(h, win_ref[:, :width]))
    row = lax.broadcasted_iota(jnp.int32, (c, c), 0)
    col = lax.broadcasted_iota(jnp.int32, (c, c), 1)
    for n in range(tm // c):
        rs = slice(n * c, (n + 1) * c)
        for gi in range(ng):
            ls = slice(gi * gw, (gi + 1) * gw)
            ws = jnp.where(row >= col, ws_ref[gi], 0.0).astype(BF16)
            vs = _nn(ws, v[rs, ls]) + bs_ref[:, gi:gi + 1]
            out_ref[rs, ls] = (u[rs, ls] * vs).astype(BF16)


def _gmlp(x, mod, w_in, ln_g, ln_b, w_s, b_s, *, seq, tm=256):
    t, d = x.shape
    width = w_in.shape[1] // 2
    ng, c, _ = w_s.shape
    tps = seq // tm
    return pl.pallas_call(
        _gmlp_kernel,
        grid=(t // tm,),
        in_specs=[
            pl.BlockSpec((tm, d), lambda i: (i, 0)),
            pl.BlockSpec((1, 6, d), lambda i: (i // tps, 0, 0)),
            _resident((d, 2 * width)),
            _resident((1, width)),
            _resident((1, width)),
            _resident((ng, c, c)),
            _resident((c, ng)),
        ],
        out_specs=pl.BlockSpec((tm, width), lambda i: (i, 0)),
        out_shape=jax.ShapeDtypeStruct((t, width), BF16),
        compiler_params=_params("arbitrary"),
        name="gmlp",
    )(x, mod, w_in.astype(BF16), ln_g.reshape(1, width), ln_b.reshape(1, width), w_s, b_s.T)


def _sb_in_kernel(x_ref, mod_ref, w_ref, q_ref, k_ref, v_ref, *, q_scale):
    d = x_ref.shape[1]
    h = _modulate(x_ref[...], mod_ref, 0, 1)
    q_ref[...] = (_nn(h, w_ref[:, 0:d]) * q_scale).astype(BF16)
    k_ref[...] = _nn(h, w_ref[:, d:2 * d]).astype(BF16)
    v_ref[...] = _nn(h, w_ref[:, 2 * d:3 * d]).astype(BF16)


def _sb_in(x, mod, w_in, *, seq, tm=512):
    t, d = x.shape
    tps = seq // tm
    row = lambda i: (i, 0)
    return pl.pallas_call(
        functools.partial(_sb_in_kernel, q_scale=(d // SB_HEADS) ** -0.5 * math.log2(math.e)),
        grid=(t // tm,),
        in_specs=[
            pl.BlockSpec((tm, d), row),
            pl.BlockSpec((1, 6, d), lambda i: (i // tps, 0, 0)),
            pl.BlockSpec((d, 3 * d), lambda i: (0, 0)),
        ],
        out_specs=[pl.BlockSpec((tm, d), row)] * 3,
        out_shape=[jax.ShapeDtypeStruct((t, d), BF16)] * 3,
        compiler_params=_params("arbitrary"),
        name="sb_in",
    )(x, mod, w_in.astype(BF16))


def _sb_core_kernel(q_ref, k_ref, v_ref, o_ref, qh_ref, sp_ref, zc_ref, carry_ref, acc_ref, knorm_ref,
                    bound_ref):
    tq, width = q_ref.shape
    seq = k_ref.shape[0]
    n_heads = qh_ref.shape[0]
    dh = LANES // 2
    qi = pl.program_id(2)
    lane = lax.broadcasted_iota(jnp.int32, (tq, LANES), 1)
    half_lanes = (lane < dh, lane >= dh)
    group = [slice((hh // 2) * LANES, (hh // 2 + 1) * LANES) for hh in range(n_heads)]
    row = lax.broadcasted_iota(jnp.int32, (tq, tq), 0)
    col = lax.broadcasted_iota(jnp.int32, (tq, tq), 1)
    ones_lower = jnp.where(row >= col, 1.0, 0.0).astype(BF16)

    @pl.when(qi == 0)
    def _():
        def key_tile(t, best):
            kf = k_ref[pl.ds(pl.multiple_of(t * tq, tq), tq), :].astype(F32)
            return jnp.maximum(best, jnp.max(kf * kf, axis=0, keepdims=True))

        col_max = lax.fori_loop(0, seq // tq, key_tile, jnp.zeros((1, width), F32))
        for hh in range(n_heads):
            bound_sq = jnp.sum(jnp.where(half_lanes[hh % 2][0:1, :], col_max[:, group[hh]], 0.0),
                               axis=-1, keepdims=True)
            knorm_ref[hh] = jnp.broadcast_to(bound_sq, knorm_ref.shape[1:])

    for hh in range(n_heads):
        q = q_ref[:, group[hh]]
        qf = q.astype(F32)
        qh_ref[hh] = jnp.where(half_lanes[hh % 2], q, jnp.zeros_like(q))
        qsq = jnp.sum(jnp.where(half_lanes[hh % 2], qf * qf, 0.0), axis=-1, keepdims=True)
        bound_ref[hh] = jnp.sqrt(qsq * knorm_ref[hh, 0:1, 0:1]) * NORM_BOUND_SLACK
    acc_ref[...] = jnp.zeros(acc_ref.shape, F32)
    carry_ref[...] = jnp.zeros(carry_ref.shape, F32)

    def stage(kt, slot, diagonal=False):
        rows = pl.ds(pl.multiple_of(kt * tq, tq), tq)
        for hh in range(n_heads):
            z = _nt(qh_ref[hh], k_ref[rows, group[hh]])
            sp = jnp.maximum(z, jnp.log2(1.0 + jnp.exp2(jnp.minimum(z, EXP2_CLAMP))))
            carry = carry_ref[hh]
            zc = z - carry
            if diagonal:
                sp = jnp.where(col < row, sp, 0.0)
                zc = jnp.where(col < row, zc, -1e30)
            sp_ref[slot, hh] = sp.astype(BF16)
            zc_ref[slot, hh] = zc
            carry_ref[hh] = carry + jnp.sum(sp, axis=-1, keepdims=True)

    def consume(kt, slot):
        rows = pl.ds(pl.multiple_of(kt * tq, tq), tq)
        for hh in range(n_heads):
            inclusive = _nn(sp_ref[slot, hh], ones_lower)
            a = jnp.exp2(zc_ref[slot, hh] - inclusive)
            acc_ref[hh] += _nn(a.astype(BF16), v_ref[rows, group[hh]])

    def still_alive():
        slack = bound_ref[0] - carry_ref[0]
        for hh in range(1, n_heads):
            slack = jnp.maximum(slack, bound_ref[hh] - carry_ref[hh])
        return (jnp.max(slack) > -DEAD_EXPONENT).astype(jnp.int32)

    stage(qi, 0, diagonal=True)

    @pl.when(qi == 0)
    def _():
        consume(0, 0)

    @pl.when(qi > 0)
    def _():
        stage(qi - 1, 1)
        alive = still_alive()
        consume(qi, 0)
        n_pairs = (qi - 1) // 2

        def more(state):
            return (state[0] < n_pairs) & (state[1] > 0)

        def pair(state):
            kt = qi - 1 - 2 * state[0]
            stage(kt - 1, 0)
            consume(kt, 1)
            stage(kt - 2, 1)
            alive = still_alive()
            consume(kt - 1, 0)
            return state[0] + 1, alive

        pairs_done, alive = lax.while_loop(more, pair, (jnp.int32(0), alive))
        staged = qi - 1 - 2 * pairs_done

        @pl.when((alive == 0) | (staged == 0))
        def _():
            consume(staged, 1)

        @pl.when((alive > 0) & (staged == 1))
        def _():
            stage(0, 0)
            consume(1, 1)
            consume(0, 0)

    for hh in range(0, n_heads, 2):
        o_ref[:, group[hh]] = jnp.where(half_lanes[0], acc_ref[hh], acc_ref[hh + 1]).astype(BF16)


def _sb_core(q, k, v, *, batch, seq, tq=256, heads_per_step=8):
    t, d = q.shape
    tq = min(tq, seq)
    nq = seq // tq
    nh = heads_per_step
    width = nh // 2 * LANES
    return pl.pallas_call(
        _sb_core_kernel,
        grid=(batch, d // width, nq),
        in_specs=[
            pl.BlockSpec((tq, width), lambda b, g, i: (b * nq + i, g)),
            pl.BlockSpec((seq, width), lambda b, g, i: (b, g)),
            pl.BlockSpec((seq, width), lambda b, g, i: (b, g)),
        ],
        out_specs=pl.BlockSpec((tq, width), lambda b, g, i: (b * nq + i, g)),
        out_shape=jax.ShapeDtypeStruct((t, d), BF16),
        scratch_shapes=[
            pltpu.VMEM((nh, tq, LANES), BF16),
            pltpu.VMEM((2, nh, tq, tq), BF16),
            pltpu.VMEM((2, nh, tq, tq), F32),
            pltpu.VMEM((nh, tq, 1), F32),
            pltpu.VMEM((nh, tq, LANES), F32),
            pltpu.VMEM((nh, SUBLANES, LANES), F32),
            pltpu.VMEM((nh, tq, 1), F32),
        ],
        compiler_params=_params("arbitrary", "arbitrary", "arbitrary"),
        name="sb_core",
    )(q, k, v)


def kernel(x, c, cond_w, cond_b, ada_w, ada_b, ln_g, ln_b, ffn_up, ffn_conv_w, ffn_conv_b, ffn_down,
           gdn_w_in, gdn_conv_w, gdn_a_log, gdn_dt_bias, gdn_norm_w, gdn_w_out,
           ret_w_in, ret_w_out,
           gmlp_w_in, gmlp_ln_g, gmlp_ln_b, gmlp_w_s, gmlp_b_s, gmlp_w_out,
           sb_w_in, sb_w_out):
    batch, seq, d = x.shape
    depth = ada_w.shape[0]
    alpha = (2.0 * depth) ** 0.25
    mods = _conditioning(c, cond_w, cond_b, ada_w, ada_b).reshape(depth, batch, 6, d)
    xt = x.reshape(batch * seq, d)
    ffn_up_b = ffn_up.astype(BF16)
    ffn_down_b = ffn_down.astype(BF16)

    for i in range(depth):
        mod = mods[i]
        mixer = i % 4
        if mixer == 0:
            q, k, v, z, gb, gt = _gdn_in(xt, mod, gdn_w_in, gdn_conv_w, gdn_a_log, gdn_dt_bias, seq=seq)
            o, w_out = _gdn_core(q, k, v, z, gb, gt, gdn_norm_w, batch=batch, seq=seq), gdn_w_out
        elif mixer == 1:
            cos_t, sin_t = _rope_tables(seq, d // RET_HEADS // 2)
            q, k, v, gate = _ret_in(xt, mod, ret_w_in, cos_t, sin_t, seq=seq)
            o, w_out = _ret_core(q, k, v, gate, batch=batch, seq=seq), ret_w_out
        elif mixer == 2:
            o = _gmlp(xt, mod, gmlp_w_in, gmlp_ln_g, gmlp_ln_b, gmlp_w_s, gmlp_b_s, seq=seq)
            w_out = gmlp_w_out
        else:
            q, k, v = _sb_in(xt, mod, sb_w_in, seq=seq)
            o, w_out = _sb_core(q, k, v, batch=batch, seq=seq), sb_w_out
        xt = _sublayer_tail(o, w_out, xt, mod, ffn_up_b, ffn_conv_w[i], ffn_conv_b[i], ffn_down_b,
                            ln_g, ln_b, layer=i, seq=seq, alpha=alpha)
    return xt.reshape(batch, seq, d)
```

```python
import functools
import math

import jax
import jax.numpy as jnp
from jax import lax
from jax.experimental import pallas as pl
from jax.experimental.pallas import tpu as pltpu

F32 = jnp.float32
BF16 = jnp.bfloat16

LANES = 128
SUBLANES = 8
VMEM_LIMIT = 56 * 1024 * 1024

LN_EPS = 1e-5
GDN_HEADS = 8
GDN_CHUNK = 64
GDN_CONV = 4
RET_HEADS = 4
RET_CHUNK = 128
RET_ROPE_BASE = 10000.0
GMLP_CHUNK = 128
GMLP_GROUPS = 8
SB_HEADS = 16
FFN_CONV = 3
DEAD_EXPONENT = 160.0
NORM_BOUND_SLACK = 1.001
EXP2_CLAMP = 64.0


def _params(*sem):
    return pltpu.CompilerParams(dimension_semantics=sem, vmem_limit_bytes=VMEM_LIMIT)


def _nn(a, b):
    return jnp.dot(a, b, preferred_element_type=F32)


def _nt(a, b):
    return lax.dot_general(a, b, (((1,), (1,)), ((), ())), preferred_element_type=F32)


def _tn(a, b):
    return lax.dot_general(a, b, (((0,), (0,)), ((), ())), preferred_element_type=F32)


def _sigmoid(x):
    return 1.0 / (1.0 + jnp.exp(-x))


def _silu(x):
    return x * _sigmoid(x)


def _softplus(x):
    return jnp.maximum(x, 0.0) + jnp.log(1.0 + jnp.exp(-jnp.abs(x)))


def _split3(x):
    hi = x.astype(BF16)
    r = x - hi.astype(F32)
    mid = r.astype(BF16)
    lo = (r - mid.astype(F32)).astype(BF16)
    return hi, mid, lo


def _modulate(x, mod_ref, shift_row, scale_row):
    return (x * (1.0 + mod_ref[0, scale_row:scale_row + 1, :])
            + mod_ref[0, shift_row:shift_row + 1, :]).astype(BF16)


def _layer_norm_rows(r, g, b, eps):
    mu = jnp.mean(r, axis=-1, keepdims=True)
    rc = r - mu
    var = jnp.mean(rc * rc, axis=-1, keepdims=True)
    return rc * lax.rsqrt(var + eps) * g + b


def _lane_replicated_columns(rows):
    n, k = rows.shape
    padded = jnp.concatenate([rows, jnp.zeros((LANES - n, k), F32)], axis=0)
    t = padded.T
    return [jnp.broadcast_to(t[:, b:b + 1], (k, LANES)) for b in range(n)]


def _rowvec_matmul(col, w):
    n = w.shape[1]
    parts = [jnp.sum(col * w[:, c:c + LANES], axis=0, keepdims=True) for c in range(0, n, LANES)]
    return jnp.concatenate(parts, axis=1)


def _cond_kernel(c_ref, cw_ref, cb_ref, aw_ref, ab_ref, o_ref, ecol_ref):
    nb = c_ref.shape[0]

    @pl.when((pl.program_id(0) == 0) & (pl.program_id(1) == 0))
    def _():
        ccols = _lane_replicated_columns(c_ref[...])
        cw = cw_ref[...]
        e = jnp.concatenate([_rowvec_matmul(col, cw) for col in ccols], axis=0) + cb_ref[...]
        ecols = _lane_replicated_columns(_silu(e))
        for b in range(nb):
            ecol_ref[b] = ecols[b]

    w = aw_ref[0]
    for b in range(nb):
        o_ref[0, b:b + 1, :] = _rowvec_matmul(ecol_ref[b], w) + ab_ref[0]


def _conditioning(c, cond_w, cond_b, ada_w, ada_b):
    nb, d = c.shape
    depth, _, n6 = ada_w.shape
    tn = 1536
    return pl.pallas_call(
        _cond_kernel,
        grid=(depth, n6 // tn),
        in_specs=[
            pl.BlockSpec((nb, d), lambda i, j: (0, 0)),
            pl.BlockSpec((d, d), lambda i, j: (0, 0)),
            pl.BlockSpec((1, d), lambda i, j: (0, 0)),
            pl.BlockSpec((1, d, tn), lambda i, j: (i, 0, j)),
            pl.BlockSpec((1, 1, tn), lambda i, j: (i, 0, j)),
        ],
        out_specs=pl.BlockSpec((1, nb, tn), lambda i, j: (i, 0, j)),
        out_shape=jax.ShapeDtypeStruct((depth, nb, n6), F32),
        scratch_shapes=[pltpu.VMEM((nb, d, LANES), F32)],
        compiler_params=_params("arbitrary", "arbitrary"),
        name="conditioning",
    )(c, cond_w, cond_b.reshape(1, d), ada_w, ada_b.reshape(depth, 1, n6))


def _ffn_kernel(o_ref, wo_ref, g1_ref, b1_ref, x_ref, mod_ref, wup_ref, cw_ref, cb_ref, wd_ref,
                g_ref, b_ref, out_ref, x1_ref, h_ref, buf_ref, act_ref, *, tiles_per_seq, alpha):
    i = pl.program_id(0)
    tm = x_ref.shape[0]
    f = wd_ref.shape[0]
    n_chunks, _, tf = buf_ref.shape
    half = tm // 2
    halves = (slice(0, half), slice(half, tm))
    for rs in halves:
        mixed = alpha * x_ref[rs, :] + (1.0 + mod_ref[0, 2:3, :]) * _nn(o_ref[rs, :], wo_ref[...])
        x1 = _layer_norm_rows(mixed, g1_ref[...], b1_ref[...], LN_EPS)
        x1_ref[rs, :] = x1
        h_ref[rs, :] = _modulate(x1, mod_ref, 3, 4)

    @pl.when(i % tiles_per_seq == 0)
    def _():
        for c in range(n_chunks):
            buf_ref[c, 0:SUBLANES, :] = jnp.zeros((SUBLANES, tf), F32)

    for c in range(n_chunks):
        cols = slice(c * tf, (c + 1) * tf)
        buf_ref[c, SUBLANES:SUBLANES + tm, :] = _nn(h_ref[...], wup_ref[:, cols])
        conv = cb_ref[:, cols]
        for tap in range(FFN_CONV):
            off = SUBLANES - (FFN_CONV - 1) + tap
            conv = conv + cw_ref[tap:tap + 1, cols] * buf_ref[c, off:off + tm, :]
        buf_ref[c, 0:SUBLANES, :] = buf_ref[c, tm:tm + SUBLANES, :]
        up = _nn(h_ref[...], wup_ref[:, f + c * tf:f + (c + 1) * tf])
        act_ref[:, cols] = (_silu(conv) * up).astype(BF16)

    for rs in halves:
        y = _nn(act_ref[rs, :], wd_ref[...])
        r = alpha * x1_ref[rs, :] + (1.0 + mod_ref[0, 5:6, :]) * y
        out_ref[rs, :] = _layer_norm_rows(r, g_ref[...], b_ref[...], LN_EPS)


def _resident(shape, layer=None):
    if layer is None:
        return pl.BlockSpec(shape, lambda *_: (0,) * len(shape), pipeline_mode=pl.Buffered(1))
    return pl.BlockSpec((None,) + tuple(shape), lambda *_: (layer,) + (0,) * len(shape),
                        pipeline_mode=pl.Buffered(1))


def _sublayer_tail(o, w_out, x, mod, w_up, conv_w, conv_b, w_down, ln_g, ln_b, *, layer, seq, alpha,
                   tm=512, n_chunks=2):
    t, d = x.shape
    kdim = o.shape[1]
    f = w_down.shape[1]
    tf = f // n_chunks
    tps = seq // tm
    rows = lambda i: (i, 0)
    return pl.pallas_call(
        functools.partial(_ffn_kernel, tiles_per_seq=tps, alpha=alpha),
        grid=(t // tm,),
        in_specs=[
            pl.BlockSpec((tm, kdim), rows),
            _resident((kdim, d)),
            _resident((1, d)),
            _resident((1, d)),
            pl.BlockSpec((tm, d), rows),
            pl.BlockSpec((1, 6, d), lambda i: (i // tps, 0, 0)),
            _resident((d, 2 * f), layer),
            _resident((FFN_CONV, f)),
            _resident((1, f)),
            _resident((f, d), layer),
            _resident((1, d)),
            _resident((1, d)),
        ],
        out_specs=pl.BlockSpec((tm, d), rows),
        out_shape=jax.ShapeDtypeStruct((t, d), F32),
        scratch_shapes=[
            pltpu.VMEM((tm, d), F32),
            pltpu.VMEM((tm, d), BF16),
            pltpu.VMEM((n_chunks, tm + SUBLANES, tf), F32),
            pltpu.VMEM((tm, f), BF16),
        ],
        compiler_params=_params("arbitrary"),
        name="sublayer_tail",
    )(o, w_out.astype(BF16), ln_g[layer, 0].reshape(1, d), ln_b[layer, 0].reshape(1, d), x, mod,
      w_up, conv_w, conv_b.reshape(1, f), w_down, ln_g[layer, 1].reshape(1, d), ln_b[layer, 1].reshape(1, d))


class _Window:
    def __init__(self, ref, lead=(), row_start=None, rows=None):
        self.ref, self.lead, self.row_start, self.rows = ref, tuple(lead), row_start, rows

    @property
    def shape(self):
        shape = self.ref.shape[len(self.lead):]
        return shape if self.rows is None else (self.rows,) + tuple(shape[1:])

    def _index(self, idx):
        idx = idx if isinstance(idx, tuple) else (idx,)
        if self.row_start is not None:
            first = idx[0]
            if first is Ellipsis:
                idx = (pl.ds(self.row_start, self.rows), Ellipsis)
            else:
                start = 0 if first.start is None else first.start
                stop = self.rows if first.stop is None else first.stop
                idx = (pl.ds(self.row_start + start, stop - start),) + idx[1:]
        return self.lead + idx

    def __getitem__(self, idx):
        return self.ref[self._index(idx)]

    def __setitem__(self, idx, value):
        self.ref[self._index(idx)] = value


def _gdn_in_tile(x_ref, mod_ref, wqkv_ref, wz_ref, wab_ref, cw_ref, alog_ref, dtb_ref,
                 q_ref, k_ref, v_ref, z_ref, gb_ref, gt_ref, buf_ref, *, first, chunk):
    tm, d = x_ref.shape
    nh = GDN_HEADS
    hd = d // nh
    h = _modulate(x_ref[...], mod_ref, 0, 1)

    context = buf_ref[0:SUBLANES, :]
    buf_ref[0:SUBLANES, :] = jnp.where(first(context.shape), 0.0, context)

    outs = (q_ref, k_ref, v_ref)
    for s in range(3):
        cs = slice(s * d, (s + 1) * d)
        buf_ref[SUBLANES:SUBLANES + tm, cs] = _nn(h, wqkv_ref[:, cs])
        y = None
        for tap in range(GDN_CONV):
            off = SUBLANES - (GDN_CONV - 1) + tap
            term = cw_ref[tap:tap + 1, cs] * buf_ref[off:off + tm, cs]
            y = term if y is None else y + term
        y = _silu(y)
        if s < 2:
            scale = hd ** -0.5 if s == 0 else 1.0
            for hh in range(nh):
                seg = y[:, hh * hd:(hh + 1) * hd]
                inv = lax.rsqrt(jnp.sum(seg * seg, axis=-1, keepdims=True) + 1e-6)
                outs[s][:, hh * hd:(hh + 1) * hd] = (seg * (inv * scale)).astype(BF16)
        else:
            outs[s][...] = y.astype(BF16)
    buf_ref[0:SUBLANES, :] = buf_ref[tm:tm + SUBLANES, :]

    z_ref[...] = _nn(h, wz_ref[...]).astype(BF16)

    pab = _nn(h, wab_ref[...])
    lane = lax.broadcasted_iota(jnp.int32, pab.shape, 1)
    g = -jnp.exp(alog_ref[...]) * _softplus(pab + dtb_ref[...])
    gb = jnp.where(lane < nh, g, jnp.where(lane < 2 * nh, _sigmoid(pab), 0.0))
    row = lax.broadcasted_iota(jnp.int32, (tm, tm), 0)
    col = lax.broadcasted_iota(jnp.int32, (tm, tm), 1)
    tri = jnp.where(col <= row, jnp.where(jnp.bitwise_xor(row, col) < chunk, 1.0, 0.0), 0.0).astype(BF16)
    hi, mid, lo = _split3(gb)
    cum = _nn(tri, hi) + _nn(tri, mid) + _nn(tri, lo)
    gb = jnp.where(lane < nh, cum, gb)
    gb_ref[...] = gb
    gt_ref[...] = gb.T[0:2 * nh, :]


def _gdn_core_tile(q_ref, k_ref, v_ref, z_ref, gb_ref, gt_ref, nw_ref, o_ref,
                   state_ref, inv_ref, pw_ref, rhs_ref, u_ref, w_ref, qk_ref, qg_ref, kd_ref, dl_ref,
                   vn_ref, oi_ref, *, first, chunk):
    rows, d = q_ref.shape
    nh = GDN_HEADS
    hd = d // nh
    n_chunks = rows // chunk
    heads = range(nh)
    lanes = [slice(hh * hd, (hh + 1) * hd) for hh in heads]

    state_ref[...] = jnp.where(first(state_ref.shape), 0.0, state_ref[...])

    row = lax.broadcasted_iota(jnp.int32, (rows, rows), 0)
    col = lax.broadcasted_iota(jnp.int32, (rows, rows), 1)
    same_chunk = jnp.bitwise_xor(row, col) < chunk
    gb = gb_ref[...]
    for hh in heads:
        kh = k_ref[:, lanes[hh]]
        kf = kh.astype(F32)
        qh = q_ref[:, lanes[hh]]
        gc = jnp.broadcast_to(gb[:, hh:hh + 1], (rows, hd))
        beta = jnp.broadcast_to(gb[:, nh + hh:nh + hh + 1], (rows, hd))
        diff = jnp.broadcast_to(gb[:, hh:hh + 1], (rows, rows)) - gt_ref[hh:hh + 1, :]
        causal = same_chunk & (row >= col)
        decay = jnp.where(causal, jnp.exp(jnp.where(causal, diff, 0.0)), 0.0)
        kb = kf * beta
        a = jnp.where(row > col, _nt(kb.astype(BF16), kh) * decay, 0.0)
        inv_ref[hh] = jnp.where(row == col, 1.0, 0.0) - a
        pw_ref[hh] = a.astype(BF16)
        qk_ref[hh] = (_nt(qh, kh) * decay).astype(BF16)
        rhs_ref[hh, :, 0:hd] = (v_ref[:, lanes[hh]].astype(F32) * beta).astype(BF16)
        rhs_ref[hh, :, hd:2 * hd] = (kb * jnp.exp(gc)).astype(BF16)
        qg_ref[hh] = (qh.astype(F32) * jnp.exp(gc)).astype(BF16)
        for c in range(n_chunks):
            rs = slice(c * chunk, (c + 1) * chunk)
            g_last = gc[(c + 1) * chunk - 1:(c + 1) * chunk, :]
            kd_ref[hh, rs, :] = (kf[rs] * jnp.exp(g_last - gc[rs])).astype(BF16)
            dl_ref[hh, c] = jnp.exp(g_last)

    for _ in range(int(math.log2(chunk)) - 1):
        for hh in heads:
            power = _nn(pw_ref[hh], pw_ref[hh]).astype(BF16)
            pw_ref[hh] = power
            inv = inv_ref[hh]
            inv_ref[hh] = inv + _nn(inv.astype(BF16), power)
    for hh in heads:
        sol = _nn(inv_ref[hh].astype(BF16), rhs_ref[hh])
        u_ref[hh] = sol[:, 0:hd]
        w_ref[hh] = sol[:, hd:2 * hd].astype(BF16)

    for c in range(n_chunks):
        rs = slice(c * chunk, (c + 1) * chunk)
        states = [state_ref[hh].astype(BF16) for hh in heads]
        v_new = [(u_ref[hh, rs, :] - _nn(w_ref[hh, rs, :], states[hh])).astype(BF16) for hh in heads]
        for hh in heads:
            vn_ref[hh, rs, :] = v_new[hh]
            state_ref[hh] = state_ref[hh] * dl_ref[hh, c] + _tn(kd_ref[hh, rs, :], v_new[hh])
            oi_ref[hh, rs, :] = _nn(qg_ref[hh, rs, :], states[hh])

    for hh in heads:
        o = oi_ref[hh] + _nn(qk_ref[hh], vn_ref[hh])
        o = o * lax.rsqrt(jnp.mean(o * o, axis=-1, keepdims=True) + 1e-6) * nw_ref[...]
        o_ref[:, lanes[hh]] = (o * _silu(z_ref[:, lanes[hh]].astype(F32))).astype(BF16)


def _gdn_mixer_kernel(x0_ref, xa_ref, xb_ref, mod0_ref, moda_ref, modb_ref,
                      wqkv_ref, wz_ref, wab_ref, cw_ref, alog_ref, dtb_ref, nw_ref, o_ref,
                      q_st, k_st, v_st, z_st, gb_st, gt_st, buf_ref, *core_scratch, tiles_per_seq):
    j = pl.program_id(0)
    tm = xa_ref.shape[0]
    weights = (wqkv_ref, wz_ref, wab_ref, cw_ref, alog_ref, dtb_ref)
    staged = (q_st, k_st, v_st, z_st, gb_st, gt_st)

    def starts_sequence(tile):
        return lambda shape: (jnp.full(shape, tile, jnp.int32) & (tiles_per_seq - 1)) == 0

    def project(x_ref, mod_ref, slot, tile):
        _gdn_in_tile(x_ref, mod_ref, *weights, *[_Window(r, (slot,)) for r in staged], buf_ref,
                     first=starts_sequence(tile), chunk=GDN_CHUNK)

    def recur(slot, tile, row_start):
        _gdn_core_tile(*[_Window(r, (slot,)) for r in staged], nw_ref,
                       _Window(o_ref, row_start=row_start, rows=tm), *core_scratch,
                       first=starts_sequence(tile), chunk=GDN_CHUNK)

    @pl.when(j == 0)
    def _():
        buf_ref[0:SUBLANES, :] = jnp.zeros((SUBLANES, buf_ref.shape[1]), F32)
        core_scratch[0][...] = jnp.zeros(core_scratch[0].shape, F32)
        project(x0_ref, mod0_ref, 0, 0)

    project(xa_ref, moda_ref, 1, 2 * j + 1)
    recur(0, 2 * j, 0)

    @pl.when(j >= 0)
    def _():
        project(xb_ref, modb_ref, 0, 2 * j + 2)
        recur(1, 2 * j + 1, tm)


def _gdn_mixer(x, mod, w_in, conv_w, a_log, dt_bias, norm_w, *, seq, rows=256):
    t, d = x.shape
    nh = GDN_HEADS
    hd = d // nh
    tps = seq // rows
    n_tiles = t // rows
    w_b = w_in.astype(BF16)
    w_ab = jnp.pad(w_b[:, 4 * d:], ((0, 0), (0, LANES - 2 * nh)))
    alog = jnp.pad(a_log, (0, LANES - nh)).reshape(1, LANES)
    dtb = jnp.pad(dt_bias, (0, LANES - nh)).reshape(1, LANES)
    once = pl.Buffered(1)
    fixed = lambda j: (0, 0)
    tile_a = lambda j: 2 * j + 1
    tile_b = lambda j: jnp.minimum(2 * j + 2, n_tiles - 1)
    return pl.pallas_call(
        functools.partial(_gdn_mixer_kernel, tiles_per_seq=tps),
        grid=(n_tiles // 2,),
        in_specs=[
            pl.BlockSpec((rows, d), lambda j: (0, 0)),
            pl.BlockSpec((rows, d), lambda j: (tile_a(j), 0)),
            pl.BlockSpec((rows, d), lambda j: (tile_b(j), 0)),
            pl.BlockSpec((1, 6, d), lambda j: (0, 0, 0)),
            pl.BlockSpec((1, 6, d), lambda j: (tile_a(j) // tps, 0, 0)),
            pl.BlockSpec((1, 6, d), lambda j: (tile_b(j) // tps, 0, 0)),
            pl.BlockSpec((d, 3 * d), fixed, pipeline_mode=once),
            pl.BlockSpec((d, d), lambda j: (0, 3), pipeline_mode=once),
            pl.BlockSpec((d, LANES), fixed, pipeline_mode=once),
            pl.BlockSpec((GDN_CONV, 3 * d), fixed, pipeline_mode=once),
            pl.BlockSpec((1, LANES), fixed),
            pl.BlockSpec((1, LANES), fixed),
            pl.BlockSpec((1, hd), fixed),
        ],
        out_specs=pl.BlockSpec((2 * rows, d), lambda j: (j, 0)),
        out_shape=jax.ShapeDtypeStruct((t, d), BF16),
        scratch_shapes=[
            pltpu.VMEM((2, rows, d), BF16),
            pltpu.VMEM((2, rows, d), BF16),
            pltpu.VMEM((2, rows, d), BF16),
            pltpu.VMEM((2, rows, d), BF16),
            pltpu.VMEM((2, rows, LANES), F32),
            pltpu.VMEM((2, 2 * nh, rows), F32),
            pltpu.VMEM((rows + SUBLANES, 3 * d), F32),
            pltpu.VMEM((nh, hd, hd), F32),
            pltpu.VMEM((nh, rows, rows), F32),
            pltpu.VMEM((nh, rows, rows), BF16),
            pltpu.VMEM((nh, rows, 2 * hd), BF16),
            pltpu.VMEM((nh, rows, hd), F32),
            pltpu.VMEM((nh, rows, hd), BF16),
            pltpu.VMEM((nh, rows, rows), BF16),
            pltpu.VMEM((nh, rows, hd), BF16),
            pltpu.VMEM((nh, rows, hd), BF16),
            pltpu.VMEM((nh, rows // GDN_CHUNK, 1, hd), F32),
            pltpu.VMEM((nh, rows, hd), BF16),
            pltpu.VMEM((nh, rows, hd), F32),
        ],
        compiler_params=_params("arbitrary"),
        name="gdn_mixer",
    )(x, x, x, mod, mod, mod, w_b, w_b, w_ab, conv_w, alog, dtb, norm_w.reshape(1, hd))


def _rope_table_kernel(cos_ref, sin_ref):
    ts, half = cos_ref.shape
    pos = (lax.broadcasted_iota(jnp.int32, (ts, half), 0) + pl.program_id(0) * ts).astype(F32)
    frac = lax.broadcasted_iota(jnp.int32, (ts, half), 1).astype(F32) / (half - 1.0)
    ang = pos * jnp.exp(-frac * math.log(RET_ROPE_BASE))
    cos_ref[...] = jnp.cos(ang)
    sin_ref[...] = jnp.sin(ang)


def _rope_tables(seq, half, ts=256):
    spec = pl.BlockSpec((ts, half), lambda i: (i, 0))
    return pl.pallas_call(
        _rope_table_kernel,
        grid=(seq // ts,),
        out_specs=[spec, spec],
        out_shape=[jax.ShapeDtypeStruct((seq, half), F32)] * 2,
        compiler_params=_params("arbitrary"),
        name="rope_tables",
    )()


def _ret_in_kernel(x_ref, mod_ref, w_ref, cos_ref, sin_ref, q_ref, k_ref, v_ref, gate_ref):
    d = x_ref.shape[1]
    nh = RET_HEADS
    dk = d // nh
    half = dk // 2
    h = _modulate(x_ref[...], mod_ref, 0, 1)
    cos_a = cos_ref[...]
    sin_a = sin_ref[...]
    for s, (out, scale) in enumerate(((q_ref, 1.0), (k_ref, dk ** -0.5))):
        t = _nn(h, w_ref[:, s * d:(s + 1) * d])
        for hh in range(nh):
            t1 = t[:, hh * dk:hh * dk + half]
            t2 = t[:, hh * dk + half:(hh + 1) * dk]
            out[:, hh * dk:hh * dk + half] = ((t1 * cos_a - t2 * sin_a) * scale).astype(BF16)
            out[:, hh * dk + half:(hh + 1) * dk] = ((t1 * sin_a + t2 * cos_a) * scale).astype(BF16)
    v_ref[...] = _nn(h, w_ref[:, 2 * d:4 * d]).astype(BF16)
    gate_ref[...] = _nn(h, w_ref[:, 4 * d:6 * d])


def _ret_in(x, mod, w_in, cos_t, sin_t, *, seq, tm=256):
    t, d = x.shape
    tps = seq // tm
    half = cos_t.shape[1]
    row = lambda i: (i, 0)
    return pl.pallas_call(
        _ret_in_kernel,
        grid=(t // tm,),
        in_specs=[
            pl.BlockSpec((tm, d), row),
            pl.BlockSpec((1, 6, d), lambda i: (i // tps, 0, 0)),
            pl.BlockSpec((d, 6 * d), lambda i: (0, 0)),
            pl.BlockSpec((tm, half), lambda i: (i % tps, 0)),
            pl.BlockSpec((tm, half), lambda i: (i % tps, 0)),
        ],
        out_specs=[pl.BlockSpec((tm, d), row), pl.BlockSpec((tm, d), row),
                   pl.BlockSpec((tm, 2 * d), row), pl.BlockSpec((tm, 2 * d), row)],
        out_shape=[jax.ShapeDtypeStruct((t, d), BF16), jax.ShapeDtypeStruct((t, d), BF16),
                   jax.ShapeDtypeStruct((t, 2 * d), BF16), jax.ShapeDtypeStruct((t, 2 * d), F32)],
        compiler_params=_params("arbitrary"),
        name="ret_in",
    )(x, mod, w_in.astype(BF16), cos_t, sin_t)


def _ret_core_kernel(q_ref, k_ref, v_ref, gate_ref, o_ref, state_ref, *, chunk):
    rows, d = q_ref.shape
    c = chunk
    nh = RET_HEADS
    dk = d // nh
    dv = v_ref.shape[1] // nh

    @pl.when(pl.program_id(1) == 0)
    def _():
        state_ref[...] = jnp.zeros(state_ref.shape, F32)

    rel = (lax.broadcasted_iota(jnp.int32, (c, c), 0)
           - lax.broadcasted_iota(jnp.int32, (c, c), 1)).astype(F32)
    idx = lax.broadcasted_iota(jnp.int32, (c, 1), 0).astype(F32)
    for hh in range(nh):
        log_gamma = math.log(1.0 - 2.0 ** (-5.0 - hh))
        dmask = jnp.where(rel >= 0, jnp.exp(jnp.maximum(rel, 0.0) * log_gamma), 0.0)
        zeta = jnp.exp((c - 1.0 - idx) * log_gamma)
        xi = jnp.exp((idx + 1.0) * log_gamma)
        state = state_ref[hh]
        for ci in range(rows // c):
            rs = slice(ci * c, (ci + 1) * c)
            qh = q_ref[rs, hh * dk:(hh + 1) * dk]
            kh = k_ref[rs, hh * dk:(hh + 1) * dk]
            vh = v_ref[rs, hh * dv:(hh + 1) * dv]
            scores = _nt(qh, kh) * dmask
            o = _nn(scores.astype(BF16), vh) + _nn(qh, state.astype(BF16)) * xi
            state = state * math.exp(c * log_gamma) + _tn((kh.astype(F32) * zeta).astype(BF16), vh)
            mu = jnp.mean(o, axis=-1, keepdims=True)
            oc = o - mu
            var = jnp.mean(oc * oc, axis=-1, keepdims=True)
            o = oc * lax.rsqrt(var + 1e-6)
            gate = gate_ref[rs, hh * dv:(hh + 1) * dv]
            o_ref[rs, hh * dv:(hh + 1) * dv] = (o * _silu(gate)).astype(BF16)
        state_ref[hh] = state


def _ret_core(q, k, v, gate, *, batch, seq, rows=256):
    t, d = q.shape
    c = rows
    n = seq // rows
    dk = d // RET_HEADS
    dv = v.shape[1] // RET_HEADS
    row = lambda b, j: (b * n + j, 0)
    return pl.pallas_call(
        functools.partial(_ret_core_kernel, chunk=RET_CHUNK),
        grid=(batch, n),
        in_specs=[pl.BlockSpec((c, d), row), pl.BlockSpec((c, d), row),
                  pl.BlockSpec((c, 2 * d), row), pl.BlockSpec((c, 2 * d), row)],
        out_specs=pl.BlockSpec((c, 2 * d), row),
        out_shape=jax.ShapeDtypeStruct((t, 2 * d), BF16),
        scratch_shapes=[pltpu.VMEM((RET_HEADS, dk, dv), F32)],
        compiler_params=_params("arbitrary", "arbitrary"),
        name="ret_core",
    )(q, k, v, gate)


def _gelu(t):
    p = 0.3275911 / math.sqrt(2.0)
    half_coefs = [0.5 * a for a in (0.254829592, -0.284496736, 1.421413741, -1.453152027, 1.061405429)]
    u = 1.0 / (1.0 + p * jnp.abs(t))
    poly = half_coefs[4]
    for coef in half_coefs[3::-1]:
        poly = poly * u + coef
    e = poly * u * jnp.exp2(t * t * (-0.5 * math.log2(math.e)))
    return t * jnp.where(t >= 0.0, 1.0 - e, e)


def _gmlp_kernel(x_ref, mod_ref, win_ref, lng_ref, lnb_ref, ws_ref, bs_ref, out_ref):
    tm, d = x_ref.shape
    width = win_ref.shape[1] // 2
    c = GMLP_CHUNK
    ng = GMLP_GROUPS
    gw = width // ng
    h = _modulate(x_ref[...], mod_ref, 0, 1)

    v = _gelu(_nn(h, win_ref[:, width:]))
    v = _layer_norm_rows(v, lng_ref[...], lnb_ref[...], LN_EPS).astype(BF16)
    u = _gelu(_nn(h, win_ref[:, :width]))
    row = lax.broadcasted_iota(jnp.int32, (c, c), 0)
    col = lax.broadcasted_iota(jnp.int32, (c, c), 1)
    for n in range(tm // c):
        rs = slice(n * c, (n + 1) * c)
        for gi in range(ng):
            ls = slice(gi * gw, (gi + 1) * gw)
            ws = jnp.where(row >= col, ws_ref[gi], 0.0).astype(BF16)
            vs = _nn(ws, v[rs, ls]) + bs_ref[:, gi:gi + 1]
            out_ref[rs, ls] = (u[rs, ls] * vs).astype(BF16)


def _gmlp(x, mod, w_in, ln_g, ln_b, w_s, b_s, *, seq, tm=256):
    t, d = x.shape
    width = w_in.shape[1] // 2
    ng, c, _ = w_s.shape
    tps = seq // tm
    return pl.pallas_call(
        _gmlp_kernel,
        grid=(t // tm,),
        in_specs=[
            pl.BlockSpec((tm, d), lambda i: (i, 0)),
            pl.BlockSpec((1, 6, d), lambda i: (i // tps, 0, 0)),
            _resident((d, 2 * width)),
            _resident((1, width)),
            _resident((1, width)),
            _resident((ng, c, c)),
            _resident((c, ng)),
        ],
        out_specs=pl.BlockSpec((tm, width), lambda i: (i, 0)),
        out_shape=jax.ShapeDtypeStruct((t, width), BF16),
        compiler_params=_params("arbitrary"),
        name="gmlp",
    )(x, mod, w_in.astype(BF16), ln_g.reshape(1, width), ln_b.reshape(1, width), w_s, b_s.T)


def _sb_in_kernel(x_ref, mod_ref, w_ref, q_ref, k_ref, v_ref, *, q_scale):
    d = x_ref.shape[1]
    h = _modulate(x_ref[...], mod_ref, 0, 1)
    q_ref[...] = (_nn(h, w_ref[:, 0:d]) * q_scale).astype(BF16)
    k_ref[...] = _nn(h, w_ref[:, d:2 * d]).astype(BF16)
    v_ref[...] = _nn(h, w_ref[:, 2 * d:3 * d]).astype(BF16)


def _sb_in(x, mod, w_in, *, seq, tm=512):
    t, d = x.shape
    tps = seq // tm
    row = lambda i: (i, 0)
    return pl.pallas_call(
        functools.partial(_sb_in_kernel, q_scale=(d // SB_HEADS) ** -0.5 * math.log2(math.e)),
        grid=(t // tm,),
        in_specs=[
            pl.BlockSpec((tm, d), row),
            pl.BlockSpec((1, 6, d), lambda i: (i // tps, 0, 0)),
            pl.BlockSpec((d, 3 * d), lambda i: (0, 0)),
        ],
        out_specs=[pl.BlockSpec((tm, d), row)] * 3,
        out_shape=[jax.ShapeDtypeStruct((t, d), BF16)] * 3,
        compiler_params=_params("arbitrary"),
        name="sb_in",
    )(x, mod, w_in.astype(BF16))


def _sb_core_kernel(q_ref, k_ref, v_ref, o_ref, qh_ref, sp_ref, zc_ref, carry_ref, acc_ref, knorm_ref,
                    bound_ref):
    tq, width = q_ref.shape
    seq = k_ref.shape[0]
    n_heads = qh_ref.shape[0]
    dh = LANES // 2
    qi = pl.program_id(2)
    lane = lax.broadcasted_iota(jnp.int32, (tq, LANES), 1)
    half_lanes = (lane < dh, lane >= dh)
    group = [slice((hh // 2) * LANES, (hh // 2 + 1) * LANES) for hh in range(n_heads)]
    row = lax.broadcasted_iota(jnp.int32, (tq, tq), 0)
    col = lax.broadcasted_iota(jnp.int32, (tq, tq), 1)
    ones_lower = jnp.where(row >= col, 1.0, 0.0).astype(BF16)

    @pl.when(qi == 0)
    def _():
        def key_tile(t, best):
            kf = k_ref[pl.ds(pl.multiple_of(t * tq, tq), tq), :].astype(F32)
            return jnp.maximum(best, jnp.max(kf * kf, axis=0, keepdims=True))

        col_max = lax.fori_loop(0, seq // tq, key_tile, jnp.zeros((1, width), F32))
        for hh in range(n_heads):
            bound_sq = jnp.sum(jnp.where(half_lanes[hh % 2][0:1, :], col_max[:, group[hh]], 0.0),
                               axis=-1, keepdims=True)
            knorm_ref[hh] = jnp.broadcast_to(bound_sq, knorm_ref.shape[1:])

    for hh in range(n_heads):
        q = q_ref[:, group[hh]]
        qf = q.astype(F32)
        qh_ref[hh] = jnp.where(half_lanes[hh % 2], q, jnp.zeros_like(q))
        qsq = jnp.sum(jnp.where(half_lanes[hh % 2], qf * qf, 0.0), axis=-1, keepdims=True)
        bound_ref[hh] = jnp.sqrt(qsq * knorm_ref[hh, 0:1, 0:1]) * NORM_BOUND_SLACK
    acc_ref[...] = jnp.zeros(acc_ref.shape, F32)
    carry_ref[...] = jnp.zeros(carry_ref.shape, F32)

    def stage(kt, slot, diagonal=False):
        rows = pl.ds(pl.multiple_of(kt * tq, tq), tq)
        for hh in range(n_heads):
            z = _nt(qh_ref[hh], k_ref[rows, group[hh]])
            sp = jnp.maximum(z, jnp.log2(1.0 + jnp.exp2(jnp.minimum(z, EXP2_CLAMP))))
            carry = carry_ref[hh]
            zc = z - carry
            if diagonal:
                sp = jnp.where(col < row, sp, 0.0)
                zc = jnp.where(col < row, zc, -1e30)
            sp_ref[slot, hh] = sp.astype(BF16)
            zc_ref[slot, hh] = zc
            carry_ref[hh] = carry + jnp.sum(sp, axis=-1, keepdims=True)

    def consume(kt, slot):
        rows = pl.ds(pl.multiple_of(kt * tq, tq), tq)
        for hh in range(n_heads):
            inclusive = _nn(sp_ref[slot, hh], ones_lower)
            a = jnp.exp2(zc_ref[slot, hh] - inclusive)
            acc_ref[hh] += _nn(a.astype(BF16), v_ref[rows, group[hh]])

    def still_alive():
        slack = bound_ref[0] - carry_ref[0]
        for hh in range(1, n_heads):
            slack = jnp.maximum(slack, bound_ref[hh] - carry_ref[hh])
        return (jnp.max(slack) > -DEAD_EXPONENT).astype(jnp.int32)

    stage(qi, 0, diagonal=True)

    @pl.when(qi == 0)
    def _():
        consume(0, 0)

    @pl.when(qi > 0)
    def _():
        stage(qi - 1, 1)
        alive = still_alive()
        consume(qi, 0)
        n_pairs = (qi - 1) // 2

        def more(state):
            return (state[0] < n_pairs) & (state[1] > 0)

        def pair(state):
            kt = qi - 1 - 2 * state[0]
            stage(kt - 1, 0)
            consume(kt, 1)
            stage(kt - 2, 1)
            alive = still_alive()
            consume(kt - 1, 0)
            return state[0] + 1, alive

        pairs_done, alive = lax.while_loop(more, pair, (jnp.int32(0), alive))
        staged = qi - 1 - 2 * pairs_done

        @pl.when((alive == 0) | (staged == 0))
        def _():
            consume(staged, 1)

        @pl.when((alive > 0) & (staged == 1))
        def _():
            stage(0, 0)
            consume(1, 1)
            consume(0, 0)

    for hh in range(0, n_heads, 2):
        o_ref[:, group[hh]] = jnp.where(half_lanes[0], acc_ref[hh], acc_ref[hh + 1]).astype(BF16)


def _sb_core(q, k, v, *, batch, seq, tq=256, heads_per_step=8):
    t, d = q.shape
    tq = min(tq, seq)
    nq = seq // tq
    nh = heads_per_step
    width = nh // 2 * LANES
    return pl.pallas_call(
        _sb_core_kernel,
        grid=(batch, d // width, nq),
        in_specs=[
            pl.BlockSpec((tq, width), lambda b, g, i: (b * nq + i, g)),
            pl.BlockSpec((seq, width), lambda b, g, i: (b, g)),
            pl.BlockSpec((seq, width), lambda b, g, i: (b, g)),
        ],
        out_specs=pl.BlockSpec((tq, width), lambda b, g, i: (b * nq + i, g)),
        out_shape=jax.ShapeDtypeStruct((t, d), BF16),
        scratch_shapes=[
            pltpu.VMEM((nh, tq, LANES), BF16),
            pltpu.VMEM((2, nh, tq, tq), BF16),
            pltpu.VMEM((2, nh, tq, tq), F32),
            pltpu.VMEM((nh, tq, 1), F32),
            pltpu.VMEM((nh, tq, LANES), F32),
            pltpu.VMEM((nh, SUBLANES, LANES), F32),
            pltpu.VMEM((nh, tq, 1), F32),
        ],
        compiler_params=_params("arbitrary", "arbitrary", "arbitrary"),
        name="sb_core",
    )(q, k, v)


def kernel(x, c, cond_w, cond_b, ada_w, ada_b, ln_g, ln_b, ffn_up, ffn_conv_w, ffn_conv_b, ffn_down,
           gdn_w_in, gdn_conv_w, gdn_a_log, gdn_dt_bias, gdn_norm_w, gdn_w_out,
           ret_w_in, ret_w_out,
           gmlp_w_in, gmlp_ln_g, gmlp_ln_b, gmlp_w_s, gmlp_b_s, gmlp_w_out,
           sb_w_in, sb_w_out):
    batch, seq, d = x.shape
    depth = ada_w.shape[0]
    alpha = (2.0 * depth) ** 0.25
    mods = _conditioning(c, cond_w, cond_b, ada_w, ada_b).reshape(depth, batch, 6, d)
    xt = x.reshape(batch * seq, d)
    ffn_up_b = ffn_up.astype(BF16)
    ffn_down_b = ffn_down.astype(BF16)

    for i in range(depth):
        mod = mods[i]
        mixer = i % 4
        if mixer == 0:
            o = _gdn_mixer(xt, mod, gdn_w_in, gdn_conv_w, gdn_a_log, gdn_dt_bias, gdn_norm_w, seq=seq)
            w_out = gdn_w_out
        elif mixer == 1:
            cos_t, sin_t = _rope_tables(seq, d // RET_HEADS // 2)
            q, k, v, gate = _ret_in(xt, mod, ret_w_in, cos_t, sin_t, seq=seq)
            o, w_out = _ret_core(q, k, v, gate, batch=batch, seq=seq), ret_w_out
        elif mixer == 2:
            o = _gmlp(xt, mod, gmlp_w_in, gmlp_ln_g, gmlp_ln_b, gmlp_w_s, gmlp_b_s, seq=seq)
            w_out = gmlp_w_out
        else:
            q, k, v = _sb_in(xt, mod, sb_w_in, seq=seq)
            o, w_out = _sb_core(q, k, v, batch=batch, seq=seq), sb_w_out
        xt = _sublayer_tail(o, w_out, xt, mod, ffn_up_b, ffn_conv_w[i], ffn_conv_b[i], ffn_down_b,
                            ln_g, ln_b, layer=i, seq=seq, alpha=alpha)
    return xt.reshape(batch, seq, d)
```

```python
import functools
import math

import jax
import jax.numpy as jnp
from jax import lax
from jax.experimental import pallas as pl
from jax.experimental.pallas import tpu as pltpu

F32 = jnp.float32
BF16 = jnp.bfloat16

LANES = 128
SUBLANES = 8
VMEM_LIMIT = 56 * 1024 * 1024

LN_EPS = 1e-5
GDN_HEADS = 8
GDN_CHUNK = 64
GDN_CONV = 4
RET_HEADS = 4
RET_CHUNK = 128
RET_ROPE_BASE = 10000.0
GMLP_CHUNK = 128
GMLP_GROUPS = 8
SB_HEADS = 16
FFN_CONV = 3
DEAD_EXPONENT = 160.0
NORM_BOUND_SLACK = 1.001
EXP2_CLAMP = 64.0


def _params(*sem):
    return pltpu.CompilerParams(dimension_semantics=sem, vmem_limit_bytes=VMEM_LIMIT)


def _nn(a, b):
    return jnp.dot(a, b, preferred_element_type=F32)


def _nt(a, b):
    return lax.dot_general(a, b, (((1,), (1,)), ((), ())), preferred_element_type=F32)


def _tn(a, b):
    return lax.dot_general(a, b, (((0,), (0,)), ((), ())), preferred_element_type=F32)


def _sigmoid(x):
    return 1.0 / (1.0 + jnp.exp(-x))


def _silu(x):
    return x * _sigmoid(x)


def _softplus(x):
    return jnp.maximum(x, 0.0) + jnp.log(1.0 + jnp.exp(-jnp.abs(x)))


def _split3(x):
    hi = x.astype(BF16)
    r = x - hi.astype(F32)
    mid = r.astype(BF16)
    lo = (r - mid.astype(F32)).astype(BF16)
    return hi, mid, lo


def _modulate(x, mod_ref, shift_row, scale_row):
    return (x * (1.0 + mod_ref[0, scale_row:scale_row + 1, :])
            + mod_ref[0, shift_row:shift_row + 1, :]).astype(BF16)


def _layer_norm_rows(r, g, b, eps):
    mu = jnp.mean(r, axis=-1, keepdims=True)
    rc = r - mu
    var = jnp.mean(rc * rc, axis=-1, keepdims=True)
    return rc * lax.rsqrt(var + eps) * g + b


def _lane_replicated_columns(rows):
    n, k = rows.shape
    padded = jnp.concatenate([rows, jnp.zeros((LANES - n, k), F32)], axis=0)
    t = padded.T
    return [jnp.broadcast_to(t[:, b:b + 1], (k, LANES)) for b in range(n)]


def _rowvec_matmul(col, w):
    n = w.shape[1]
    parts = [jnp.sum(col * w[:, c:c + LANES], axis=0, keepdims=True) for c in range(0, n, LANES)]
    return jnp.concatenate(parts, axis=1)


def _cond_kernel(c_ref, cw_ref, cb_ref, aw_ref, ab_ref, o_ref, ecol_ref):
    nb = c_ref.shape[0]

    @pl.when((pl.program_id(0) == 0) & (pl.program_id(1) == 0))
    def _():
        ccols = _lane_replicated_columns(c_ref[...])
        cw = cw_ref[...]
        e = jnp.concatenate([_rowvec_matmul(col, cw) for col in ccols], axis=0) + cb_ref[...]
        ecols = _lane_replicated_columns(_silu(e))
        for b in range(nb):
            ecol_ref[b] = ecols[b]

    w = aw_ref[0]
    for b in range(nb):
        o_ref[0, b:b + 1, :] = _rowvec_matmul(ecol_ref[b], w) + ab_ref[0]


def _conditioning(c, cond_w, cond_b, ada_w, ada_b):
    nb, d = c.shape
    depth, _, n6 = ada_w.shape
    tn = 1536
    return pl.pallas_call(
        _cond_kernel,
        grid=(depth, n6 // tn),
        in_specs=[
            pl.BlockSpec((nb, d), lambda i, j: (0, 0)),
            pl.BlockSpec((d, d), lambda i, j: (0, 0)),
            pl.BlockSpec((1, d), lambda i, j: (0, 0)),
            pl.BlockSpec((1, d, tn), lambda i, j: (i, 0, j)),
            pl.BlockSpec((1, 1, tn), lambda i, j: (i, 0, j)),
        ],
        out_specs=pl.BlockSpec((1, nb, tn), lambda i, j: (i, 0, j)),
        out_shape=jax.ShapeDtypeStruct((depth, nb, n6), F32),
        scratch_shapes=[pltpu.VMEM((nb, d, LANES), F32)],
        compiler_params=_params("arbitrary", "arbitrary"),
        name="conditioning",
    )(c, cond_w, cond_b.reshape(1, d), ada_w, ada_b.reshape(depth, 1, n6))


def _ffn_kernel(o_ref, wo_ref, g1_ref, b1_ref, x_ref, mod_ref, wup_ref, cw_ref, cb_ref, wd_ref,
                g_ref, b_ref, out_ref, x1_ref, h_ref, buf_ref, act_ref, *, tiles_per_seq, alpha):
    i = pl.program_id(0)
    tm = x_ref.shape[0]
    f = wd_ref.shape[0]
    n_chunks, _, tf = buf_ref.shape
    half = tm // 2
    halves = (slice(0, half), slice(half, tm))
    for rs in halves:
        mixed = alpha * x_ref[rs, :] + (1.0 + mod_ref[0, 2:3, :]) * _nn(o_ref[rs, :], wo_ref[...])
        x1 = _layer_norm_rows(mixed, g1_ref[...], b1_ref[...], LN_EPS)
        x1_ref[rs, :] = x1
        h_ref[rs, :] = _modulate(x1, mod_ref, 3, 4)

    @pl.when(i % tiles_per_seq == 0)
    def _():
        for c in range(n_chunks):
            buf_ref[c, 0:SUBLANES, :] = jnp.zeros((SUBLANES, tf), F32)

    for c in range(n_chunks):
        cols = slice(c * tf, (c + 1) * tf)
        buf_ref[c, SUBLANES:SUBLANES + tm, :] = _nn(h_ref[...], wup_ref[:, cols])
        conv = cb_ref[:, cols]
        for tap in range(FFN_CONV):
            off = SUBLANES - (FFN_CONV - 1) + tap
            conv = conv + cw_ref[tap:tap + 1, cols] * buf_ref[c, off:off + tm, :]
        buf_ref[c, 0:SUBLANES, :] = buf_ref[c, tm:tm + SUBLANES, :]
        up = _nn(h_ref[...], wup_ref[:, f + c * tf:f + (c + 1) * tf])
        act_ref[:, cols] = (_silu(conv) * up).astype(BF16)

    for rs in halves:
        y = _nn(act_ref[rs, :], wd_ref[...])
        r = alpha * x1_ref[rs, :] + (1.0 + mod_ref[0, 5:6, :]) * y
        out_ref[rs, :] = _layer_norm_rows(r, g_ref[...], b_ref[...], LN_EPS)


def _resident(shape, layer=None):
    if layer is None:
        return pl.BlockSpec(shape, lambda *_: (0,) * len(shape), pipeline_mode=pl.Buffered(1))
    return pl.BlockSpec((None,) + tuple(shape), lambda *_: (layer,) + (0,) * len(shape),
                        pipeline_mode=pl.Buffered(1))


def _sublayer_tail(o, w_out, x, mod, w_up, conv_w, conv_b, w_down, ln_g, ln_b, *, layer, seq, alpha,
                   tm=512, n_chunks=2):
    t, d = x.shape
    kdim = o.shape[1]
    f = w_down.shape[1]
    tf = f // n_chunks
    tps = seq // tm
    rows = lambda i: (i, 0)
    return pl.pallas_call(
        functools.partial(_ffn_kernel, tiles_per_seq=tps, alpha=alpha),
        grid=(t // tm,),
        in_specs=[
            pl.BlockSpec((tm, kdim), rows),
            _resident((kdim, d)),
            _resident((1, d)),
            _resident((1, d)),
            pl.BlockSpec((tm, d), rows),
            pl.BlockSpec((1, 6, d), lambda i: (i // tps, 0, 0)),
            _resident((d, 2 * f), layer),
            _resident((FFN_CONV, f)),
            _resident((1, f)),
            _resident((f, d), layer),
            _resident((1, d)),
            _resident((1, d)),
        ],
        out_specs=pl.BlockSpec((tm, d), rows),
        out_shape=jax.ShapeDtypeStruct((t, d), F32),
        scratch_shapes=[
            pltpu.VMEM((tm, d), F32),
            pltpu.VMEM((tm, d), BF16),
            pltpu.VMEM((n_chunks, tm + SUBLANES, tf), F32),
            pltpu.VMEM((tm, f), BF16),
        ],
        compiler_params=_params("arbitrary"),
        name="sublayer_tail",
    )(o, w_out.astype(BF16), ln_g[layer, 0].reshape(1, d), ln_b[layer, 0].reshape(1, d), x, mod,
      w_up, conv_w, conv_b.reshape(1, f), w_down, ln_g[layer, 1].reshape(1, d), ln_b[layer, 1].reshape(1, d))


def _gdn_in_kernel(x_ref, mod_ref, wqkv_ref, wz_ref, wab_ref, cw_ref, alog_ref, dtb_ref,
                   q_ref, k_ref, v_ref, z_ref, gb_ref, gt_ref, buf_ref, *, tiles_per_seq, chunk):
    i = pl.program_id(0)
    tm, d = x_ref.shape
    nh = GDN_HEADS
    hd = d // nh
    h = _modulate(x_ref[...], mod_ref, 0, 1)

    @pl.when(i % tiles_per_seq == 0)
    def _():
        buf_ref[0:SUBLANES, :] = jnp.zeros((SUBLANES, buf_ref.shape[1]), F32)

    outs = (q_ref, k_ref, v_ref)
    for s in range(3):
        cs = slice(s * d, (s + 1) * d)
        buf_ref[SUBLANES:SUBLANES + tm, cs] = _nn(h, wqkv_ref[:, cs])
        y = None
        for tap in range(GDN_CONV):
            off = SUBLANES - (GDN_CONV - 1) + tap
            term = cw_ref[tap:tap + 1, cs] * buf_ref[off:off + tm, cs]
            y = term if y is None else y + term
        y = _silu(y)
        if s < 2:
            scale = hd ** -0.5 if s == 0 else 1.0
            for hh in range(nh):
                seg = y[:, hh * hd:(hh + 1) * hd]
                inv = lax.rsqrt(jnp.sum(seg * seg, axis=-1, keepdims=True) + 1e-6)
                outs[s][:, hh * hd:(hh + 1) * hd] = (seg * (inv * scale)).astype(BF16)
        else:
            outs[s][...] = y.astype(BF16)
    buf_ref[0:SUBLANES, :] = buf_ref[tm:tm + SUBLANES, :]

    z_ref[...] = _nn(h, wz_ref[...]).astype(BF16)

    pab = _nn(h, wab_ref[...])
    lane = lax.broadcasted_iota(jnp.int32, pab.shape, 1)
    g = -jnp.exp(alog_ref[...]) * _softplus(pab + dtb_ref[...])
    gb = jnp.where(lane < nh, g, jnp.where(lane < 2 * nh, _sigmoid(pab), 0.0))
    row = lax.broadcasted_iota(jnp.int32, (tm, tm), 0)
    col = lax.broadcasted_iota(jnp.int32, (tm, tm), 1)
    tri = jnp.where(col <= row, jnp.where(jnp.bitwise_xor(row, col) < chunk, 1.0, 0.0), 0.0).astype(BF16)
    hi, mid, lo = _split3(gb)
    cum = _nn(tri, hi) + _nn(tri, mid) + _nn(tri, lo)
    gb = jnp.where(lane < nh, cum, gb)
    gb_ref[...] = gb
    gt_ref[...] = gb.T[0:2 * nh, :]


def _gdn_in(x, mod, w_in, conv_w, a_log, dt_bias, *, seq, tm=256):
    t, d = x.shape
    nh = GDN_HEADS
    tps = seq // tm
    w_b = w_in.astype(BF16)
    w_ab = jnp.pad(w_b[:, 4 * d:], ((0, 0), (0, LANES - 2 * nh)))
    once = pl.Buffered(1)
    alog = jnp.pad(a_log, (0, LANES - nh)).reshape(1, LANES)
    dtb = jnp.pad(dt_bias, (0, LANES - nh)).reshape(1, LANES)
    row = lambda i: (i, 0)
    fixed = lambda i: (0, 0)
    return pl.pallas_call(
        functools.partial(_gdn_in_kernel, tiles_per_seq=tps, chunk=GDN_CHUNK),
        grid=(t // tm,),
        in_specs=[
            pl.BlockSpec((tm, d), row),
            pl.BlockSpec((1, 6, d), lambda i: (i // tps, 0, 0)),
            pl.BlockSpec((d, 3 * d), fixed, pipeline_mode=once),
            pl.BlockSpec((d, d), lambda i: (0, 3), pipeline_mode=once),
            pl.BlockSpec((d, LANES), fixed, pipeline_mode=once),
            pl.BlockSpec((GDN_CONV, 3 * d), fixed, pipeline_mode=once),
            pl.BlockSpec((1, LANES), fixed),
            pl.BlockSpec((1, LANES), fixed),
        ],
        out_specs=[pl.BlockSpec((tm, d), row)] * 4 + [pl.BlockSpec((tm, LANES), row),
                                                      pl.BlockSpec((2 * nh, tm), lambda i: (0, i))],
        out_shape=[jax.ShapeDtypeStruct((t, d), BF16)] * 4 + [jax.ShapeDtypeStruct((t, LANES), F32),
                                                              jax.ShapeDtypeStruct((2 * nh, t), F32)],
        scratch_shapes=[pltpu.VMEM((tm + SUBLANES, 3 * d), F32)],
        compiler_params=_params("arbitrary"),
        name="gdn_in",
    )(x, mod, w_b, w_b, w_ab, conv_w, alog, dtb)


def _gdn_core_kernel(q_ref, k_ref, v_ref, z_ref, gb_ref, gt_ref, nw_ref, o_ref,
                     state_ref, inv_ref, pw_ref, rhs_ref, u_ref, w_ref, qk_ref, qg_ref, kd_ref, dl_ref,
                     vn_ref, oi_ref, *, chunk):
    rows, d = q_ref.shape
    nh = GDN_HEADS
    hd = d // nh
    n_chunks = rows // chunk
    heads = range(nh)
    lanes = [slice(hh * hd, (hh + 1) * hd) for hh in heads]

    @pl.when(pl.program_id(1) == 0)
    def _():
        state_ref[...] = jnp.zeros(state_ref.shape, F32)

    row = lax.broadcasted_iota(jnp.int32, (rows, rows), 0)
    col = lax.broadcasted_iota(jnp.int32, (rows, rows), 1)
    same_chunk = jnp.bitwise_xor(row, col) < chunk
    gb = gb_ref[...]
    for hh in heads:
        kh = k_ref[:, lanes[hh]]
        kf = kh.astype(F32)
        qh = q_ref[:, lanes[hh]]
        gc = jnp.broadcast_to(gb[:, hh:hh + 1], (rows, hd))
        beta = jnp.broadcast_to(gb[:, nh + hh:nh + hh + 1], (rows, hd))
        diff = jnp.broadcast_to(gb[:, hh:hh + 1], (rows, rows)) - gt_ref[hh:hh + 1, :]
        causal = same_chunk & (row >= col)
        decay = jnp.where(causal, jnp.exp(jnp.where(causal, diff, 0.0)), 0.0)
        kb = kf * beta
        a = jnp.where(row > col, _nt(kb.astype(BF16), kh) * decay, 0.0)
        inv_ref[hh] = jnp.where(row == col, 1.0, 0.0) - a
        pw_ref[hh] = a.astype(BF16)
        qk_ref[hh] = (_nt(qh, kh) * decay).astype(BF16)
        rhs_ref[hh, :, 0:hd] = (v_ref[:, lanes[hh]].astype(F32) * beta).astype(BF16)
        rhs_ref[hh, :, hd:2 * hd] = (kb * jnp.exp(gc)).astype(BF16)
        qg_ref[hh] = (qh.astype(F32) * jnp.exp(gc)).astype(BF16)
        for c in range(n_chunks):
            rs = slice(c * chunk, (c + 1) * chunk)
            g_last = gc[(c + 1) * chunk - 1:(c + 1) * chunk, :]
            kd_ref[hh, rs, :] = (kf[rs] * jnp.exp(g_last - gc[rs])).astype(BF16)
            dl_ref[hh, c] = jnp.exp(g_last)

    for _ in range(int(math.log2(chunk)) - 1):
        for hh in heads:
            power = _nn(pw_ref[hh], pw_ref[hh]).astype(BF16)
            pw_ref[hh] = power
            inv = inv_ref[hh]
            inv_ref[hh] = inv + _nn(inv.astype(BF16), power)
    for hh in heads:
        sol = _nn(inv_ref[hh].astype(BF16), rhs_ref[hh])
        u_ref[hh] = sol[:, 0:hd]
        w_ref[hh] = sol[:, hd:2 * hd].astype(BF16)

    for c in range(n_chunks):
        rs = slice(c * chunk, (c + 1) * chunk)
        states = [state_ref[hh].astype(BF16) for hh in heads]
        v_new = [(u_ref[hh, rs, :] - _nn(w_ref[hh, rs, :], states[hh])).astype(BF16) for hh in heads]
        for hh in heads:
            vn_ref[hh, rs, :] = v_new[hh]
            state_ref[hh] = state_ref[hh] * dl_ref[hh, c] + _tn(kd_ref[hh, rs, :], v_new[hh])
            oi_ref[hh, rs, :] = _nn(qg_ref[hh, rs, :], states[hh])

    for hh in heads:
        o = oi_ref[hh] + _nn(qk_ref[hh], vn_ref[hh])
        o = o * lax.rsqrt(jnp.mean(o * o, axis=-1, keepdims=True) + 1e-6) * nw_ref[...]
        o_ref[:, lanes[hh]] = (o * _silu(z_ref[:, lanes[hh]].astype(F32))).astype(BF16)


def _gdn_core(q, k, v, z, gb, gt, norm_w, *, batch, seq, rows=256):
    t, d = q.shape
    n = seq // rows
    nh = GDN_HEADS
    hd = d // nh
    row = lambda b, j: (b * n + j, 0)
    return pl.pallas_call(
        functools.partial(_gdn_core_kernel, chunk=GDN_CHUNK),
        grid=(batch, n),
        in_specs=[pl.BlockSpec((rows, d), row)] * 4 + [
            pl.BlockSpec((rows, LANES), row),
            pl.BlockSpec((2 * nh, rows), lambda b, j: (0, b * n + j)),
            pl.BlockSpec((1, hd), lambda b, j: (0, 0)),
        ],
        out_specs=pl.BlockSpec((rows, d), row),
        out_shape=jax.ShapeDtypeStruct((t, d), BF16),
        scratch_shapes=[
            pltpu.VMEM((nh, hd, hd), F32),
            pltpu.VMEM((nh, rows, rows), F32),
            pltpu.VMEM((nh, rows, rows), BF16),
            pltpu.VMEM((nh, rows, 2 * hd), BF16),
            pltpu.VMEM((nh, rows, hd), F32),
            pltpu.VMEM((nh, rows, hd), BF16),
            pltpu.VMEM((nh, rows, rows), BF16),
            pltpu.VMEM((nh, rows, hd), BF16),
            pltpu.VMEM((nh, rows, hd), BF16),
            pltpu.VMEM((nh, rows // GDN_CHUNK, 1, hd), F32),
            pltpu.VMEM((nh, rows, hd), BF16),
            pltpu.VMEM((nh, rows, hd), F32),
        ],
        compiler_params=_params("arbitrary", "arbitrary"),
        name="gdn_core",
    )(q, k, v, z, gb, gt, norm_w.reshape(1, hd))


def _rope_table_kernel(cos_ref, sin_ref):
    ts, half = cos_ref.shape
    pos = (lax.broadcasted_iota(jnp.int32, (ts, half), 0) + pl.program_id(0) * ts).astype(F32)
    frac = lax.broadcasted_iota(jnp.int32, (ts, half), 1).astype(F32) / (half - 1.0)
    ang = pos * jnp.exp(-frac * math.log(RET_ROPE_BASE))
    cos_ref[...] = jnp.cos(ang)
    sin_ref[...] = jnp.sin(ang)


def _rope_tables(seq, half, ts=256):
    spec = pl.BlockSpec((ts, half), lambda i: (i, 0))
    return pl.pallas_call(
        _rope_table_kernel,
        grid=(seq // ts,),
        out_specs=[spec, spec],
        out_shape=[jax.ShapeDtypeStruct((seq, half), F32)] * 2,
        compiler_params=_params("arbitrary"),
        name="rope_tables",
    )()


def _ret_in_kernel(x_ref, mod_ref, w_ref, cos_ref, sin_ref, q_ref, k_ref, v_ref, gate_ref):
    d = x_ref.shape[1]
    nh = RET_HEADS
    dk = d // nh
    half = dk // 2
    h = _modulate(x_ref[...], mod_ref, 0, 1)
    cos_a = cos_ref[...]
    sin_a = sin_ref[...]
    for s, (out, scale) in enumerate(((q_ref, 1.0), (k_ref, dk ** -0.5))):
        t = _nn(h, w_ref[:, s * d:(s + 1) * d])
        for hh in range(nh):
            t1 = t[:, hh * dk:hh * dk + half]
            t2 = t[:, hh * dk + half:(hh + 1) * dk]
            out[:, hh * dk:hh * dk + half] = ((t1 * cos_a - t2 * sin_a) * scale).astype(BF16)
            out[:, hh * dk + half:(hh + 1) * dk] = ((t1 * sin_a + t2 * cos_a) * scale).astype(BF16)
    v_ref[...] = _nn(h, w_ref[:, 2 * d:4 * d]).astype(BF16)
    gate_ref[...] = _silu(_nn(h, w_ref[:, 4 * d:6 * d])).astype(BF16)


def _ret_in(x, mod, w_in, cos_t, sin_t, *, seq, tm=512):
    t, d = x.shape
    tps = seq // tm
    half = cos_t.shape[1]
    row = lambda i: (i, 0)
    return pl.pallas_call(
        _ret_in_kernel,
        grid=(t // tm,),
        in_specs=[
            pl.BlockSpec((tm, d), row),
            pl.BlockSpec((1, 6, d), lambda i: (i // tps, 0, 0)),
            _resident((d, 6 * d)),
            pl.BlockSpec((tm, half), lambda i: (i % tps, 0)),
            pl.BlockSpec((tm, half), lambda i: (i % tps, 0)),
        ],
        out_specs=[pl.BlockSpec((tm, d), row), pl.BlockSpec((tm, d), row),
                   pl.BlockSpec((tm, 2 * d), row), pl.BlockSpec((tm, 2 * d), row)],
        out_shape=[jax.ShapeDtypeStruct((t, d), BF16), jax.ShapeDtypeStruct((t, d), BF16),
                   jax.ShapeDtypeStruct((t, 2 * d), BF16), jax.ShapeDtypeStruct((t, 2 * d), BF16)],
        compiler_params=_params("arbitrary"),
        name="ret_in",
    )(x, mod, w_in.astype(BF16), cos_t, sin_t)


def _ret_core_kernel(q_ref, k_ref, v_ref, gate_ref, o_ref, state_ref, *, chunk):
    rows, d = q_ref.shape
    c = chunk
    nh = RET_HEADS
    dk = d // nh
    dv = v_ref.shape[1] // nh

    @pl.when(pl.program_id(1) == 0)
    def _():
        state_ref[...] = jnp.zeros(state_ref.shape, F32)

    rel = (lax.broadcasted_iota(jnp.int32, (c, c), 0)
           - lax.broadcasted_iota(jnp.int32, (c, c), 1)).astype(F32)
    idx = lax.broadcasted_iota(jnp.int32, (c, 1), 0).astype(F32)
    for hh in range(nh):
        log_gamma = math.log(1.0 - 2.0 ** (-5.0 - hh))
        dmask = jnp.where(rel >= 0, jnp.exp(jnp.maximum(rel, 0.0) * log_gamma), 0.0)
        zeta = jnp.exp((c - 1.0 - idx) * log_gamma)
        xi = jnp.exp((idx + 1.0) * log_gamma)
        state = state_ref[hh]
        for ci in range(rows // c):
            rs = slice(ci * c, (ci + 1) * c)
            qh = q_ref[rs, hh * dk:(hh + 1) * dk]
            kh = k_ref[rs, hh * dk:(hh + 1) * dk]
            vh = v_ref[rs, hh * dv:(hh + 1) * dv]
            scores = _nt(qh, kh) * dmask
            o = _nn(scores.astype(BF16), vh) + _nn(qh, state.astype(BF16)) * xi
            state = state * math.exp(c * log_gamma) + _tn((kh.astype(F32) * zeta).astype(BF16), vh)
            mu = jnp.mean(o, axis=-1, keepdims=True)
            oc = o - mu
            var = jnp.mean(oc * oc, axis=-1, keepdims=True)
            o = oc * lax.rsqrt(var + 1e-6)
            gate = gate_ref[rs, hh * dv:(hh + 1) * dv].astype(F32)
            o_ref[rs, hh * dv:(hh + 1) * dv] = (o * gate).astype(BF16)
        state_ref[hh] = state


def _ret_core(q, k, v, gate, *, batch, seq, rows=512):
    t, d = q.shape
    c = rows
    n = seq // rows
    dk = d // RET_HEADS
    dv = v.shape[1] // RET_HEADS
    row = lambda b, j: (b * n + j, 0)
    return pl.pallas_call(
        functools.partial(_ret_core_kernel, chunk=RET_CHUNK),
        grid=(batch, n),
        in_specs=[pl.BlockSpec((c, d), row), pl.BlockSpec((c, d), row),
                  pl.BlockSpec((c, 2 * d), row), pl.BlockSpec((c, 2 * d), row)],
        out_specs=pl.BlockSpec((c, 2 * d), row),
        out_shape=jax.ShapeDtypeStruct((t, 2 * d), BF16),
        scratch_shapes=[pltpu.VMEM((RET_HEADS, dk, dv), F32)],
        compiler_params=_params("arbitrary", "arbitrary"),
        name="ret_core",
    )(q, k, v, gate)


def _gelu(t):
    p = 0.3275911 / math.sqrt(2.0)
    half_coefs = [0.5 * a for a in (0.254829592, -0.284496736, 1.421413741, -1.453152027, 1.061405429)]
    u = 1.0 / (1.0 + p * jnp.abs(t))
    poly = half_coefs[4]
    for coef in half_coefs[3::-1]:
        poly = poly * u + coef
    e = poly * u * jnp.exp2(t * t * (-0.5 * math.log2(math.e)))
    return t * jnp.where(t >= 0.0, 1.0 - e, e)


def _gmlp_kernel(x_ref, mod_ref, win_ref, lng_ref, lnb_ref, ws_ref, bs_ref, out_ref):
    tm, d = x_ref.shape
    width = win_ref.shape[1] // 2
    c = GMLP_CHUNK
    ng = GMLP_GROUPS
    gw = width // ng
    h = _modulate(x_ref[...], mod_ref, 0, 1)

    v = _gelu(_nn(h, win_ref[:, width:]))
    v = _layer_norm_rows(v, lng_ref[...], lnb_ref[...], LN_EPS).astype(BF16)
    u = _gelu(_nn(h, win_ref[:, :width]))
    row = lax.broadcasted_iota(jnp.int32, (c, c), 0)
    col = lax.broadcasted_iota(jnp.int32, (c, c), 1)
    for n in range(tm // c):
        rs = slice(n * c, (n + 1) * c)
        for gi in range(ng):
            ls = slice(gi * gw, (gi + 1) * gw)
            ws = jnp.where(row >= col, ws_ref[gi], 0.0).astype(BF16)
            vs = _nn(ws, v[rs, ls]) + bs_ref[:, gi:gi + 1]
            out_ref[rs, ls] = (u[rs, ls] * vs).astype(BF16)


def _gmlp(x, mod, w_in, ln_g, ln_b, w_s, b_s, *, seq, tm=512):
    t, d = x.shape
    width = w_in.shape[1] // 2
    ng, c, _ = w_s.shape
    tps = seq // tm
    return pl.pallas_call(
        _gmlp_kernel,
        grid=(t // tm,),
        in_specs=[
            pl.BlockSpec((tm, d), lambda i: (i, 0)),
            pl.BlockSpec((1, 6, d), lambda i: (i // tps, 0, 0)),
            _resident((d, 2 * width)),
            _resident((1, width)),
            _resident((1, width)),
            _resident((ng, c, c)),
            _resident((c, ng)),
        ],
        out_specs=pl.BlockSpec((tm, width), lambda i: (i, 0)),
        out_shape=jax.ShapeDtypeStruct((t, width), BF16),
        compiler_params=_params("arbitrary"),
        name="gmlp",
    )(x, mod, w_in.astype(BF16), ln_g.reshape(1, width), ln_b.reshape(1, width), w_s, b_s.T)


def _sb_in_kernel(x_ref, mod_ref, w_ref, q_ref, k_ref, v_ref, *, q_scale):
    d = x_ref.shape[1]
    h = _modulate(x_ref[...], mod_ref, 0, 1)
    q_ref[...] = (_nn(h, w_ref[:, 0:d]) * q_scale).astype(BF16)
    k_ref[...] = _nn(h, w_ref[:, d:2 * d]).astype(BF16)
    v_ref[...] = _nn(h, w_ref[:, 2 * d:3 * d]).astype(BF16)


def _sb_in(x, mod, w_in, *, seq, tm=512):
    t, d = x.shape
    tps = seq // tm
    row = lambda i: (i, 0)
    return pl.pallas_call(
        functools.partial(_sb_in_kernel, q_scale=(d // SB_HEADS) ** -0.5 * math.log2(math.e)),
        grid=(t // tm,),
        in_specs=[
            pl.BlockSpec((tm, d), row),
            pl.BlockSpec((1, 6, d), lambda i: (i // tps, 0, 0)),
            pl.BlockSpec((d, 3 * d), lambda i: (0, 0)),
        ],
        out_specs=[pl.BlockSpec((tm, d), row)] * 3,
        out_shape=[jax.ShapeDtypeStruct((t, d), BF16)] * 3,
        compiler_params=_params("arbitrary"),
        name="sb_in",
    )(x, mod, w_in.astype(BF16))


def _sb_core_kernel(q_ref, k_ref, v_ref, o_ref, qh_ref, sp_ref, zc_ref, carry_ref, acc_ref, knorm_ref,
                    bound_ref):
    tq, width = q_ref.shape
    seq = k_ref.shape[0]
    n_heads = qh_ref.shape[0]
    dh = LANES // 2
    qi = pl.program_id(2)
    lane = lax.broadcasted_iota(jnp.int32, (tq, LANES), 1)
    half_lanes = (lane < dh, lane >= dh)
    group = [slice((hh // 2) * LANES, (hh // 2 + 1) * LANES) for hh in range(n_heads)]
    row = lax.broadcasted_iota(jnp.int32, (tq, tq), 0)
    col = lax.broadcasted_iota(jnp.int32, (tq, tq), 1)
    ones_lower = jnp.where(row >= col, 1.0, 0.0).astype(BF16)

    @pl.when(qi == 0)
    def _():
        def key_tile(t, best):
            kf = k_ref[pl.ds(pl.multiple_of(t * tq, tq), tq), :].astype(F32)
            return jnp.maximum(best, jnp.max(kf * kf, axis=0, keepdims=True))

        col_max = lax.fori_loop(0, seq // tq, key_tile, jnp.zeros((1, width), F32))
        for hh in range(n_heads):
            bound_sq = jnp.sum(jnp.where(half_lanes[hh % 2][0:1, :], col_max[:, group[hh]], 0.0),
                               axis=-1, keepdims=True)
            knorm_ref[hh] = jnp.broadcast_to(bound_sq, knorm_ref.shape[1:])

    for hh in range(n_heads):
        q = q_ref[:, group[hh]]
        qf = q.astype(F32)
        qh_ref[hh] = jnp.where(half_lanes[hh % 2], q, jnp.zeros_like(q))
        qsq = jnp.sum(jnp.where(half_lanes[hh % 2], qf * qf, 0.0), axis=-1, keepdims=True)
        bound_ref[hh] = jnp.sqrt(qsq * knorm_ref[hh, 0:1, 0:1]) * NORM_BOUND_SLACK
    acc_ref[...] = jnp.zeros(acc_ref.shape, F32)
    carry_ref[...] = jnp.zeros(carry_ref.shape, F32)

    def stage(kt, slot, diagonal=False):
        rows = pl.ds(pl.multiple_of(kt * tq, tq), tq)
        for hh in range(n_heads):
            z = _nt(qh_ref[hh], k_ref[rows, group[hh]])
            sp = jnp.maximum(z, jnp.log2(1.0 + jnp.exp2(jnp.minimum(z, EXP2_CLAMP))))
            carry = carry_ref[hh]
            zc = z - carry
            if diagonal:
                sp = jnp.where(col < row, sp, 0.0)
                zc = jnp.where(col < row, zc, -1e30)
            sp_ref[slot, hh] = sp.astype(BF16)
            zc_ref[slot, hh] = zc
            carry_ref[hh] = carry + jnp.sum(sp, axis=-1, keepdims=True)

    def consume(kt, slot):
        rows = pl.ds(pl.multiple_of(kt * tq, tq), tq)
        for hh in range(n_heads):
            inclusive = _nn(sp_ref[slot, hh], ones_lower)
            a = jnp.exp2(zc_ref[slot, hh] - inclusive)
            acc_ref[hh] += _nn(a.astype(BF16), v_ref[rows, group[hh]])

    def still_alive():
        slack = bound_ref[0] - carry_ref[0]
        for hh in range(1, n_heads):
            slack = jnp.maximum(slack, bound_ref[hh] - carry_ref[hh])
        return (jnp.max(slack) > -DEAD_EXPONENT).astype(jnp.int32)

    stage(qi, 0, diagonal=True)

    @pl.when(qi == 0)
    def _():
        consume(0, 0)

    @pl.when(qi > 0)
    def _():
        stage(qi - 1, 1)
        alive = still_alive()
        consume(qi, 0)
        n_pairs = (qi - 1) // 2

        def more(state):
            return (state[0] < n_pairs) & (state[1] > 0)

        def pair(state):
            kt = qi - 1 - 2 * state[0]
            stage(kt - 1, 0)
            consume(kt, 1)
            stage(kt - 2, 1)
            alive = still_alive()
            consume(kt - 1, 0)
            return state[0] + 1, alive

        pairs_done, alive = lax.while_loop(more, pair, (jnp.int32(0), alive))
        staged = qi - 1 - 2 * pairs_done

        @pl.when((alive == 0) | (staged == 0))
        def _():
            consume(staged, 1)

        @pl.when((alive > 0) & (staged == 1))
        def _():
            stage(0, 0)
            consume(1, 1)
            consume(0, 0)

    for hh in range(0, n_heads, 2):
        o_ref[:, group[hh]] = jnp.where(half_lanes[0], acc_ref[hh], acc_ref[hh + 1]).astype(BF16)


def _sb_core(q, k, v, *, batch, seq, tq=256, heads_per_step=8):
    t, d = q.shape
    tq = min(tq, seq)
    nq = seq // tq
    nh = heads_per_step
    width = nh // 2 * LANES
    return pl.pallas_call(
        _sb_core_kernel,
        grid=(batch, d // width, nq),
        in_specs=[
            pl.BlockSpec((tq, width), lambda b, g, i: (b * nq + i, g)),
            pl.BlockSpec((seq, width), lambda b, g, i: (b, g)),
            pl.BlockSpec((seq, width), lambda b, g, i: (b, g)),
        ],
        out_specs=pl.BlockSpec((tq, width), lambda b, g, i: (b * nq + i, g)),
        out_shape=jax.ShapeDtypeStruct((t, d), BF16),
        scratch_shapes=[
            pltpu.VMEM((nh, tq, LANES), BF16),
            pltpu.VMEM((2, nh, tq, tq), BF16),
            pltpu.VMEM((2, nh, tq, tq), F32),
            pltpu.VMEM((nh, tq, 1), F32),
            pltpu.VMEM((nh, tq, LANES), F32),
            pltpu.VMEM((nh, SUBLANES, LANES), F32),
            pltpu.VMEM((nh, tq, 1), F32),
        ],
        compiler_params=_params("arbitrary", "arbitrary", "arbitrary"),
        name="sb_core",
    )(q, k, v)


def kernel(x, c, cond_w, cond_b, ada_w, ada_b, ln_g, ln_b, ffn_up, ffn_conv_w, ffn_conv_b, ffn_down,
           gdn_w_in, gdn_conv_w, gdn_a_log, gdn_dt_bias, gdn_norm_w, gdn_w_out,
           ret_w_in, ret_w_out,
           gmlp_w_in, gmlp_ln_g, gmlp_ln_b, gmlp_w_s, gmlp_b_s, gmlp_w_out,
           sb_w_in, sb_w_out):
    batch, seq, d = x.shape
    depth = ada_w.shape[0]
    alpha = (2.0 * depth) ** 0.25
    mods = _conditioning(c, cond_w, cond_b, ada_w, ada_b).reshape(depth, batch, 6, d)
    xt = x.reshape(batch * seq, d)
    ffn_up_b = ffn_up.astype(BF16)
    ffn_down_b = ffn_down.astype(BF16)

    for i in range(depth):
        mod = mods[i]
        mixer = i % 4
        if mixer == 0:
            q, k, v, z, gb, gt = _gdn_in(xt, mod, gdn_w_in, gdn_conv_w, gdn_a_log, gdn_dt_bias, seq=seq)
            o, w_out = _gdn_core(q, k, v, z, gb, gt, gdn_norm_w, batch=batch, seq=seq), gdn_w_out
        elif mixer == 1:
            cos_t, sin_t = _rope_tables(seq, d // RET_HEADS // 2)
            q, k, v, gate = _ret_in(xt, mod, ret_w_in, cos_t, sin_t, seq=seq)
            o, w_out = _ret_core(q, k, v, gate, batch=batch, seq=seq), ret_w_out
        elif mixer == 2:
            o = _gmlp(xt, mod, gmlp_w_in, gmlp_ln_g, gmlp_ln_b, gmlp_w_s, gmlp_b_s, seq=seq)
            w_out = gmlp_w_out
        else:
            q, k, v = _sb_in(xt, mod, sb_w_in, seq=seq)
            o, w_out = _sb_core(q, k, v, batch=batch, seq=seq), sb_w_out
        xt = _sublayer_tail(o, w_out, xt, mod, ffn_up_b, ffn_conv_w[i], ffn_conv_b[i], ffn_down_b,
                            ln_g, ln_b, layer=i, seq=seq, alpha=alpha)
    return xt.reshape(batch, seq, d)
```

```python
import functools
import math

import jax
import jax.numpy as jnp
from jax import lax
from jax.experimental import pallas as pl
from jax.experimental.pallas import tpu as pltpu

F32 = jnp.float32
BF16 = jnp.bfloat16

LANES = 128
SUBLANES = 8
VMEM_LIMIT = 56 * 1024 * 1024

LN_EPS = 1e-5
GDN_HEADS = 8
GDN_CHUNK = 64
GDN_CONV = 4
RET_HEADS = 4
RET_CHUNK = 128
RET_ROPE_BASE = 10000.0
GMLP_CHUNK = 128
GMLP_GROUPS = 8
SB_HEADS = 16
FFN_CONV = 3
DEAD_EXPONENT = 160.0
NORM_BOUND_SLACK = 1.001
EXP2_CLAMP = 64.0


def _params(*sem):
    return pltpu.CompilerParams(dimension_semantics=sem, vmem_limit_bytes=VMEM_LIMIT)


def _nn(a, b):
    return jnp.dot(a, b, preferred_element_type=F32)


def _nt(a, b):
    return lax.dot_general(a, b, (((1,), (1,)), ((), ())), preferred_element_type=F32)


def _tn(a, b):
    return lax.dot_general(a, b, (((0,), (0,)), ((), ())), preferred_element_type=F32)


def _sigmoid(x):
    return 1.0 / (1.0 + jnp.exp(-x))


def _silu(x):
    return x * _sigmoid(x)


def _softplus(x):
    return jnp.maximum(x, 0.0) + jnp.log(1.0 + jnp.exp(-jnp.abs(x)))


def _split3(x):
    hi = x.astype(BF16)
    r = x - hi.astype(F32)
    mid = r.astype(BF16)
    lo = (r - mid.astype(F32)).astype(BF16)
    return hi, mid, lo


def _modulate(x, mod_ref, shift_row, scale_row):
    return (x * (1.0 + mod_ref[0, scale_row:scale_row + 1, :])
            + mod_ref[0, shift_row:shift_row + 1, :]).astype(BF16)


def _layer_norm_rows(r, g, b, eps):
    mu = jnp.mean(r, axis=-1, keepdims=True)
    rc = r - mu
    var = jnp.mean(rc * rc, axis=-1, keepdims=True)
    return rc * lax.rsqrt(var + eps) * g + b


def _lane_replicated_columns(rows):
    n, k = rows.shape
    padded = jnp.concatenate([rows, jnp.zeros((LANES - n, k), F32)], axis=0)
    t = padded.T
    return [jnp.broadcast_to(t[:, b:b + 1], (k, LANES)) for b in range(n)]


def _rowvec_matmul(col, w):
    n = w.shape[1]
    parts = [jnp.sum(col * w[:, c:c + LANES], axis=0, keepdims=True) for c in range(0, n, LANES)]
    return jnp.concatenate(parts, axis=1)


def _cond_kernel(c_ref, cw_ref, cb_ref, aw_ref, ab_ref, o_ref, ecol_ref):
    nb = c_ref.shape[0]

    @pl.when((pl.program_id(0) == 0) & (pl.program_id(1) == 0))
    def _():
        ccols = _lane_replicated_columns(c_ref[...])
        cw = cw_ref[...]
        e = jnp.concatenate([_rowvec_matmul(col, cw) for col in ccols], axis=0) + cb_ref[...]
        ecols = _lane_replicated_columns(_silu(e))
        for b in range(nb):
            ecol_ref[b] = ecols[b]

    w = aw_ref[0]
    for b in range(nb):
        o_ref[0, b:b + 1, :] = _rowvec_matmul(ecol_ref[b], w) + ab_ref[0]


def _conditioning(c, cond_w, cond_b, ada_w, ada_b):
    nb, d = c.shape
    depth, _, n6 = ada_w.shape
    tn = 1536
    return pl.pallas_call(
        _cond_kernel,
        grid=(depth, n6 // tn),
        in_specs=[
            pl.BlockSpec((nb, d), lambda i, j: (0, 0)),
            pl.BlockSpec((d, d), lambda i, j: (0, 0)),
            pl.BlockSpec((1, d), lambda i, j: (0, 0)),
            pl.BlockSpec((1, d, tn), lambda i, j: (i, 0, j)),
            pl.BlockSpec((1, 1, tn), lambda i, j: (i, 0, j)),
        ],
        out_specs=pl.BlockSpec((1, nb, tn), lambda i, j: (i, 0, j)),
        out_shape=jax.ShapeDtypeStruct((depth, nb, n6), F32),
        scratch_shapes=[pltpu.VMEM((nb, d, LANES), F32)],
        compiler_params=_params("arbitrary", "arbitrary"),
        name="conditioning",
    )(c, cond_w, cond_b.reshape(1, d), ada_w, ada_b.reshape(depth, 1, n6))


def _ffn_kernel(o_ref, wo_ref, g1_ref, b1_ref, x_ref, mod_ref, wup_ref, cw_ref, cb_ref, wd_ref,
                g_ref, b_ref, out_ref, x1_ref, h_ref, buf_ref, act_ref, *, tiles_per_seq, alpha):
    i = pl.program_id(0)
    tm = x_ref.shape[0]
    f = wd_ref.shape[0]
    n_chunks, _, tf = buf_ref.shape
    half = tm // 2
    halves = (slice(0, half), slice(half, tm))
    for rs in halves:
        mixed = alpha * x_ref[rs, :] + (1.0 + mod_ref[0, 2:3, :]) * _nn(o_ref[rs, :], wo_ref[...])
        x1 = _layer_norm_rows(mixed, g1_ref[...], b1_ref[...], LN_EPS)
        x1_ref[rs, :] = x1
        h_ref[rs, :] = _modulate(x1, mod_ref, 3, 4)

    @pl.when(i % tiles_per_seq == 0)
    def _():
        for c in range(n_chunks):
            buf_ref[c, 0:SUBLANES, :] = jnp.zeros((SUBLANES, tf), F32)

    for c in range(n_chunks):
        cols = slice(c * tf, (c + 1) * tf)
        buf_ref[c, SUBLANES:SUBLANES + tm, :] = _nn(h_ref[...], wup_ref[:, cols])
        conv = cb_ref[:, cols]
        for tap in range(FFN_CONV):
            off = SUBLANES - (FFN_CONV - 1) + tap
            conv = conv + cw_ref[tap:tap + 1, cols] * buf_ref[c, off:off + tm, :]
        buf_ref[c, 0:SUBLANES, :] = buf_ref[c, tm:tm + SUBLANES, :]
        up = _nn(h_ref[...], wup_ref[:, f + c * tf:f + (c + 1) * tf])
        act_ref[:, cols] = (_silu(conv) * up).astype(BF16)

    for rs in halves:
        y = _nn(act_ref[rs, :], wd_ref[...])
        r = alpha * x1_ref[rs, :] + (1.0 + mod_ref[0, 5:6, :]) * y
        out_ref[rs, :] = _layer_norm_rows(r, g_ref[...], b_ref[...], LN_EPS)


def _resident(shape, layer=None):
    if layer is None:
        return pl.BlockSpec(shape, lambda *_: (0,) * len(shape), pipeline_mode=pl.Buffered(1))
    return pl.BlockSpec((None,) + tuple(shape), lambda *_: (layer,) + (0,) * len(shape),
                        pipeline_mode=pl.Buffered(1))


def _sublayer_tail(o, w_out, x, mod, w_up, conv_w, conv_b, w_down, ln_g, ln_b, *, layer, seq, alpha,
                   tm=512, n_chunks=11):
    t, d = x.shape
    kdim = o.shape[1]
    f = w_down.shape[1]
    tf = f // n_chunks
    tps = seq // tm
    rows = lambda i: (i, 0)
    return pl.pallas_call(
        functools.partial(_ffn_kernel, tiles_per_seq=tps, alpha=alpha),
        grid=(t // tm,),
        in_specs=[
            pl.BlockSpec((tm, kdim), rows),
            _resident((kdim, d)),
            _resident((1, d)),
            _resident((1, d)),
            pl.BlockSpec((tm, d), rows),
            pl.BlockSpec((1, 6, d), lambda i: (i // tps, 0, 0)),
            _resident((d, 2 * f), layer),
            _resident((FFN_CONV, f)),
            _resident((1, f)),
            _resident((f, d), layer),
            _resident((1, d)),
            _resident((1, d)),
        ],
        out_specs=pl.BlockSpec((tm, d), rows),
        out_shape=jax.ShapeDtypeStruct((t, d), F32),
        scratch_shapes=[
            pltpu.VMEM((tm, d), F32),
            pltpu.VMEM((tm, d), BF16),
            pltpu.VMEM((n_chunks, tm + SUBLANES, tf), F32),
            pltpu.VMEM((tm, f), BF16),
        ],
        compiler_params=_params("arbitrary"),
        name="sublayer_tail",
    )(o, w_out.astype(BF16), ln_g[layer, 0].reshape(1, d), ln_b[layer, 0].reshape(1, d), x, mod,
      w_up, conv_w, conv_b.reshape(1, f), w_down, ln_g[layer, 1].reshape(1, d), ln_b[layer, 1].reshape(1, d))


def _gdn_in_kernel(x_ref, mod_ref, wqkv_ref, wz_ref, wab_ref, cw_ref, alog_ref, dtb_ref,
                   q_ref, k_ref, v_ref, z_ref, gb_ref, gt_ref, buf_ref, *, tiles_per_seq, chunk):
    i = pl.program_id(0)
    tm, d = x_ref.shape
    nh = GDN_HEADS
    hd = d // nh
    h = _modulate(x_ref[...], mod_ref, 0, 1)

    @pl.when(i % tiles_per_seq == 0)
    def _():
        buf_ref[0:SUBLANES, :] = jnp.zeros((SUBLANES, buf_ref.shape[1]), F32)

    outs = (q_ref, k_ref, v_ref)
    for s in range(3):
        cs = slice(s * d, (s + 1) * d)
        buf_ref[SUBLANES:SUBLANES + tm, cs] = _nn(h, wqkv_ref[:, cs])
        y = None
        for tap in range(GDN_CONV):
            off = SUBLANES - (GDN_CONV - 1) + tap
            term = cw_ref[tap:tap + 1, cs] * buf_ref[off:off + tm, cs]
            y = term if y is None else y + term
        y = _silu(y)
        if s < 2:
            scale = hd ** -0.5 if s == 0 else 1.0
            for hh in range(nh):
                seg = y[:, hh * hd:(hh + 1) * hd]
                inv = lax.rsqrt(jnp.sum(seg * seg, axis=-1, keepdims=True) + 1e-6)
                outs[s][:, hh * hd:(hh + 1) * hd] = (seg * (inv * scale)).astype(BF16)
        else:
            outs[s][...] = y.astype(BF16)
    buf_ref[0:SUBLANES, :] = buf_ref[tm:tm + SUBLANES, :]

    z_ref[...] = _nn(h, wz_ref[...]).astype(BF16)

    pab = _nn(h, wab_ref[...])
    lane = lax.broadcasted_iota(jnp.int32, pab.shape, 1)
    g = -jnp.exp(alog_ref[...]) * _softplus(pab + dtb_ref[...])
    gb = jnp.where(lane < nh, g, jnp.where(lane < 2 * nh, _sigmoid(pab), 0.0))
    row = lax.broadcasted_iota(jnp.int32, (tm, tm), 0)
    col = lax.broadcasted_iota(jnp.int32, (tm, tm), 1)
    tri = jnp.where(col <= row, jnp.where(jnp.bitwise_xor(row, col) < chunk, 1.0, 0.0), 0.0).astype(BF16)
    hi, mid, lo = _split3(gb)
    cum = _nn(tri, hi) + _nn(tri, mid) + _nn(tri, lo)
    gb = jnp.where(lane < nh, cum, gb)
    gb_ref[...] = gb
    gt_ref[...] = gb.T[0:2 * nh, :]


def _gdn_in(x, mod, w_in, conv_w, a_log, dt_bias, *, seq, tm=256):
    t, d = x.shape
    nh = GDN_HEADS
    tps = seq // tm
    w_b = w_in.astype(BF16)
    w_ab = jnp.pad(w_b[:, 4 * d:], ((0, 0), (0, LANES - 2 * nh)))
    once = pl.Buffered(1)
    alog = jnp.pad(a_log, (0, LANES - nh)).reshape(1, LANES)
    dtb = jnp.pad(dt_bias, (0, LANES - nh)).reshape(1, LANES)
    row = lambda i: (i, 0)
    fixed = lambda i: (0, 0)
    return pl.pallas_call(
        functools.partial(_gdn_in_kernel, tiles_per_seq=tps, chunk=GDN_CHUNK),
        grid=(t // tm,),
        in_specs=[
            pl.BlockSpec((tm, d), row),
            pl.BlockSpec((1, 6, d), lambda i: (i // tps, 0, 0)),
            pl.BlockSpec((d, 3 * d), fixed, pipeline_mode=once),
            pl.BlockSpec((d, d), lambda i: (0, 3), pipeline_mode=once),
            pl.BlockSpec((d, LANES), fixed, pipeline_mode=once),
            pl.BlockSpec((GDN_CONV, 3 * d), fixed, pipeline_mode=once),
            pl.BlockSpec((1, LANES), fixed),
            pl.BlockSpec((1, LANES), fixed),
        ],
        out_specs=[pl.BlockSpec((tm, d), row)] * 4 + [pl.BlockSpec((tm, LANES), row),
                                                      pl.BlockSpec((2 * nh, tm), lambda i: (0, i))],
        out_shape=[jax.ShapeDtypeStruct((t, d), BF16)] * 4 + [jax.ShapeDtypeStruct((t, LANES), F32),
                                                              jax.ShapeDtypeStruct((2 * nh, t), F32)],
        scratch_shapes=[pltpu.VMEM((tm + SUBLANES, 3 * d), F32)],
        compiler_params=_params("arbitrary"),
        name="gdn_in",
    )(x, mod, w_b, w_b, w_ab, conv_w, alog, dtb)


def _gdn_core_kernel(q_ref, k_ref, v_ref, z_ref, gb_ref, gt_ref, nw_ref, o_ref,
                     state_ref, inv_ref, pw_ref, rhs_ref, u_ref, w_ref, qk_ref, qg_ref, kd_ref, dl_ref,
                     vn_ref, oi_ref, *, chunk):
    rows, d = q_ref.shape
    nh = GDN_HEADS
    hd = d // nh
    n_chunks = rows // chunk
    heads = range(nh)
    lanes = [slice(hh * hd, (hh + 1) * hd) for hh in heads]

    @pl.when(pl.program_id(1) == 0)
    def _():
        state_ref[...] = jnp.zeros(state_ref.shape, F32)

    row = lax.broadcasted_iota(jnp.int32, (rows, rows), 0)
    col = lax.broadcasted_iota(jnp.int32, (rows, rows), 1)
    same_chunk = jnp.bitwise_xor(row, col) < chunk
    gb = gb_ref[...]
    for hh in heads:
        kh = k_ref[:, lanes[hh]]
        kf = kh.astype(F32)
        qh = q_ref[:, lanes[hh]]
        gc = jnp.broadcast_to(gb[:, hh:hh + 1], (rows, hd))
        beta = jnp.broadcast_to(gb[:, nh + hh:nh + hh + 1], (rows, hd))
        diff = jnp.broadcast_to(gb[:, hh:hh + 1], (rows, rows)) - gt_ref[hh:hh + 1, :]
        causal = same_chunk & (row >= col)
        decay = jnp.where(causal, jnp.exp(jnp.where(causal, diff, 0.0)), 0.0)
        kb = kf * beta
        a = jnp.where(row > col, _nt(kb.astype(BF16), kh) * decay, 0.0)
        inv_ref[hh] = jnp.where(row == col, 1.0, 0.0) - a
        pw_ref[hh] = a.astype(BF16)
        qk_ref[hh] = (_nt(qh, kh) * decay).astype(BF16)
        rhs_ref[hh, :, 0:hd] = (v_ref[:, lanes[hh]].astype(F32) * beta).astype(BF16)
        rhs_ref[hh, :, hd:2 * hd] = (kb * jnp.exp(gc)).astype(BF16)
        qg_ref[hh] = (qh.astype(F32) * jnp.exp(gc)).astype(BF16)
        for c in range(n_chunks):
            rs = slice(c * chunk, (c + 1) * chunk)
            g_last = gc[(c + 1) * chunk - 1:(c + 1) * chunk, :]
            kd_ref[hh, rs, :] = (kf[rs] * jnp.exp(g_last - gc[rs])).astype(BF16)
            dl_ref[hh, c] = jnp.exp(g_last)

    for _ in range(int(math.log2(chunk)) - 1):
        for hh in heads:
            power = _nn(pw_ref[hh], pw_ref[hh]).astype(BF16)
            pw_ref[hh] = power
            inv = inv_ref[hh]
            inv_ref[hh] = inv + _nn(inv.astype(BF16), power)
    for hh in heads:
        sol = _nn(inv_ref[hh].astype(BF16), rhs_ref[hh])
        u_ref[hh] = sol[:, 0:hd]
        w_ref[hh] = sol[:, hd:2 * hd].astype(BF16)

    for c in range(n_chunks):
        rs = slice(c * chunk, (c + 1) * chunk)
        states = [state_ref[hh].astype(BF16) for hh in heads]
        v_new = [(u_ref[hh, rs, :] - _nn(w_ref[hh, rs, :], states[hh])).astype(BF16) for hh in heads]
        for hh in heads:
            vn_ref[hh, rs, :] = v_new[hh]
            state_ref[hh] = state_ref[hh] * dl_ref[hh, c] + _tn(kd_ref[hh, rs, :], v_new[hh])
            oi_ref[hh, rs, :] = _nn(qg_ref[hh, rs, :], states[hh])

    for hh in heads:
        o = oi_ref[hh] + _nn(qk_ref[hh], vn_ref[hh])
        o = o * lax.rsqrt(jnp.mean(o * o, axis=-1, keepdims=True) + 1e-6) * nw_ref[...]
        o_ref[:, lanes[hh]] = (o * _silu(z_ref[:, lanes[hh]].astype(F32))).astype(BF16)


def _gdn_core(q, k, v, z, gb, gt, norm_w, *, batch, seq, rows=256):
    t, d = q.shape
    n = seq // rows
    nh = GDN_HEADS
    hd = d // nh
    row = lambda b, j: (b * n + j, 0)
    return pl.pallas_call(
        functools.partial(_gdn_core_kernel, chunk=GDN_CHUNK),
        grid=(batch, n),
        in_specs=[pl.BlockSpec((rows, d), row)] * 4 + [
            pl.BlockSpec((rows, LANES), row),
            pl.BlockSpec((2 * nh, rows), lambda b, j: (0, b * n + j)),
            pl.BlockSpec((1, hd), lambda b, j: (0, 0)),
        ],
        out_specs=pl.BlockSpec((rows, d), row),
        out_shape=jax.ShapeDtypeStruct((t, d), BF16),
        scratch_shapes=[
            pltpu.VMEM((nh, hd, hd), F32),
            pltpu.VMEM((nh, rows, rows), F32),
            pltpu.VMEM((nh, rows, rows), BF16),
            pltpu.VMEM((nh, rows, 2 * hd), BF16),
            pltpu.VMEM((nh, rows, hd), F32),
            pltpu.VMEM((nh, rows, hd), BF16),
            pltpu.VMEM((nh, rows, rows), BF16),
            pltpu.VMEM((nh, rows, hd), BF16),
            pltpu.VMEM((nh, rows, hd), BF16),
            pltpu.VMEM((nh, rows // GDN_CHUNK, 1, hd), F32),
            pltpu.VMEM((nh, rows, hd), BF16),
            pltpu.VMEM((nh, rows, hd), F32),
        ],
        compiler_params=_params("arbitrary", "arbitrary"),
        name="gdn_core",
    )(q, k, v, z, gb, gt, norm_w.reshape(1, hd))


def _rope_table_kernel(cos_ref, sin_ref):
    ts, half = cos_ref.shape
    pos = (lax.broadcasted_iota(jnp.int32, (ts, half), 0) + pl.program_id(0) * ts).astype(F32)
    frac = lax.broadcasted_iota(jnp.int32, (ts, half), 1).astype(F32) / (half - 1.0)
    ang = pos * jnp.exp(-frac * math.log(RET_ROPE_BASE))
    cos_ref[...] = jnp.cos(ang)
    sin_ref[...] = jnp.sin(ang)


def _rope_tables(seq, half, ts=256):
    spec = pl.BlockSpec((ts, half), lambda i: (i, 0))
    return pl.pallas_call(
        _rope_table_kernel,
        grid=(seq // ts,),
        out_specs=[spec, spec],
        out_shape=[jax.ShapeDtypeStruct((seq, half), F32)] * 2,
        compiler_params=_params("arbitrary"),
        name="rope_tables",
    )()


def _ret_in_kernel(x_ref, mod_ref, w_ref, cos_ref, sin_ref, q_ref, k_ref, v_ref, gate_ref):
    d = x_ref.shape[1]
    nh = RET_HEADS
    dk = d // nh
    half = dk // 2
    h = _modulate(x_ref[...], mod_ref, 0, 1)
    cos_a = cos_ref[...]
    sin_a = sin_ref[...]
    for s, (out, scale) in enumerate(((q_ref, 1.0), (k_ref, dk ** -0.5))):
        t = _nn(h, w_ref[:, s * d:(s + 1) * d])
        for hh in range(nh):
            t1 = t[:, hh * dk:hh * dk + half]
            t2 = t[:, hh * dk + half:(hh + 1) * dk]
            out[:, hh * dk:hh * dk + half] = ((t1 * cos_a - t2 * sin_a) * scale).astype(BF16)
            out[:, hh * dk + half:(hh + 1) * dk] = ((t1 * sin_a + t2 * cos_a) * scale).astype(BF16)
    v_ref[...] = _nn(h, w_ref[:, 2 * d:4 * d]).astype(BF16)
    gate_ref[...] = _silu(_nn(h, w_ref[:, 4 * d:6 * d])).astype(BF16)


def _ret_in(x, mod, w_in, cos_t, sin_t, *, seq, tm=512):
    t, d = x.shape
    tps = seq // tm
    half = cos_t.shape[1]
    row = lambda i: (i, 0)
    return pl.pallas_call(
        _ret_in_kernel,
        grid=(t // tm,),
        in_specs=[
            pl.BlockSpec((tm, d), row),
            pl.BlockSpec((1, 6, d), lambda i: (i // tps, 0, 0)),
            _resident((d, 6 * d)),
            pl.BlockSpec((tm, half), lambda i: (i % tps, 0)),
            pl.BlockSpec((tm, half), lambda i: (i % tps, 0)),
        ],
        out_specs=[pl.BlockSpec((tm, d), row), pl.BlockSpec((tm, d), row),
                   pl.BlockSpec((tm, 2 * d), row), pl.BlockSpec((tm, 2 * d), row)],
        out_shape=[jax.ShapeDtypeStruct((t, d), BF16), jax.ShapeDtypeStruct((t, d), BF16),
                   jax.ShapeDtypeStruct((t, 2 * d), BF16), jax.ShapeDtypeStruct((t, 2 * d), BF16)],
        compiler_params=_params("arbitrary"),
        name="ret_in",
    )(x, mod, w_in.astype(BF16), cos_t, sin_t)


def _ret_core_kernel(q_ref, k_ref, v_ref, gate_ref, o_ref, state_ref, *, chunk):
    rows, d = q_ref.shape
    c = chunk
    nh = RET_HEADS
    dk = d // nh
    dv = v_ref.shape[1] // nh

    @pl.when(pl.program_id(1) == 0)
    def _():
        state_ref[...] = jnp.zeros(state_ref.shape, F32)

    rel = (lax.broadcasted_iota(jnp.int32, (c, c), 0)
           - lax.broadcasted_iota(jnp.int32, (c, c), 1)).astype(F32)
    idx = lax.broadcasted_iota(jnp.int32, (c, 1), 0).astype(F32)
    for hh in range(nh):
        log_gamma = math.log(1.0 - 2.0 ** (-5.0 - hh))
        dmask = jnp.where(rel >= 0, jnp.exp(jnp.maximum(rel, 0.0) * log_gamma), 0.0)
        zeta = jnp.exp((c - 1.0 - idx) * log_gamma)
        xi = jnp.exp((idx + 1.0) * log_gamma)
        state = state_ref[hh]
        for ci in range(rows // c):
            rs = slice(ci * c, (ci + 1) * c)
            qh = q_ref[rs, hh * dk:(hh + 1) * dk]
            kh = k_ref[rs, hh * dk:(hh + 1) * dk]
            vh = v_ref[rs, hh * dv:(hh + 1) * dv]
            scores = _nt(qh, kh) * dmask
            o = _nn(scores.astype(BF16), vh) + _nn(qh, state.astype(BF16)) * xi
            state = state * math.exp(c * log_gamma) + _tn((kh.astype(F32) * zeta).astype(BF16), vh)
            mu = jnp.mean(o, axis=-1, keepdims=True)
            oc = o - mu
            var = jnp.mean(oc * oc, axis=-1, keepdims=True)
            o = oc * lax.rsqrt(var + 1e-6)
            gate = gate_ref[rs, hh * dv:(hh + 1) * dv].astype(F32)
            o_ref[rs, hh * dv:(hh + 1) * dv] = (o * gate).astype(BF16)
        state_ref[hh] = state


def _ret_core(q, k, v, gate, *, batch, seq, rows=512):
    t, d = q.shape
    c = rows
    n = seq // rows
    dk = d // RET_HEADS
    dv = v.shape[1] // RET_HEADS
    row = lambda b, j: (b * n + j, 0)
    return pl.pallas_call(
        functools.partial(_ret_core_kernel, chunk=RET_CHUNK),
        grid=(batch, n),
        in_specs=[pl.BlockSpec((c, d), row), pl.BlockSpec((c, d), row),
                  pl.BlockSpec((c, 2 * d), row), pl.BlockSpec((c, 2 * d), row)],
        out_specs=pl.BlockSpec((c, 2 * d), row),
        out_shape=jax.ShapeDtypeStruct((t, 2 * d), BF16),
        scratch_shapes=[pltpu.VMEM((RET_HEADS, dk, dv), F32)],
        compiler_params=_params("arbitrary", "arbitrary"),
        name="ret_core",
    )(q, k, v, gate)


def _gelu(t):
    p = 0.3275911 / math.sqrt(2.0)
    half_coefs = [0.5 * a for a in (0.254829592, -0.284496736, 1.421413741, -1.453152027, 1.061405429)]
    u = 1.0 / (1.0 + p * jnp.abs(t))
    poly = half_coefs[4]
    for coef in half_coefs[3::-1]:
        poly = poly * u + coef
    e = poly * u * jnp.exp2(t * t * (-0.5 * math.log2(math.e)))
    return t * jnp.where(t >= 0.0, 1.0 - e, e)


def _gmlp_kernel(x_ref, mod_ref, win_ref, lng_ref, lnb_ref, ws_ref, bs_ref, out_ref):
    tm, d = x_ref.shape
    width = win_ref.shape[1] // 2
    c = GMLP_CHUNK
    ng = GMLP_GROUPS
    gw = width // ng
    h = _modulate(x_ref[...], mod_ref, 0, 1)

    v = _gelu(_nn(h, win_ref[:, width:]))
    v = _layer_norm_rows(v, lng_ref[...], lnb_ref[...], LN_EPS).astype(BF16)
    u = _gelu(_nn(h, win_ref[:, :width]))
    row = lax.broadcasted_iota(jnp.int32, (c, c), 0)
    col = lax.broadcasted_iota(jnp.int32, (c, c), 1)
    for n in range(tm // c):
        rs = slice(n * c, (n + 1) * c)
        for gi in range(ng):
            ls = slice(gi * gw, (gi + 1) * gw)
            ws = jnp.where(row >= col, ws_ref[gi], 0.0).astype(BF16)
            vs = _nn(ws, v[rs, ls]) + bs_ref[:, gi:gi + 1]
            out_ref[rs, ls] = (u[rs, ls] * vs).astype(BF16)


def _gmlp(x, mod, w_in, ln_g, ln_b, w_s, b_s, *, seq, tm=512):
    t, d = x.shape
    width = w_in.shape[1] // 2
    ng, c, _ = w_s.shape
    tps = seq // tm
    return pl.pallas_call(
        _gmlp_kernel,
        grid=(t // tm,),
        in_specs=[
            pl.BlockSpec((tm, d), lambda i: (i, 0)),
            pl.BlockSpec((1, 6, d), lambda i: (i // tps, 0, 0)),
            _resident((d, 2 * width)),
            _resident((1, width)),
            _resident((1, width)),
            _resident((ng, c, c)),
            _resident((c, ng)),
        ],
        out_specs=pl.BlockSpec((tm, width), lambda i: (i, 0)),
        out_shape=jax.ShapeDtypeStruct((t, width), BF16),
        compiler_params=_params("arbitrary"),
        name="gmlp",
    )(x, mod, w_in.astype(BF16), ln_g.reshape(1, width), ln_b.reshape(1, width), w_s, b_s.T)


def _sb_in_kernel(x_ref, mod_ref, w_ref, q_ref, k_ref, v_ref, *, q_scale):
    d = x_ref.shape[1]
    h = _modulate(x_ref[...], mod_ref, 0, 1)
    q_ref[...] = (_nn(h, w_ref[:, 0:d]) * q_scale).astype(BF16)
    k_ref[...] = _nn(h, w_ref[:, d:2 * d]).astype(BF16)
    v_ref[...] = _nn(h, w_ref[:, 2 * d:3 * d]).astype(BF16)


def _sb_in(x, mod, w_in, *, seq, tm=512):
    t, d = x.shape
    tps = seq // tm
    row = lambda i: (i, 0)
    return pl.pallas_call(
        functools.partial(_sb_in_kernel, q_scale=(d // SB_HEADS) ** -0.5 * math.log2(math.e)),
        grid=(t // tm,),
        in_specs=[
            pl.BlockSpec((tm, d), row),
            pl.BlockSpec((1, 6, d), lambda i: (i // tps, 0, 0)),
            pl.BlockSpec((d, 3 * d), lambda i: (0, 0)),
        ],
        out_specs=[pl.BlockSpec((tm, d), row)] * 3,
        out_shape=[jax.ShapeDtypeStruct((t, d), BF16)] * 3,
        compiler_params=_params("arbitrary"),
        name="sb_in",
    )(x, mod, w_in.astype(BF16))


def _sb_core_kernel(q_ref, k_ref, v_ref, o_ref, qh_ref, sp_ref, zc_ref, carry_ref, acc_ref, knorm_ref,
                    bound_ref):
    tq, width = q_ref.shape
    seq = k_ref.shape[0]
    n_heads = qh_ref.shape[0]
    dh = LANES // 2
    qi = pl.program_id(2)
    lane = lax.broadcasted_iota(jnp.int32, (tq, LANES), 1)
    half_lanes = (lane < dh, lane >= dh)
    group = [slice((hh // 2) * LANES, (hh // 2 + 1) * LANES) for hh in range(n_heads)]
    row = lax.broadcasted_iota(jnp.int32, (tq, tq), 0)
    col = lax.broadcasted_iota(jnp.int32, (tq, tq), 1)
    ones_lower = jnp.where(row >= col, 1.0, 0.0).astype(BF16)

    @pl.when(qi == 0)
    def _():
        def key_tile(t, best):
            kf = k_ref[pl.ds(pl.multiple_of(t * tq, tq), tq), :].astype(F32)
            return jnp.maximum(best, jnp.max(kf * kf, axis=0, keepdims=True))

        col_max = lax.fori_loop(0, seq // tq, key_tile, jnp.zeros((1, width), F32))
        for hh in range(n_heads):
            bound_sq = jnp.sum(jnp.where(half_lanes[hh % 2][0:1, :], col_max[:, group[hh]], 0.0),
                               axis=-1, keepdims=True)
            knorm_ref[hh] = jnp.broadcast_to(bound_sq, knorm_ref.shape[1:])

    for hh in range(n_heads):
        q = q_ref[:, group[hh]]
        qf = q.astype(F32)
        qh_ref[hh] = jnp.where(half_lanes[hh % 2], q, jnp.zeros_like(q))
        qsq = jnp.sum(jnp.where(half_lanes[hh % 2], qf * qf, 0.0), axis=-1, keepdims=True)
        bound_ref[hh] = jnp.sqrt(qsq * knorm_ref[hh, 0:1, 0:1]) * NORM_BOUND_SLACK
    acc_ref[...] = jnp.zeros(acc_ref.shape, F32)
    carry_ref[...] = jnp.zeros(carry_ref.shape, F32)

    def stage(kt, slot, diagonal=False):
        rows = pl.ds(pl.multiple_of(kt * tq, tq), tq)
        for hh in range(n_heads):
            z = _nt(qh_ref[hh], k_ref[rows, group[hh]])
            sp = jnp.maximum(z, jnp.log2(1.0 + jnp.exp2(jnp.minimum(z, EXP2_CLAMP))))
            carry = carry_ref[hh]
            zc = z - carry
            if diagonal:
                sp = jnp.where(col < row, sp, 0.0)
                zc = jnp.where(col < row, zc, -1e30)
            sp_ref[slot, hh] = sp.astype(BF16)
            zc_ref[slot, hh] = zc
            carry_ref[hh] = carry + jnp.sum(sp, axis=-1, keepdims=True)

    def consume(kt, slot):
        rows = pl.ds(pl.multiple_of(kt * tq, tq), tq)
        for hh in range(n_heads):
            inclusive = _nn(sp_ref[slot, hh], ones_lower)
            a = jnp.exp2(zc_ref[slot, hh] - inclusive)
            acc_ref[hh] += _nn(a.astype(BF16), v_ref[rows, group[hh]])

    def still_alive():
        slack = bound_ref[0] - carry_ref[0]
        for hh in range(1, n_heads):
            slack = jnp.maximum(slack, bound_ref[hh] - carry_ref[hh])
        return (jnp.max(slack) > -DEAD_EXPONENT).astype(jnp.int32)

    stage(qi, 0, diagonal=True)

    @pl.when(qi == 0)
    def _():
        consume(0, 0)

    @pl.when(qi > 0)
    def _():
        stage(qi - 1, 1)
        alive = still_alive()
        consume(qi, 0)
        n_pairs = (qi - 1) // 2

        def more(state):
            return (state[0] < n_pairs) & (state[1] > 0)

        def pair(state):
            kt = qi - 1 - 2 * state[0]
            stage(kt - 1, 0)
            consume(kt, 1)
            stage(kt - 2, 1)
            alive = still_alive()
            consume(kt - 1, 0)
            return state[0] + 1, alive

        pairs_done, alive = lax.while_loop(more, pair, (jnp.int32(0), alive))
        staged = qi - 1 - 2 * pairs_done

        @pl.when((alive == 0) | (staged == 0))
        def _():
            consume(staged, 1)

        @pl.when((alive > 0) & (staged == 1))
        def _():
            stage(0, 0)
            consume(1, 1)
            consume(0, 0)

    for hh in range(0, n_heads, 2):
        o_ref[:, group[hh]] = jnp.where(half_lanes[0], acc_ref[hh], acc_ref[hh + 1]).astype(BF16)


def _sb_core(q, k, v, *, batch, seq, tq=256, heads_per_step=8):
    t, d = q.shape
    tq = min(tq, seq)
    nq = seq // tq
    nh = heads_per_step
    width = nh // 2 * LANES
    return pl.pallas_call(
        _sb_core_kernel,
        grid=(batch, d // width, nq),
        in_specs=[
            pl.BlockSpec((tq, width), lambda b, g, i: (b * nq + i, g)),
            pl.BlockSpec((seq, width), lambda b, g, i: (b, g)),
            pl.BlockSpec((seq, width), lambda b, g, i: (b, g)),
        ],
        out_specs=pl.BlockSpec((tq, width), lambda b, g, i: (b * nq + i, g)),
        out_shape=jax.ShapeDtypeStruct((t, d), BF16),
        scratch_shapes=[
            pltpu.VMEM((nh, tq, LANES), BF16),
            pltpu.VMEM((2, nh, tq, tq), BF16),
            pltpu.VMEM((2, nh, tq, tq), F32),
            pltpu.VMEM((nh, tq, 1), F32),
            pltpu.VMEM((nh, tq, LANES), F32),
            pltpu.VMEM((nh, SUBLANES, LANES), F32),
            pltpu.VMEM((nh, tq, 1), F32),
        ],
        compiler_params=_params("arbitrary", "arbitrary", "arbitrary"),
        name="sb_core",
    )(q, k, v)


def kernel(x, c, cond_w, cond_b, ada_w, ada_b, ln_g, ln_b, ffn_up, ffn_conv_w, ffn_conv_b, ffn_down,
           gdn_w_in, gdn_conv_w, gdn_a_log, gdn_dt_bias, gdn_norm_w, gdn_w_out,
           ret_w_in, ret_w_out,
           gmlp_w_in, gmlp_ln_g, gmlp_ln_b, gmlp_w_s, gmlp_b_s, gmlp_w_out,
           sb_w_in, sb_w_out):
    batch, seq, d = x.shape
    depth = ada_w.shape[0]
    alpha = (2.0 * depth) ** 0.25
    mods = _conditioning(c, cond_w, cond_b, ada_w, ada_b).reshape(depth, batch, 6, d)
    xt = x.reshape(batch * seq, d)
    ffn_up_b = ffn_up.astype(BF16)
    ffn_down_b = ffn_down.astype(BF16)

    for i in range(depth):
        mod = mods[i]
        mixer = i % 4
        if mixer == 0:
            q, k, v, z, gb, gt = _gdn_in(xt, mod, gdn_w_in, gdn_conv_w, gdn_a_log, gdn_dt_bias, seq=seq)
            o, w_out = _gdn_core(q, k, v, z, gb, gt, gdn_norm_w, batch=batch, seq=seq), gdn_w_out
        elif mixer == 1:
            cos_t, sin_t = _rope_tables(seq, d // RET_HEADS // 2)
            q, k, v, gate = _ret_in(xt, mod, ret_w_in, cos_t, sin_t, seq=seq)
            o, w_out = _ret_core(q, k, v, gate, batch=batch, seq=seq), ret_w_out
        elif mixer == 2:
            o = _gmlp(xt, mod, gmlp_w_in, gmlp_ln_g, gmlp_ln_b, gmlp_w_s, gmlp_b_s, seq=seq)
            w_out = gmlp_w_out
        else:
            q, k, v = _sb_in(xt, mod, sb_w_in, seq=seq)
            o, w_out = _sb_core(q, k, v, batch=batch, seq=seq), sb_w_out
        xt = _sublayer_tail(o, w_out, xt, mod, ffn_up_b, ffn_conv_w[i], ffn_conv_b[i], ffn_down_b,
                            ln_g, ln_b, layer=i, seq=seq, alpha=alpha)
    return xt.reshape(batch, seq, d)
```

```python
import functools
import math

import jax
import jax.numpy as jnp
from jax import lax
from jax.experimental import pallas as pl
from jax.experimental.pallas import tpu as pltpu

F32 = jnp.float32
BF16 = jnp.bfloat16

LANES = 128
SUBLANES = 8
VMEM_LIMIT = 56 * 1024 * 1024

LN_EPS = 1e-5
GDN_HEADS = 8
GDN_CHUNK = 64
GDN_CONV = 4
RET_HEADS = 4
RET_CHUNK = 128
RET_ROPE_BASE = 10000.0
GMLP_CHUNK = 128
GMLP_GROUPS = 8
SB_HEADS = 16
FFN_CONV = 3
DEAD_EXPONENT = 160.0
NORM_BOUND_SLACK = 1.001
EXP2_CLAMP = 64.0


def _params(*sem):
    return pltpu.CompilerParams(dimension_semantics=sem, vmem_limit_bytes=VMEM_LIMIT)


def _nn(a, b):
    return jnp.dot(a, b, preferred_element_type=F32)


def _nt(a, b):
    return lax.dot_general(a, b, (((1,), (1,)), ((), ())), preferred_element_type=F32)


def _tn(a, b):
    return lax.dot_general(a, b, (((0,), (0,)), ((), ())), preferred_element_type=F32)


def _sigmoid(x):
    return 1.0 / (1.0 + jnp.exp(-x))


def _silu(x):
    return x * _sigmoid(x)


def _softplus(x):
    return jnp.maximum(x, 0.0) + jnp.log(1.0 + jnp.exp(-jnp.abs(x)))


def _split3(x):
    hi = x.astype(BF16)
    r = x - hi.astype(F32)
    mid = r.astype(BF16)
    lo = (r - mid.astype(F32)).astype(BF16)
    return hi, mid, lo


def _modulate(x, mod_ref, shift_row, scale_row):
    return (x * (1.0 + mod_ref[0, scale_row:scale_row + 1, :])
            + mod_ref[0, shift_row:shift_row + 1, :]).astype(BF16)


def _layer_norm_rows(r, g, b, eps):
    mu = jnp.mean(r, axis=-1, keepdims=True)
    rc = r - mu
    var = jnp.mean(rc * rc, axis=-1, keepdims=True)
    return rc * lax.rsqrt(var + eps) * g + b


def _lane_replicated_columns(rows):
    n, k = rows.shape
    padded = jnp.concatenate([rows, jnp.zeros((LANES - n, k), F32)], axis=0)
    t = padded.T
    return [jnp.broadcast_to(t[:, b:b + 1], (k, LANES)) for b in range(n)]


def _rowvec_matmul(col, w):
    n = w.shape[1]
    parts = [jnp.sum(col * w[:, c:c + LANES], axis=0, keepdims=True) for c in range(0, n, LANES)]
    return jnp.concatenate(parts, axis=1)


def _cond_kernel(c_ref, cw_ref, cb_ref, aw_ref, ab_ref, o_ref, ecol_ref):
    nb = c_ref.shape[0]

    @pl.when((pl.program_id(0) == 0) & (pl.program_id(1) == 0))
    def _():
        ccols = _lane_replicated_columns(c_ref[...])
        cw = cw_ref[...]
        e = jnp.concatenate([_rowvec_matmul(col, cw) for col in ccols], axis=0) + cb_ref[...]
        ecols = _lane_replicated_columns(_silu(e))
        for b in range(nb):
            ecol_ref[b] = ecols[b]

    w = aw_ref[0]
    for b in range(nb):
        o_ref[0, b:b + 1, :] = _rowvec_matmul(ecol_ref[b], w) + ab_ref[0]


def _conditioning(c, cond_w, cond_b, ada_w, ada_b):
    nb, d = c.shape
    depth, _, n6 = ada_w.shape
    tn = 1536
    return pl.pallas_call(
        _cond_kernel,
        grid=(depth, n6 // tn),
        in_specs=[
            pl.BlockSpec((nb, d), lambda i, j: (0, 0)),
            pl.BlockSpec((d, d), lambda i, j: (0, 0)),
            pl.BlockSpec((1, d), lambda i, j: (0, 0)),
            pl.BlockSpec((1, d, tn), lambda i, j: (i, 0, j)),
            pl.BlockSpec((1, 1, tn), lambda i, j: (i, 0, j)),
        ],
        out_specs=pl.BlockSpec((1, nb, tn), lambda i, j: (i, 0, j)),
        out_shape=jax.ShapeDtypeStruct((depth, nb, n6), F32),
        scratch_shapes=[pltpu.VMEM((nb, d, LANES), F32)],
        compiler_params=_params("arbitrary", "arbitrary"),
        name="conditioning",
    )(c, cond_w, cond_b.reshape(1, d), ada_w, ada_b.reshape(depth, 1, n6))


def _ffn_kernel(o_ref, wo_ref, g1_ref, b1_ref, x_ref, mod_ref, wup_ref, cw_ref, cb_ref, wd_ref,
                g_ref, b_ref, out_ref, x1_ref, h_ref, buf_ref, act_ref, *, tiles_per_seq, alpha):
    i = pl.program_id(0)
    tm = x_ref.shape[0]
    f = wd_ref.shape[0]
    n_chunks, _, tf = buf_ref.shape

    @pl.when(i % tiles_per_seq == 0)
    def _():
        for c in range(n_chunks):
            buf_ref[c, 0:SUBLANES, :] = jnp.zeros((SUBLANES, tf), F32)

    half = tm // 2
    halves = (slice(0, half), slice(half, tm))
    for rs in halves:
        mixed = alpha * x_ref[rs, :] + (1.0 + mod_ref[0, 2:3, :]) * _nn(o_ref[rs, :], wo_ref[...])
        x1 = _layer_norm_rows(mixed, g1_ref[...], b1_ref[...], LN_EPS)
        x1_ref[rs, :] = x1
        h_ref[rs, :] = _modulate(x1, mod_ref, 3, 4)

    for c in range(n_chunks):
        cols = slice(c * tf, (c + 1) * tf)
        buf_ref[c, SUBLANES:SUBLANES + tm, :] = _nn(h_ref[...], wup_ref[:, cols])
        conv = cb_ref[:, cols]
        for tap in range(FFN_CONV):
            off = SUBLANES - (FFN_CONV - 1) + tap
            conv = conv + cw_ref[tap:tap + 1, cols] * buf_ref[c, off:off + tm, :]
        buf_ref[c, 0:SUBLANES, :] = buf_ref[c, tm:tm + SUBLANES, :]
        up = _nn(h_ref[...], wup_ref[:, f + c * tf:f + (c + 1) * tf])
        act_ref[:, cols] = (_silu(conv) * up).astype(BF16)

    for rs in halves:
        y = _nn(act_ref[rs, :], wd_ref[...])
        r = alpha * x1_ref[rs, :] + (1.0 + mod_ref[0, 5:6, :]) * y
        out_ref[rs, :] = _layer_norm_rows(r, g_ref[...], b_ref[...], LN_EPS)


def _resident(shape, layer=None):
    if layer is None:
        return pl.BlockSpec(shape, lambda *_: (0,) * len(shape), pipeline_mode=pl.Buffered(1))
    return pl.BlockSpec((None,) + tuple(shape), lambda *_: (layer,) + (0,) * len(shape),
                        pipeline_mode=pl.Buffered(1))


def _sublayer_tail(o, w_out, x, mod, w_up, conv_w, conv_b, w_down, ln_g, ln_b, *, layer, seq, alpha,
                   tm=512, n_chunks=11):
    t, d = x.shape
    kdim = o.shape[1]
    f = w_down.shape[1]
    tf = f // n_chunks
    tps = seq // tm
    rows = lambda i: (i, 0)
    return pl.pallas_call(
        functools.partial(_ffn_kernel, tiles_per_seq=tps, alpha=alpha),
        grid=(t // tm,),
        in_specs=[
            pl.BlockSpec((tm, kdim), rows),
            _resident((kdim, d)),
            _resident((1, d)),
            _resident((1, d)),
            pl.BlockSpec((tm, d), rows),
            pl.BlockSpec((1, 6, d), lambda i: (i // tps, 0, 0)),
            _resident((d, 2 * f), layer),
            _resident((FFN_CONV, f)),
            _resident((1, f)),
            _resident((f, d), layer),
            _resident((1, d)),
            _resident((1, d)),
        ],
        out_specs=pl.BlockSpec((tm, d), rows),
        out_shape=jax.ShapeDtypeStruct((t, d), F32),
        scratch_shapes=[
            pltpu.VMEM((tm, d), F32),
            pltpu.VMEM((tm, d), BF16),
            pltpu.VMEM((n_chunks, tm + SUBLANES, tf), F32),
            pltpu.VMEM((tm, f), BF16),
        ],
        compiler_params=_params("arbitrary"),
        name="sublayer_tail",
    )(o, w_out.astype(BF16), ln_g[layer, 0].reshape(1, d), ln_b[layer, 0].reshape(1, d), x, mod,
      w_up, conv_w, conv_b.reshape(1, f), w_down, ln_g[layer, 1].reshape(1, d), ln_b[layer, 1].reshape(1, d))


def _gdn_in_kernel(x_ref, mod_ref, wqkv_ref, wz_ref, wab_ref, cw_ref, alog_ref, dtb_ref,
                   q_ref, k_ref, v_ref, z_ref, gb_ref, gt_ref, buf_ref, *, tiles_per_seq, chunk):
    i = pl.program_id(0)
    tm, d = x_ref.shape
    nh = GDN_HEADS
    hd = d // nh
    h = _modulate(x_ref[...], mod_ref, 0, 1)

    @pl.when(i % tiles_per_seq == 0)
    def _():
        buf_ref[0:SUBLANES, :] = jnp.zeros((SUBLANES, buf_ref.shape[1]), F32)

    outs = (q_ref, k_ref, v_ref)
    for s in range(3):
        cs = slice(s * d, (s + 1) * d)
        buf_ref[SUBLANES:SUBLANES + tm, cs] = _nn(h, wqkv_ref[:, cs])
        y = None
        for tap in range(GDN_CONV):
            off = SUBLANES - (GDN_CONV - 1) + tap
            term = cw_ref[tap:tap + 1, cs] * buf_ref[off:off + tm, cs]
            y = term if y is None else y + term
        y = _silu(y)
        if s < 2:
            scale = hd ** -0.5 if s == 0 else 1.0
            for hh in range(nh):
                seg = y[:, hh * hd:(hh + 1) * hd]
                inv = lax.rsqrt(jnp.sum(seg * seg, axis=-1, keepdims=True) + 1e-6)
                outs[s][:, hh * hd:(hh + 1) * hd] = (seg * (inv * scale)).astype(BF16)
        else:
            outs[s][...] = y.astype(BF16)
    buf_ref[0:SUBLANES, :] = buf_ref[tm:tm + SUBLANES, :]

    z_ref[...] = _nn(h, wz_ref[...]).astype(BF16)

    pab = _nn(h, wab_ref[...])
    lane = lax.broadcasted_iota(jnp.int32, pab.shape, 1)
    g = -jnp.exp(alog_ref[...]) * _softplus(pab + dtb_ref[...])
    gb = jnp.where(lane < nh, g, jnp.where(lane < 2 * nh, _sigmoid(pab), 0.0))
    row = lax.broadcasted_iota(jnp.int32, (tm, tm), 0)
    col = lax.broadcasted_iota(jnp.int32, (tm, tm), 1)
    tri = jnp.where(col <= row, jnp.where(jnp.bitwise_xor(row, col) < chunk, 1.0, 0.0), 0.0).astype(BF16)
    hi, mid, lo = _split3(gb)
    cum = _nn(tri, hi) + _nn(tri, mid) + _nn(tri, lo)
    gb = jnp.where(lane < nh, cum, gb)
    gb_ref[...] = gb
    gt_ref[...] = gb.T[0:2 * nh, :]


def _gdn_in(x, mod, w_in, conv_w, a_log, dt_bias, *, seq, tm=256):
    t, d = x.shape
    nh = GDN_HEADS
    tps = seq // tm
    w_b = w_in.astype(BF16)
    w_ab = jnp.pad(w_b[:, 4 * d:], ((0, 0), (0, LANES - 2 * nh)))
    once = pl.Buffered(1)
    alog = jnp.pad(a_log, (0, LANES - nh)).reshape(1, LANES)
    dtb = jnp.pad(dt_bias, (0, LANES - nh)).reshape(1, LANES)
    row = lambda i: (i, 0)
    fixed = lambda i: (0, 0)
    return pl.pallas_call(
        functools.partial(_gdn_in_kernel, tiles_per_seq=tps, chunk=GDN_CHUNK),
        grid=(t // tm,),
        in_specs=[
            pl.BlockSpec((tm, d), row),
            pl.BlockSpec((1, 6, d), lambda i: (i // tps, 0, 0)),
            pl.BlockSpec((d, 3 * d), fixed, pipeline_mode=once),
            pl.BlockSpec((d, d), lambda i: (0, 3), pipeline_mode=once),
            pl.BlockSpec((d, LANES), fixed, pipeline_mode=once),
            pl.BlockSpec((GDN_CONV, 3 * d), fixed, pipeline_mode=once),
            pl.BlockSpec((1, LANES), fixed),
            pl.BlockSpec((1, LANES), fixed),
        ],
        out_specs=[pl.BlockSpec((tm, d), row)] * 4 + [pl.BlockSpec((tm, LANES), row),
                                                      pl.BlockSpec((2 * nh, tm), lambda i: (0, i))],
        out_shape=[jax.ShapeDtypeStruct((t, d), BF16)] * 4 + [jax.ShapeDtypeStruct((t, LANES), F32),
                                                              jax.ShapeDtypeStruct((2 * nh, t), F32)],
        scratch_shapes=[pltpu.VMEM((tm + SUBLANES, 3 * d), F32)],
        compiler_params=_params("arbitrary"),
        name="gdn_in",
    )(x, mod, w_b, w_b, w_ab, conv_w, alog, dtb)


def _gdn_core_kernel(q_ref, k_ref, v_ref, z_ref, gb_ref, gt_ref, nw_ref, o_ref,
                     state_ref, inv_ref, pw_ref, rhs_ref, u_ref, w_ref, qk_ref, qg_ref, kd_ref, dl_ref,
                     vn_ref, oi_ref, *, chunk):
    rows, d = q_ref.shape
    nh = GDN_HEADS
    hd = d // nh
    n_chunks = rows // chunk
    heads = range(nh)
    lanes = [slice(hh * hd, (hh + 1) * hd) for hh in heads]

    @pl.when(pl.program_id(1) == 0)
    def _():
        state_ref[...] = jnp.zeros(state_ref.shape, F32)

    row = lax.broadcasted_iota(jnp.int32, (rows, rows), 0)
    col = lax.broadcasted_iota(jnp.int32, (rows, rows), 1)
    same_chunk = jnp.bitwise_xor(row, col) < chunk
    gb = gb_ref[...]
    for hh in heads:
        kh = k_ref[:, lanes[hh]]
        kf = kh.astype(F32)
        qh = q_ref[:, lanes[hh]]
        gc = jnp.broadcast_to(gb[:, hh:hh + 1], (rows, hd))
        beta = jnp.broadcast_to(gb[:, nh + hh:nh + hh + 1], (rows, hd))
        diff = jnp.broadcast_to(gb[:, hh:hh + 1], (rows, rows)) - gt_ref[hh:hh + 1, :]
        causal = same_chunk & (row >= col)
        decay = jnp.where(causal, jnp.exp(jnp.where(causal, diff, 0.0)), 0.0)
        kb = kf * beta
        a = jnp.where(row > col, _nt(kb.astype(BF16), kh) * decay, 0.0)
        inv_ref[hh] = jnp.where(row == col, 1.0, 0.0) - a
        pw_ref[hh] = a.astype(BF16)
        qk_ref[hh] = (_nt(qh, kh) * decay).astype(BF16)
        rhs_ref[hh, :, 0:hd] = (v_ref[:, lanes[hh]].astype(F32) * beta).astype(BF16)
        rhs_ref[hh, :, hd:2 * hd] = (kb * jnp.exp(gc)).astype(BF16)
        qg_ref[hh] = (qh.astype(F32) * jnp.exp(gc)).astype(BF16)
        for c in range(n_chunks):
            rs = slice(c * chunk, (c + 1) * chunk)
            g_last = gc[(c + 1) * chunk - 1:(c + 1) * chunk, :]
            kd_ref[hh, rs, :] = (kf[rs] * jnp.exp(g_last - gc[rs])).astype(BF16)
            dl_ref[hh, c] = jnp.exp(g_last)

    for _ in range(int(math.log2(chunk)) - 1):
        for hh in heads:
            power = _nn(pw_ref[hh], pw_ref[hh]).astype(BF16)
            pw_ref[hh] = power
            inv = inv_ref[hh]
            inv_ref[hh] = inv + _nn(inv.astype(BF16), power)
    for hh in heads:
        sol = _nn(inv_ref[hh].astype(BF16), rhs_ref[hh])
        u_ref[hh] = sol[:, 0:hd]
        w_ref[hh] = sol[:, hd:2 * hd].astype(BF16)

    for c in range(n_chunks):
        rs = slice(c * chunk, (c + 1) * chunk)
        states = [state_ref[hh].astype(BF16) for hh in heads]
        v_new = [(u_ref[hh, rs, :] - _nn(w_ref[hh, rs, :], states[hh])).astype(BF16) for hh in heads]
        for hh in heads:
            vn_ref[hh, rs, :] = v_new[hh]
            state_ref[hh] = state_ref[hh] * dl_ref[hh, c] + _tn(kd_ref[hh, rs, :], v_new[hh])
            oi_ref[hh, rs, :] = _nn(qg_ref[hh, rs, :], states[hh])

    for hh in heads:
        o = oi_ref[hh] + _nn(qk_ref[hh], vn_ref[hh])
        o = o * lax.rsqrt(jnp.mean(o * o, axis=-1, keepdims=True) + 1e-6) * nw_ref[...]
        o_ref[:, lanes[hh]] = (o * _silu(z_ref[:, lanes[hh]].astype(F32))).astype(BF16)


def _gdn_core(q, k, v, z, gb, gt, norm_w, *, batch, seq, rows=256):
    t, d = q.shape
    n = seq // rows
    nh = GDN_HEADS
    hd = d // nh
    row = lambda b, j: (b * n + j, 0)
    return pl.pallas_call(
        functools.partial(_gdn_core_kernel, chunk=GDN_CHUNK),
        grid=(batch, n),
        in_specs=[pl.BlockSpec((rows, d), row)] * 4 + [
            pl.BlockSpec((rows, LANES), row),
            pl.BlockSpec((2 * nh, rows), lambda b, j: (0, b * n + j)),
            pl.BlockSpec((1, hd), lambda b, j: (0, 0)),
        ],
        out_specs=pl.BlockSpec((rows, d), row),
        out_shape=jax.ShapeDtypeStruct((t, d), BF16),
        scratch_shapes=[
            pltpu.VMEM((nh, hd, hd), F32),
            pltpu.VMEM((nh, rows, rows), F32),
            pltpu.VMEM((nh, rows, rows), BF16),
            pltpu.VMEM((nh, rows, 2 * hd), BF16),
            pltpu.VMEM((nh, rows, hd), F32),
            pltpu.VMEM((nh, rows, hd), BF16),
            pltpu.VMEM((nh, rows, rows), BF16),
            pltpu.VMEM((nh, rows, hd), BF16),
            pltpu.VMEM((nh, rows, hd), BF16),
            pltpu.VMEM((nh, rows // GDN_CHUNK, 1, hd), F32),
            pltpu.VMEM((nh, rows, hd), BF16),
            pltpu.VMEM((nh, rows, hd), F32),
        ],
        compiler_params=_params("arbitrary", "arbitrary"),
        name="gdn_core",
    )(q, k, v, z, gb, gt, norm_w.reshape(1, hd))


def _rope_table_kernel(cos_ref, sin_ref):
    ts, half = cos_ref.shape
    pos = (lax.broadcasted_iota(jnp.int32, (ts, half), 0) + pl.program_id(0) * ts).astype(F32)
    frac = lax.broadcasted_iota(jnp.int32, (ts, half), 1).astype(F32) / (half - 1.0)
    ang = pos * jnp.exp(-frac * math.log(RET_ROPE_BASE))
    cos_ref[...] = jnp.cos(ang)
    sin_ref[...] = jnp.sin(ang)


def _rope_tables(seq, half, ts=256):
    spec = pl.BlockSpec((ts, half), lambda i: (i, 0))
    return pl.pallas_call(
        _rope_table_kernel,
        grid=(seq // ts,),
        out_specs=[spec, spec],
        out_shape=[jax.ShapeDtypeStruct((seq, half), F32)] * 2,
        compiler_params=_params("arbitrary"),
        name="rope_tables",
    )()


def _ret_in_kernel(x_ref, mod_ref, w_ref, cos_ref, sin_ref, q_ref, k_ref, v_ref, gate_ref):
    d = x_ref.shape[1]
    nh = RET_HEADS
    dk = d // nh
    half = dk // 2
    h = _modulate(x_ref[...], mod_ref, 0, 1)
    cos_a = cos_ref[...]
    sin_a = sin_ref[...]
    for s, (out, scale) in enumerate(((q_ref, 1.0), (k_ref, dk ** -0.5))):
        t = _nn(h, w_ref[:, s * d:(s + 1) * d])
        for hh in range(nh):
            t1 = t[:, hh * dk:hh * dk + half]
            t2 = t[:, hh * dk + half:(hh + 1) * dk]
            out[:, hh * dk:hh * dk + half] = ((t1 * cos_a - t2 * sin_a) * scale).astype(BF16)
            out[:, hh * dk + half:(hh + 1) * dk] = ((t1 * sin_a + t2 * cos_a) * scale).astype(BF16)
    v_ref[...] = _nn(h, w_ref[:, 2 * d:4 * d]).astype(BF16)
    gate_ref[...] = _silu(_nn(h, w_ref[:, 4 * d:6 * d])).astype(BF16)


def _ret_in(x, mod, w_in, cos_t, sin_t, *, seq, tm=512):
    t, d = x.shape
    tps = seq // tm
    half = cos_t.shape[1]
    row = lambda i: (i, 0)
    return pl.pallas_call(
        _ret_in_kernel,
        grid=(t // tm,),
        in_specs=[
            pl.BlockSpec((tm, d), row),
            pl.BlockSpec((1, 6, d), lambda i: (i // tps, 0, 0)),
            _resident((d, 6 * d)),
            pl.BlockSpec((tm, half), lambda i: (i % tps, 0)),
            pl.BlockSpec((tm, half), lambda i: (i % tps, 0)),
        ],
        out_specs=[pl.BlockSpec((tm, d), row), pl.BlockSpec((tm, d), row),
                   pl.BlockSpec((tm, 2 * d), row), pl.BlockSpec((tm, 2 * d), row)],
        out_shape=[jax.ShapeDtypeStruct((t, d), BF16), jax.ShapeDtypeStruct((t, d), BF16),
                   jax.ShapeDtypeStruct((t, 2 * d), BF16), jax.ShapeDtypeStruct((t, 2 * d), BF16)],
        compiler_params=_params("arbitrary"),
        name="ret_in",
    )(x, mod, w_in.astype(BF16), cos_t, sin_t)


def _ret_core_kernel(q_ref, k_ref, v_ref, gate_ref, o_ref, state_ref, *, chunk):
    rows, d = q_ref.shape
    c = chunk
    nh = RET_HEADS
    dk = d // nh
    dv = v_ref.shape[1] // nh

    @pl.when(pl.program_id(1) == 0)
    def _():
        state_ref[...] = jnp.zeros(state_ref.shape, F32)

    rel = (lax.broadcasted_iota(jnp.int32, (c, c), 0)
           - lax.broadcasted_iota(jnp.int32, (c, c), 1)).astype(F32)
    idx = lax.broadcasted_iota(jnp.int32, (c, 1), 0).astype(F32)
    for hh in range(nh):
        log_gamma = math.log(1.0 - 2.0 ** (-5.0 - hh))
        dmask = jnp.where(rel >= 0, jnp.exp(jnp.maximum(rel, 0.0) * log_gamma), 0.0)
        zeta = jnp.exp((c - 1.0 - idx) * log_gamma)
        xi = jnp.exp((idx + 1.0) * log_gamma)
        state = state_ref[hh]
        for ci in range(rows // c):
            rs = slice(ci * c, (ci + 1) * c)
            qh = q_ref[rs, hh * dk:(hh + 1) * dk]
            kh = k_ref[rs, hh * dk:(hh + 1) * dk]
            vh = v_ref[rs, hh * dv:(hh + 1) * dv]
            scores = _nt(qh, kh) * dmask
            o = _nn(scores.astype(BF16), vh) + _nn(qh, state.astype(BF16)) * xi
            state = state * math.exp(c * log_gamma) + _tn((kh.astype(F32) * zeta).astype(BF16), vh)
            mu = jnp.mean(o, axis=-1, keepdims=True)
            oc = o - mu
            var = jnp.mean(oc * oc, axis=-1, keepdims=True)
            o = oc * lax.rsqrt(var + 1e-6)
            gate = gate_ref[rs, hh * dv:(hh + 1) * dv].astype(F32)
            o_ref[rs, hh * dv:(hh + 1) * dv] = (o * gate).astype(BF16)
        state_ref[hh] = state


def _ret_core(q, k, v, gate, *, batch, seq, rows=512):
    t, d = q.shape
    c = rows
    n = seq // rows
    dk = d // RET_HEADS
    dv = v.shape[1] // RET_HEADS
    row = lambda b, j: (b * n + j, 0)
    return pl.pallas_call(
        functools.partial(_ret_core_kernel, chunk=RET_CHUNK),
        grid=(batch, n),
        in_specs=[pl.BlockSpec((c, d), row), pl.BlockSpec((c, d), row),
                  pl.BlockSpec((c, 2 * d), row), pl.BlockSpec((c, 2 * d), row)],
        out_specs=pl.BlockSpec((c, 2 * d), row),
        out_shape=jax.ShapeDtypeStruct((t, 2 * d), BF16),
        scratch_shapes=[pltpu.VMEM((RET_HEADS, dk, dv), F32)],
        compiler_params=_params("arbitrary", "arbitrary"),
        name="ret_core",
    )(q, k, v, gate)


def _gelu(t):
    p = 0.3275911 / math.sqrt(2.0)
    half_coefs = [0.5 * a for a in (0.254829592, -0.284496736, 1.421413741, -1.453152027, 1.061405429)]
    u = 1.0 / (1.0 + p * jnp.abs(t))
    poly = half_coefs[4]
    for coef in half_coefs[3::-1]:
        poly = poly * u + coef
    e = poly * u * jnp.exp2(t * t * (-0.5 * math.log2(math.e)))
    return t * jnp.where(t >= 0.0, 1.0 - e, e)


def _gmlp_kernel(x_ref, mod_ref, win_ref, lng_ref, lnb_ref, ws_ref, bs_ref, out_ref):
    tm, d = x_ref.shape
    width = win_ref.shape[1] // 2
    c = GMLP_CHUNK
    ng = GMLP_GROUPS
    gw = width // ng
    h = _modulate(x_ref[...], mod_ref, 0, 1)

    v = _gelu(_nn(h, win_ref[:, width:]))
    v = _layer_norm_rows(v, lng_ref[...], lnb_ref[...], LN_EPS).astype(BF16)
    u = _gelu(_nn(h, win_ref[:, :width]))
    row = lax.broadcasted_iota(jnp.int32, (c, c), 0)
    col = lax.broadcasted_iota(jnp.int32, (c, c), 1)
    for n in range(tm // c):
        rs = slice(n * c, (n + 1) * c)
        for gi in range(ng):
            ls = slice(gi * gw, (gi + 1) * gw)
            ws = jnp.where(row >= col, ws_ref[gi], 0.0).astype(BF16)
            vs = _nn(ws, v[rs, ls]) + bs_ref[:, gi:gi + 1]
            out_ref[rs, ls] = (u[rs, ls] * vs).astype(BF16)


def _gmlp(x, mod, w_in, ln_g, ln_b, w_s, b_s, *, seq, tm=512):
    t, d = x.shape
    width = w_in.shape[1] // 2
    ng, c, _ = w_s.shape
    tps = seq // tm
    return pl.pallas_call(
        _gmlp_kernel,
        grid=(t // tm,),
        in_specs=[
            pl.BlockSpec((tm, d), lambda i: (i, 0)),
            pl.BlockSpec((1, 6, d), lambda i: (i // tps, 0, 0)),
            _resident((d, 2 * width)),
            _resident((1, width)),
            _resident((1, width)),
            _resident((ng, c, c)),
            _resident((c, ng)),
        ],
        out_specs=pl.BlockSpec((tm, width), lambda i: (i, 0)),
        out_shape=jax.ShapeDtypeStruct((t, width), BF16),
        compiler_params=_params("arbitrary"),
        name="gmlp",
    )(x, mod, w_in.astype(BF16), ln_g.reshape(1, width), ln_b.reshape(1, width), w_s, b_s.T)


def _sb_in_kernel(x_ref, mod_ref, w_ref, q_ref, k_ref, v_ref, *, q_scale):
    d = x_ref.shape[1]
    h = _modulate(x_ref[...], mod_ref, 0, 1)
    q_ref[...] = (_nn(h, w_ref[:, 0:d]) * q_scale).astype(BF16)
    k_ref[...] = _nn(h, w_ref[:, d:2 * d]).astype(BF16)
    v_ref[...] = _nn(h, w_ref[:, 2 * d:3 * d]).astype(BF16)


def _sb_in(x, mod, w_in, *, seq, tm=512):
    t, d = x.shape
    tps = seq // tm
    row = lambda i: (i, 0)
    return pl.pallas_call(
        functools.partial(_sb_in_kernel, q_scale=(d // SB_HEADS) ** -0.5 * math.log2(math.e)),
        grid=(t // tm,),
        in_specs=[
            pl.BlockSpec((tm, d), row),
            pl.BlockSpec((1, 6, d), lambda i: (i // tps, 0, 0)),
            pl.BlockSpec((d, 3 * d), lambda i: (0, 0)),
        ],
        out_specs=[pl.BlockSpec((tm, d), row)] * 3,
        out_shape=[jax.ShapeDtypeStruct((t, d), BF16)] * 3,
        compiler_params=_params("arbitrary"),
        name="sb_in",
    )(x, mod, w_in.astype(BF16))


def _sb_core_kernel(q_ref, k_ref, v_ref, o_ref, qh_ref, sp_ref, zc_ref, carry_ref, acc_ref, knorm_ref,
                    bound_ref):
    tq, width = q_ref.shape
    seq = k_ref.shape[0]
    n_heads = qh_ref.shape[0]
    dh = LANES // 2
    qi = pl.program_id(2)
    lane = lax.broadcasted_iota(jnp.int32, (tq, LANES), 1)
    half_lanes = (lane < dh, lane >= dh)
    group = [slice((hh // 2) * LANES, (hh // 2 + 1) * LANES) for hh in range(n_heads)]
    row = lax.broadcasted_iota(jnp.int32, (tq, tq), 0)
    col = lax.broadcasted_iota(jnp.int32, (tq, tq), 1)
    ones_lower = jnp.where(row >= col, 1.0, 0.0).astype(BF16)

    @pl.when(qi == 0)
    def _():
        def key_tile(t, best):
            kf = k_ref[pl.ds(pl.multiple_of(t * tq, tq), tq), :].astype(F32)
            return jnp.maximum(best, jnp.max(kf * kf, axis=0, keepdims=True))

        col_max = lax.fori_loop(0, seq // tq, key_tile, jnp.zeros((1, width), F32))
        for hh in range(n_heads):
            bound_sq = jnp.sum(jnp.where(half_lanes[hh % 2][0:1, :], col_max[:, group[hh]], 0.0),
                               axis=-1, keepdims=True)
            knorm_ref[hh] = jnp.broadcast_to(bound_sq, knorm_ref.shape[1:])

    for hh in range(n_heads):
        q = q_ref[:, group[hh]]
        qf = q.astype(F32)
        qh_ref[hh] = jnp.where(half_lanes[hh % 2], q, jnp.zeros_like(q))
        qsq = jnp.sum(jnp.where(half_lanes[hh % 2], qf * qf, 0.0), axis=-1, keepdims=True)
        bound_ref[hh] = jnp.sqrt(qsq * knorm_ref[hh, 0:1, 0:1]) * NORM_BOUND_SLACK
    acc_ref[...] = jnp.zeros(acc_ref.shape, F32)
    carry_ref[...] = jnp.zeros(carry_ref.shape, F32)

    def stage(kt, slot, diagonal=False):
        rows = pl.ds(pl.multiple_of(kt * tq, tq), tq)
        for hh in range(n_heads):
            z = _nt(qh_ref[hh], k_ref[rows, group[hh]])
            sp = jnp.maximum(z, jnp.log2(1.0 + jnp.exp2(jnp.minimum(z, EXP2_CLAMP))))
            carry = carry_ref[hh]
            zc = z - carry
            if diagonal:
                sp = jnp.where(col < row, sp, 0.0)
                zc = jnp.where(col < row, zc, -1e30)
            sp_ref[slot, hh] = sp.astype(BF16)
            zc_ref[slot, hh] = zc
            carry_ref[hh] = carry + jnp.sum(sp, axis=-1, keepdims=True)

    def consume(kt, slot):
        rows = pl.ds(pl.multiple_of(kt * tq, tq), tq)
        for hh in range(n_heads):
            inclusive = _nn(sp_ref[slot, hh], ones_lower)
            a = jnp.exp2(zc_ref[slot, hh] - inclusive)
            acc_ref[hh] += _nn(a.astype(BF16), v_ref[rows, group[hh]])

    def still_alive():
        slack = bound_ref[0] - carry_ref[0]
        for hh in range(1, n_heads):
            slack = jnp.maximum(slack, bound_ref[hh] - carry_ref[hh])
        return (jnp.max(slack) > -DEAD_EXPONENT).astype(jnp.int32)

    stage(qi, 0, diagonal=True)

    @pl.when(qi == 0)
    def _():
        consume(0, 0)

    @pl.when(qi > 0)
    def _():
        stage(qi - 1, 1)
        alive = still_alive()
        consume(qi, 0)
        n_pairs = (qi - 1) // 2

        def more(state):
            return (state[0] < n_pairs) & (state[1] > 0)

        def pair(state):
            kt = qi - 1 - 2 * state[0]
            stage(kt - 1, 0)
            consume(kt, 1)
            stage(kt - 2, 1)
            alive = still_alive()
            consume(kt - 1, 0)
            return state[0] + 1, alive

        pairs_done, alive = lax.while_loop(more, pair, (jnp.int32(0), alive))
        staged = qi - 1 - 2 * pairs_done

        @pl.when((alive == 0) | (staged == 0))
        def _():
            consume(staged, 1)

        @pl.when((alive > 0) & (staged == 1))
        def _():
            stage(0, 0)
            consume(1, 1)
            consume(0, 0)

    for hh in range(0, n_heads, 2):
        o_ref[:, group[hh]] = jnp.where(half_lanes[0], acc_ref[hh], acc_ref[hh + 1]).astype(BF16)


def _sb_core(q, k, v, *, batch, seq, tq=256, heads_per_step=8):
    t, d = q.shape
    tq = min(tq, seq)
    nq = seq // tq
    nh = heads_per_step
    width = nh // 2 * LANES
    return pl.pallas_call(
        _sb_core_kernel,
        grid=(batch, d // width, nq),
        in_specs=[
            pl.BlockSpec((tq, width), lambda b, g, i: (b * nq + i, g)),
            pl.BlockSpec((seq, width), lambda b, g, i: (b, g)),
            pl.BlockSpec((seq, width), lambda b, g, i: (b, g)),
        ],
        out_specs=pl.BlockSpec((tq, width), lambda b, g, i: (b * nq + i, g)),
        out_shape=jax.ShapeDtypeStruct((t, d), BF16),
        scratch_shapes=[
            pltpu.VMEM((nh, tq, LANES), BF16),
            pltpu.VMEM((2, nh, tq, tq), BF16),
            pltpu.VMEM((2, nh, tq, tq), F32),
            pltpu.VMEM((nh, tq, 1), F32),
            pltpu.VMEM((nh, tq, LANES), F32),
            pltpu.VMEM((nh, SUBLANES, LANES), F32),
            pltpu.VMEM((nh, tq, 1), F32),
        ],
        compiler_params=_params("arbitrary", "arbitrary", "arbitrary"),
        name="sb_core",
    )(q, k, v)


def kernel(x, c, cond_w, cond_b, ada_w, ada_b, ln_g, ln_b, ffn_up, ffn_conv_w, ffn_conv_b, ffn_down,
           gdn_w_in, gdn_conv_w, gdn_a_log, gdn_dt_bias, gdn_norm_w, gdn_w_out,
           ret_w_in, ret_w_out,
           gmlp_w_in, gmlp_ln_g, gmlp_ln_b, gmlp_w_s, gmlp_b_s, gmlp_w_out,
           sb_w_in, sb_w_out):
    batch, seq, d = x.shape
    depth = ada_w.shape[0]
    alpha = (2.0 * depth) ** 0.25
    mods = _conditioning(c, cond_w, cond_b, ada_w, ada_b).reshape(depth, batch, 6, d)
    xt = x.reshape(batch * seq, d)
    ffn_up_b = ffn_up.astype(BF16)
    ffn_down_b = ffn_down.astype(BF16)

    for i in range(depth):
        mod = mods[i]
        mixer = i % 4
        if mixer == 0:
            q, k, v, z, gb, gt = _gdn_in(xt, mod, gdn_w_in, gdn_conv_w, gdn_a_log, gdn_dt_bias, seq=seq)
            o, w_out = _gdn_core(q, k, v, z, gb, gt, gdn_norm_w, batch=batch, seq=seq), gdn_w_out
        elif mixer == 1:
            cos_t, sin_t = _rope_tables(seq, d // RET_HEADS // 2)
            q, k, v, gate = _ret_in(xt, mod, ret_w_in, cos_t, sin_t, seq=seq)
            o, w_out = _ret_core(q, k, v, gate, batch=batch, seq=seq), ret_w_out
        elif mixer == 2:
            o = _gmlp(xt, mod, gmlp_w_in, gmlp_ln_g, gmlp_ln_b, gmlp_w_s, gmlp_b_s, seq=seq)
            w_out = gmlp_w_out
        else:
            q, k, v = _sb_in(xt, mod, sb_w_in, seq=seq)
            o, w_out = _sb_core(q, k, v, batch=batch, seq=seq), sb_w_out
        xt = _sublayer_tail(o, w_out, xt, mod, ffn_up_b, ffn_conv_w[i], ffn_conv_b[i], ffn_down_b,
                            ln_g, ln_b, layer=i, seq=seq, alpha=alpha)
    return xt.reshape(batch, seq, d)
```

```python
import functools
import math

import jax
import jax.numpy as jnp
from jax import lax
from jax.experimental import pallas as pl
from jax.experimental.pallas import tpu as pltpu

F32 = jnp.float32
BF16 = jnp.bfloat16

LANES = 128
SUBLANES = 8
VMEM_LIMIT = 56 * 1024 * 1024

LN_EPS = 1e-5
GDN_HEADS = 8
GDN_CHUNK = 64
GDN_CONV = 4
RET_HEADS = 4
RET_CHUNK = 128
RET_ROPE_BASE = 10000.0
GMLP_CHUNK = 128
GMLP_GROUPS = 8
SB_HEADS = 16
FFN_CONV = 3
DEAD_EXPONENT = 160.0
NORM_BOUND_SLACK = 1.001
EXP2_CLAMP = 64.0


def _params(*sem):
    return pltpu.CompilerParams(dimension_semantics=sem, vmem_limit_bytes=VMEM_LIMIT)


def _nn(a, b):
    return jnp.dot(a, b, preferred_element_type=F32)


def _nt(a, b):
    return lax.dot_general(a, b, (((1,), (1,)), ((), ())), preferred_element_type=F32)


def _tn(a, b):
    return lax.dot_general(a, b, (((0,), (0,)), ((), ())), preferred_element_type=F32)


def _sigmoid(x):
    return 1.0 / (1.0 + jnp.exp(-x))


def _silu(x):
    return x * _sigmoid(x)


def _softplus(x):
    return jnp.maximum(x, 0.0) + jnp.log(1.0 + jnp.exp(-jnp.abs(x)))


def _split3(x):
    hi = x.astype(BF16)
    r = x - hi.astype(F32)
    mid = r.astype(BF16)
    lo = (r - mid.astype(F32)).astype(BF16)
    return hi, mid, lo


def _modulate(x, mod_ref, shift_row, scale_row):
    return (x * (1.0 + mod_ref[0, scale_row:scale_row + 1, :])
            + mod_ref[0, shift_row:shift_row + 1, :]).astype(BF16)


def _layer_norm_rows(r, g, b, eps):
    mu = jnp.mean(r, axis=-1, keepdims=True)
    rc = r - mu
    var = jnp.mean(rc * rc, axis=-1, keepdims=True)
    return rc * lax.rsqrt(var + eps) * g + b


def _lane_replicated_columns(rows):
    n, k = rows.shape
    padded = jnp.concatenate([rows, jnp.zeros((LANES - n, k), F32)], axis=0)
    t = padded.T
    return [jnp.broadcast_to(t[:, b:b + 1], (k, LANES)) for b in range(n)]


def _rowvec_matmul(col, w):
    n = w.shape[1]
    parts = [jnp.sum(col * w[:, c:c + LANES], axis=0, keepdims=True) for c in range(0, n, LANES)]
    return jnp.concatenate(parts, axis=1)


def _cond_kernel(c_ref, cw_ref, cb_ref, aw_ref, ab_ref, o_ref, ecol_ref):
    nb = c_ref.shape[0]

    @pl.when((pl.program_id(0) == 0) & (pl.program_id(1) == 0))
    def _():
        ccols = _lane_replicated_columns(c_ref[...])
        cw = cw_ref[...]
        e = jnp.concatenate([_rowvec_matmul(col, cw) for col in ccols], axis=0) + cb_ref[...]
        ecols = _lane_replicated_columns(_silu(e))
        for b in range(nb):
            ecol_ref[b] = ecols[b]

    w = aw_ref[0]
    for b in range(nb):
        o_ref[0, b:b + 1, :] = _rowvec_matmul(ecol_ref[b], w) + ab_ref[0]


def _conditioning(c, cond_w, cond_b, ada_w, ada_b):
    nb, d = c.shape
    depth, _, n6 = ada_w.shape
    tn = 1536
    return pl.pallas_call(
        _cond_kernel,
        grid=(depth, n6 // tn),
        in_specs=[
            pl.BlockSpec((nb, d), lambda i, j: (0, 0)),
            pl.BlockSpec((d, d), lambda i, j: (0, 0)),
            pl.BlockSpec((1, d), lambda i, j: (0, 0)),
            pl.BlockSpec((1, d, tn), lambda i, j: (i, 0, j)),
            pl.BlockSpec((1, 1, tn), lambda i, j: (i, 0, j)),
        ],
        out_specs=pl.BlockSpec((1, nb, tn), lambda i, j: (i, 0, j)),
        out_shape=jax.ShapeDtypeStruct((depth, nb, n6), F32),
        scratch_shapes=[pltpu.VMEM((nb, d, LANES), F32)],
        compiler_params=_params("arbitrary", "arbitrary"),
        name="conditioning",
    )(c, cond_w, cond_b.reshape(1, d), ada_w, ada_b.reshape(depth, 1, n6))


def _ffn_kernel(o_ref, wo_ref, g1_ref, b1_ref, x_ref, mod_ref, wup_ref, cw_ref, cb_ref, wd_ref,
                g_ref, b_ref, out_ref, x1_ref, h_ref, buf_ref, act_ref, *, tiles_per_seq, alpha):
    i = pl.program_id(0)
    tm = x_ref.shape[0]
    f = wd_ref.shape[0]
    n_chunks, _, tf = buf_ref.shape

    @pl.when(i % tiles_per_seq == 0)
    def _():
        for c in range(n_chunks):
            buf_ref[c, 0:SUBLANES, :] = jnp.zeros((SUBLANES, tf), F32)

    half = tm // 2
    halves = (slice(0, half), slice(half, tm))
    for rs in halves:
        mixed = alpha * x_ref[rs, :] + (1.0 + mod_ref[0, 2:3, :]) * _nn(o_ref[rs, :], wo_ref[...])
        x1 = _layer_norm_rows(mixed, g1_ref[...], b1_ref[...], LN_EPS)
        x1_ref[rs, :] = x1
        h_ref[rs, :] = _modulate(x1, mod_ref, 3, 4)

    for c in range(n_chunks):
        cols = slice(c * tf, (c + 1) * tf)
        buf_ref[c, SUBLANES:SUBLANES + tm, :] = _nn(h_ref[...], wup_ref[:, cols])
        conv = cb_ref[:, cols]
        for tap in range(FFN_CONV):
            off = SUBLANES - (FFN_CONV - 1) + tap
            conv = conv + cw_ref[tap:tap + 1, cols] * buf_ref[c, off:off + tm, :]
        buf_ref[c, 0:SUBLANES, :] = buf_ref[c, tm:tm + SUBLANES, :]
        up = _nn(h_ref[...], wup_ref[:, f + c * tf:f + (c + 1) * tf])
        act_ref[:, cols] = (_silu(conv) * up).astype(BF16)

    for rs in halves:
        y = _nn(act_ref[rs, :], wd_ref[...])
        r = alpha * x1_ref[rs, :] + (1.0 + mod_ref[0, 5:6, :]) * y
        out_ref[rs, :] = _layer_norm_rows(r, g_ref[...], b_ref[...], LN_EPS)


def _resident(shape, layer=None):
    if layer is None:
        return pl.BlockSpec(shape, lambda *_: (0,) * len(shape), pipeline_mode=pl.Buffered(1))
    return pl.BlockSpec((None,) + tuple(shape), lambda *_: (layer,) + (0,) * len(shape),
                        pipeline_mode=pl.Buffered(1))


def _sublayer_tail(o, w_out, x, mod, w_up, conv_w, conv_b, w_down, ln_g, ln_b, *, layer, seq, alpha,
                   tm=512, n_chunks=11):
    t, d = x.shape
    kdim = o.shape[1]
    f = w_down.shape[1]
    tf = f // n_chunks
    tps = seq // tm
    rows = lambda i: (i, 0)
    return pl.pallas_call(
        functools.partial(_ffn_kernel, tiles_per_seq=tps, alpha=alpha),
        grid=(t // tm,),
        in_specs=[
            pl.BlockSpec((tm, kdim), rows),
            _resident((kdim, d)),
            _resident((1, d)),
            _resident((1, d)),
            pl.BlockSpec((tm, d), rows),
            pl.BlockSpec((1, 6, d), lambda i: (i // tps, 0, 0)),
            _resident((d, 2 * f), layer),
            _resident((FFN_CONV, f)),
            _resident((1, f)),
            _resident((f, d), layer),
            _resident((1, d)),
            _resident((1, d)),
        ],
        out_specs=pl.BlockSpec((tm, d), rows),
        out_shape=jax.ShapeDtypeStruct((t, d), F32),
        scratch_shapes=[
            pltpu.VMEM((tm, d), F32),
            pltpu.VMEM((tm, d), BF16),
            pltpu.VMEM((n_chunks, tm + SUBLANES, tf), F32),
            pltpu.VMEM((tm, f), BF16),
        ],
        compiler_params=_params("arbitrary"),
        name="sublayer_tail",
    )(o, w_out.astype(BF16), ln_g[layer, 0].reshape(1, d), ln_b[layer, 0].reshape(1, d), x, mod,
      w_up, conv_w, conv_b.reshape(1, f), w_down, ln_g[layer, 1].reshape(1, d), ln_b[layer, 1].reshape(1, d))


def _gdn_in_kernel(x_ref, mod_ref, wqkv_ref, wz_ref, wab_ref, cw_ref, alog_ref, dtb_ref,
                   q_ref, k_ref, v_ref, z_ref, gb_ref, gt_ref, buf_ref, *, tiles_per_seq, chunk):
    i = pl.program_id(0)
    tm, d = x_ref.shape
    nh = GDN_HEADS
    hd = d // nh

    @pl.when(i % tiles_per_seq == 0)
    def _():
        buf_ref[0:SUBLANES, :] = jnp.zeros((SUBLANES, buf_ref.shape[1]), F32)

    h = _modulate(x_ref[...], mod_ref, 0, 1)
    outs = (q_ref, k_ref, v_ref)
    for s in range(3):
        cs = slice(s * d, (s + 1) * d)
        buf_ref[SUBLANES:SUBLANES + tm, cs] = _nn(h, wqkv_ref[:, cs])
        y = None
        for tap in range(GDN_CONV):
            off = SUBLANES - (GDN_CONV - 1) + tap
            term = cw_ref[tap:tap + 1, cs] * buf_ref[off:off + tm, cs]
            y = term if y is None else y + term
        y = _silu(y)
        if s < 2:
            scale = hd ** -0.5 if s == 0 else 1.0
            for hh in range(nh):
                seg = y[:, hh * hd:(hh + 1) * hd]
                inv = lax.rsqrt(jnp.sum(seg * seg, axis=-1, keepdims=True) + 1e-6)
                outs[s][:, hh * hd:(hh + 1) * hd] = (seg * (inv * scale)).astype(BF16)
        else:
            outs[s][...] = y.astype(BF16)
    buf_ref[0:SUBLANES, :] = buf_ref[tm:tm + SUBLANES, :]

    z_ref[...] = _nn(h, wz_ref[...]).astype(BF16)

    pab = _nn(h, wab_ref[...])
    lane = lax.broadcasted_iota(jnp.int32, pab.shape, 1)
    g = -jnp.exp(alog_ref[...]) * _softplus(pab + dtb_ref[...])
    gb = jnp.where(lane < nh, g, jnp.where(lane < 2 * nh, _sigmoid(pab), 0.0))
    row = lax.broadcasted_iota(jnp.int32, (tm, tm), 0)
    col = lax.broadcasted_iota(jnp.int32, (tm, tm), 1)
    tri = jnp.where(col <= row, jnp.where(jnp.bitwise_xor(row, col) < chunk, 1.0, 0.0), 0.0).astype(BF16)
    hi, mid, lo = _split3(gb)
    cum = _nn(tri, hi) + _nn(tri, mid) + _nn(tri, lo)
    gb = jnp.where(lane < nh, cum, gb)
    gb_ref[...] = gb
    gt_ref[...] = gb.T[0:2 * nh, :]


def _gdn_in(x, mod, w_in, conv_w, a_log, dt_bias, *, seq, tm=256):
    t, d = x.shape
    nh = GDN_HEADS
    tps = seq // tm
    w_b = w_in.astype(BF16)
    w_ab = jnp.pad(w_b[:, 4 * d:], ((0, 0), (0, LANES - 2 * nh)))
    once = pl.Buffered(1)
    alog = jnp.pad(a_log, (0, LANES - nh)).reshape(1, LANES)
    dtb = jnp.pad(dt_bias, (0, LANES - nh)).reshape(1, LANES)
    row = lambda i: (i, 0)
    fixed = lambda i: (0, 0)
    return pl.pallas_call(
        functools.partial(_gdn_in_kernel, tiles_per_seq=tps, chunk=GDN_CHUNK),
        grid=(t // tm,),
        in_specs=[
            pl.BlockSpec((tm, d), row),
            pl.BlockSpec((1, 6, d), lambda i: (i // tps, 0, 0)),
            pl.BlockSpec((d, 3 * d), fixed, pipeline_mode=once),
            pl.BlockSpec((d, d), lambda i: (0, 3), pipeline_mode=once),
            pl.BlockSpec((d, LANES), fixed, pipeline_mode=once),
            pl.BlockSpec((GDN_CONV, 3 * d), fixed, pipeline_mode=once),
            pl.BlockSpec((1, LANES), fixed),
            pl.BlockSpec((1, LANES), fixed),
        ],
        out_specs=[pl.BlockSpec((tm, d), row)] * 4 + [pl.BlockSpec((tm, LANES), row),
                                                      pl.BlockSpec((2 * nh, tm), lambda i: (0, i))],
        out_shape=[jax.ShapeDtypeStruct((t, d), BF16)] * 4 + [jax.ShapeDtypeStruct((t, LANES), F32),
                                                              jax.ShapeDtypeStruct((2 * nh, t), F32)],
        scratch_shapes=[pltpu.VMEM((tm + SUBLANES, 3 * d), F32)],
        compiler_params=_params("arbitrary"),
        name="gdn_in",
    )(x, mod, w_b, w_b, w_ab, conv_w, alog, dtb)


def _gdn_core_kernel(q_ref, k_ref, v_ref, z_ref, gb_ref, gt_ref, nw_ref, o_ref,
                     state_ref, inv_ref, pw_ref, rhs_ref, u_ref, w_ref, qk_ref, qg_ref, kd_ref, dl_ref,
                     vn_ref, oi_ref, *, chunk):
    rows, d = q_ref.shape
    nh = GDN_HEADS
    hd = d // nh
    n_chunks = rows // chunk
    heads = range(nh)
    lanes = [slice(hh * hd, (hh + 1) * hd) for hh in heads]

    @pl.when(pl.program_id(1) == 0)
    def _():
        state_ref[...] = jnp.zeros(state_ref.shape, F32)

    row = lax.broadcasted_iota(jnp.int32, (rows, rows), 0)
    col = lax.broadcasted_iota(jnp.int32, (rows, rows), 1)
    same_chunk = jnp.bitwise_xor(row, col) < chunk
    gb = gb_ref[...]
    for hh in heads:
        kh = k_ref[:, lanes[hh]]
        kf = kh.astype(F32)
        qh = q_ref[:, lanes[hh]]
        gc = jnp.broadcast_to(gb[:, hh:hh + 1], (rows, hd))
        beta = jnp.broadcast_to(gb[:, nh + hh:nh + hh + 1], (rows, hd))
        diff = jnp.broadcast_to(gb[:, hh:hh + 1], (rows, rows)) - gt_ref[hh:hh + 1, :]
        causal = same_chunk & (row >= col)
        decay = jnp.where(causal, jnp.exp(jnp.where(causal, diff, 0.0)), 0.0)
        kb = kf * beta
        a = jnp.where(row > col, _nt(kb.astype(BF16), kh) * decay, 0.0)
        inv_ref[hh] = jnp.where(row == col, 1.0, 0.0) - a
        pw_ref[hh] = a.astype(BF16)
        qk_ref[hh] = (_nt(qh, kh) * decay).astype(BF16)
        rhs_ref[hh, :, 0:hd] = (v_ref[:, lanes[hh]].astype(F32) * beta).astype(BF16)
        rhs_ref[hh, :, hd:2 * hd] = (kb * jnp.exp(gc)).astype(BF16)
        qg_ref[hh] = (qh.astype(F32) * jnp.exp(gc)).astype(BF16)
        for c in range(n_chunks):
            rs = slice(c * chunk, (c + 1) * chunk)
            g_last = gc[(c + 1) * chunk - 1:(c + 1) * chunk, :]
            kd_ref[hh, rs, :] = (kf[rs] * jnp.exp(g_last - gc[rs])).astype(BF16)
            dl_ref[hh, c] = jnp.exp(g_last)

    for _ in range(int(math.log2(chunk)) - 1):
        for hh in heads:
            power = _nn(pw_ref[hh], pw_ref[hh]).astype(BF16)
            pw_ref[hh] = power
            inv = inv_ref[hh]
            inv_ref[hh] = inv + _nn(inv.astype(BF16), power)
    for hh in heads:
        sol = _nn(inv_ref[hh].astype(BF16), rhs_ref[hh])
        u_ref[hh] = sol[:, 0:hd]
        w_ref[hh] = sol[:, hd:2 * hd].astype(BF16)

    for c in range(n_chunks):
        rs = slice(c * chunk, (c + 1) * chunk)
        states = [state_ref[hh].astype(BF16) for hh in heads]
        v_new = [(u_ref[hh, rs, :] - _nn(w_ref[hh, rs, :], states[hh])).astype(BF16) for hh in heads]
        for hh in heads:
            vn_ref[hh, rs, :] = v_new[hh]
            state_ref[hh] = state_ref[hh] * dl_ref[hh, c] + _tn(kd_ref[hh, rs, :], v_new[hh])
            oi_ref[hh, rs, :] = _nn(qg_ref[hh, rs, :], states[hh])

    for hh in heads:
        o = oi_ref[hh] + _nn(qk_ref[hh], vn_ref[hh])
        o = o * lax.rsqrt(jnp.mean(o * o, axis=-1, keepdims=True) + 1e-6) * nw_ref[...]
        o_ref[:, lanes[hh]] = (o * _silu(z_ref[:, lanes[hh]].astype(F32))).astype(BF16)


def _gdn_core(q, k, v, z, gb, gt, norm_w, *, batch, seq, rows=256):
    t, d = q.shape
    n = seq // rows
    nh = GDN_HEADS
    hd = d // nh
    row = lambda b, j: (b * n + j, 0)
    return pl.pallas_call(
        functools.partial(_gdn_core_kernel, chunk=GDN_CHUNK),
        grid=(batch, n),
        in_specs=[pl.BlockSpec((rows, d), row)] * 4 + [
            pl.BlockSpec((rows, LANES), row),
            pl.BlockSpec((2 * nh, rows), lambda b, j: (0, b * n + j)),
            pl.BlockSpec((1, hd), lambda b, j: (0, 0)),
        ],
        out_specs=pl.BlockSpec((rows, d), row),
        out_shape=jax.ShapeDtypeStruct((t, d), BF16),
        scratch_shapes=[
            pltpu.VMEM((nh, hd, hd), F32),
            pltpu.VMEM((nh, rows, rows), F32),
            pltpu.VMEM((nh, rows, rows), BF16),
            pltpu.VMEM((nh, rows, 2 * hd), BF16),
            pltpu.VMEM((nh, rows, hd), F32),
            pltpu.VMEM((nh, rows, hd), BF16),
            pltpu.VMEM((nh, rows, rows), BF16),
            pltpu.VMEM((nh, rows, hd), BF16),
            pltpu.VMEM((nh, rows, hd), BF16),
            pltpu.VMEM((nh, rows // GDN_CHUNK, 1, hd), F32),
            pltpu.VMEM((nh, rows, hd), BF16),
            pltpu.VMEM((nh, rows, hd), F32),
        ],
        compiler_params=_params("arbitrary", "arbitrary"),
        name="gdn_core",
    )(q, k, v, z, gb, gt, norm_w.reshape(1, hd))


def _rope_table_kernel(cos_ref, sin_ref):
    ts, half = cos_ref.shape
    pos = (lax.broadcasted_iota(jnp.int32, (ts, half), 0) + pl.program_id(0) * ts).astype(F32)
    frac = lax.broadcasted_iota(jnp.int32, (ts, half), 1).astype(F32) / (half - 1.0)
    ang = pos * jnp.exp(-frac * math.log(RET_ROPE_BASE))
    cos_ref[...] = jnp.cos(ang)
    sin_ref[...] = jnp.sin(ang)


def _rope_tables(seq, half, ts=256):
    spec = pl.BlockSpec((ts, half), lambda i: (i, 0))
    return pl.pallas_call(
        _rope_table_kernel,
        grid=(seq // ts,),
        out_specs=[spec, spec],
        out_shape=[jax.ShapeDtypeStruct((seq, half), F32)] * 2,
        compiler_params=_params("arbitrary"),
        name="rope_tables",
    )()


def _ret_in_kernel(x_ref, mod_ref, w_ref, cos_ref, sin_ref, q_ref, k_ref, v_ref, gate_ref):
    d = x_ref.shape[1]
    nh = RET_HEADS
    dk = d // nh
    half = dk // 2
    h = _modulate(x_ref[...], mod_ref, 0, 1)
    cos_a = cos_ref[...]
    sin_a = sin_ref[...]
    for s, (out, scale) in enumerate(((q_ref, 1.0), (k_ref, dk ** -0.5))):
        t = _nn(h, w_ref[:, s * d:(s + 1) * d])
        for hh in range(nh):
            t1 = t[:, hh * dk:hh * dk + half]
            t2 = t[:, hh * dk + half:(hh + 1) * dk]
            out[:, hh * dk:hh * dk + half] = ((t1 * cos_a - t2 * sin_a) * scale).astype(BF16)
            out[:, hh * dk + half:(hh + 1) * dk] = ((t1 * sin_a + t2 * cos_a) * scale).astype(BF16)
    v_ref[...] = _nn(h, w_ref[:, 2 * d:4 * d]).astype(BF16)
    gate_ref[...] = _silu(_nn(h, w_ref[:, 4 * d:6 * d])).astype(BF16)


def _ret_in(x, mod, w_in, cos_t, sin_t, *, seq, tm=512):
    t, d = x.shape
    tps = seq // tm
    half = cos_t.shape[1]
    row = lambda i: (i, 0)
    return pl.pallas_call(
        _ret_in_kernel,
        grid=(t // tm,),
        in_specs=[
            pl.BlockSpec((tm, d), row),
            pl.BlockSpec((1, 6, d), lambda i: (i // tps, 0, 0)),
            _resident((d, 6 * d)),
            pl.BlockSpec((tm, half), lambda i: (i % tps, 0)),
            pl.BlockSpec((tm, half), lambda i: (i % tps, 0)),
        ],
        out_specs=[pl.BlockSpec((tm, d), row), pl.BlockSpec((tm, d), row),
                   pl.BlockSpec((tm, 2 * d), row), pl.BlockSpec((tm, 2 * d), row)],
        out_shape=[jax.ShapeDtypeStruct((t, d), BF16), jax.ShapeDtypeStruct((t, d), BF16),
                   jax.ShapeDtypeStruct((t, 2 * d), BF16), jax.ShapeDtypeStruct((t, 2 * d), BF16)],
        compiler_params=_params("arbitrary"),
        name="ret_in",
    )(x, mod, w_in.astype(BF16), cos_t, sin_t)


def _ret_core_kernel(q_ref, k_ref, v_ref, gate_ref, o_ref, state_ref, *, chunk):
    rows, d = q_ref.shape
    c = chunk
    nh = RET_HEADS
    dk = d // nh
    dv = v_ref.shape[1] // nh

    @pl.when(pl.program_id(1) == 0)
    def _():
        state_ref[...] = jnp.zeros(state_ref.shape, F32)

    rel = (lax.broadcasted_iota(jnp.int32, (c, c), 0)
           - lax.broadcasted_iota(jnp.int32, (c, c), 1)).astype(F32)
    idx = lax.broadcasted_iota(jnp.int32, (c, 1), 0).astype(F32)
    for hh in range(nh):
        log_gamma = math.log(1.0 - 2.0 ** (-5.0 - hh))
        dmask = jnp.where(rel >= 0, jnp.exp(jnp.maximum(rel, 0.0) * log_gamma), 0.0)
        zeta = jnp.exp((c - 1.0 - idx) * log_gamma)
        xi = jnp.exp((idx + 1.0) * log_gamma)
        state = state_ref[hh]
        for ci in range(rows // c):
            rs = slice(ci * c, (ci + 1) * c)
            qh = q_ref[rs, hh * dk:(hh + 1) * dk]
            kh = k_ref[rs, hh * dk:(hh + 1) * dk]
            vh = v_ref[rs, hh * dv:(hh + 1) * dv]
            scores = _nt(qh, kh) * dmask
            o = _nn(scores.astype(BF16), vh) + _nn(qh, state.astype(BF16)) * xi
            state = state * math.exp(c * log_gamma) + _tn((kh.astype(F32) * zeta).astype(BF16), vh)
            mu = jnp.mean(o, axis=-1, keepdims=True)
            oc = o - mu
            var = jnp.mean(oc * oc, axis=-1, keepdims=True)
            o = oc * lax.rsqrt(var + 1e-6)
            gate = gate_ref[rs, hh * dv:(hh + 1) * dv].astype(F32)
            o_ref[rs, hh * dv:(hh + 1) * dv] = (o * gate).astype(BF16)
        state_ref[hh] = state


def _ret_core(q, k, v, gate, *, batch, seq, rows=512):
    t, d = q.shape
    c = rows
    n = seq // rows
    dk = d // RET_HEADS
    dv = v.shape[1] // RET_HEADS
    row = lambda b, j: (b * n + j, 0)
    return pl.pallas_call(
        functools.partial(_ret_core_kernel, chunk=RET_CHUNK),
        grid=(batch, n),
        in_specs=[pl.BlockSpec((c, d), row), pl.BlockSpec((c, d), row),
                  pl.BlockSpec((c, 2 * d), row), pl.BlockSpec((c, 2 * d), row)],
        out_specs=pl.BlockSpec((c, 2 * d), row),
        out_shape=jax.ShapeDtypeStruct((t, 2 * d), BF16),
        scratch_shapes=[pltpu.VMEM((RET_HEADS, dk, dv), F32)],
        compiler_params=_params("arbitrary", "arbitrary"),
        name="ret_core",
    )(q, k, v, gate)


def _gelu(t):
    p = 0.3275911 / math.sqrt(2.0)
    half_coefs = [0.5 * a for a in (0.254829592, -0.284496736, 1.421413741, -1.453152027, 1.061405429)]
    u = 1.0 / (1.0 + p * jnp.abs(t))
    poly = half_coefs[4]
    for coef in half_coefs[3::-1]:
        poly = poly * u + coef
    e = poly * u * jnp.exp2(t * t * (-0.5 * math.log2(math.e)))
    return t * jnp.where(t >= 0.0, 1.0 - e, e)


def _gmlp_kernel(x_ref, mod_ref, win_ref, lng_ref, lnb_ref, ws_ref, bs_ref, out_ref):
    tm, d = x_ref.shape
    width = win_ref.shape[1] // 2
    c = GMLP_CHUNK
    ng = GMLP_GROUPS
    gw = width // ng
    h = _modulate(x_ref[...], mod_ref, 0, 1)

    v = _gelu(_nn(h, win_ref[:, width:]))
    v = _layer_norm_rows(v, lng_ref[...], lnb_ref[...], LN_EPS).astype(BF16)
    u = _gelu(_nn(h, win_ref[:, :width]))
    row = lax.broadcasted_iota(jnp.int32, (c, c), 0)
    col = lax.broadcasted_iota(jnp.int32, (c, c), 1)
    for n in range(tm // c):
        rs = slice(n * c, (n + 1) * c)
        for gi in range(ng):
            ls = slice(gi * gw, (gi + 1) * gw)
            ws = jnp.where(row >= col, ws_ref[gi], 0.0).astype(BF16)
            vs = _nn(ws, v[rs, ls]) + bs_ref[:, gi:gi + 1]
            out_ref[rs, ls] = (u[rs, ls] * vs).astype(BF16)


def _gmlp(x, mod, w_in, ln_g, ln_b, w_s, b_s, *, seq, tm=512):
    t, d = x.shape
    width = w_in.shape[1] // 2
    ng, c, _ = w_s.shape
    tps = seq // tm
    return pl.pallas_call(
        _gmlp_kernel,
        grid=(t // tm,),
        in_specs=[
            pl.BlockSpec((tm, d), lambda i: (i, 0)),
            pl.BlockSpec((1, 6, d), lambda i: (i // tps, 0, 0)),
            _resident((d, 2 * width)),
            _resident((1, width)),
            _resident((1, width)),
            _resident((ng, c, c)),
            _resident((c, ng)),
        ],
        out_specs=pl.BlockSpec((tm, width), lambda i: (i, 0)),
        out_shape=jax.ShapeDtypeStruct((t, width), BF16),
        compiler_params=_params("arbitrary"),
        name="gmlp",
    )(x, mod, w_in.astype(BF16), ln_g.reshape(1, width), ln_b.reshape(1, width), w_s, b_s.T)


def _sb_in_kernel(x_ref, mod_ref, w_ref, q_ref, k_ref, v_ref, *, q_scale):
    d = x_ref.shape[1]
    h = _modulate(x_ref[...], mod_ref, 0, 1)
    q_ref[...] = (_nn(h, w_ref[:, 0:d]) * q_scale).astype(BF16)
    k_ref[...] = _nn(h, w_ref[:, d:2 * d]).astype(BF16)
    v_ref[...] = _nn(h, w_ref[:, 2 * d:3 * d]).astype(BF16)


def _sb_in(x, mod, w_in, *, seq, tm=512):
    t, d = x.shape
    tps = seq // tm
    row = lambda i: (i, 0)
    return pl.pallas_call(
        functools.partial(_sb_in_kernel, q_scale=(d // SB_HEADS) ** -0.5 * math.log2(math.e)),
        grid=(t // tm,),
        in_specs=[
            pl.BlockSpec((tm, d), row),
            pl.BlockSpec((1, 6, d), lambda i: (i // tps, 0, 0)),
            pl.BlockSpec((d, 3 * d), lambda i: (0, 0)),
        ],
        out_specs=[pl.BlockSpec((tm, d), row)] * 3,
        out_shape=[jax.ShapeDtypeStruct((t, d), BF16)] * 3,
        compiler_params=_params("arbitrary"),
        name="sb_in",
    )(x, mod, w_in.astype(BF16))


def _sb_core_kernel(q_ref, k_ref, v_ref, o_ref, qh_ref, sp_ref, zc_ref, carry_ref, acc_ref, knorm_ref,
                    bound_ref):
    tq, width = q_ref.shape
    seq = k_ref.shape[0]
    n_heads = qh_ref.shape[0]
    dh = LANES // 2
    qi = pl.program_id(2)
    lane = lax.broadcasted_iota(jnp.int32, (tq, LANES), 1)
    half_lanes = (lane < dh, lane >= dh)
    group = [slice((hh // 2) * LANES, (hh // 2 + 1) * LANES) for hh in range(n_heads)]
    row = lax.broadcasted_iota(jnp.int32, (tq, tq), 0)
    col = lax.broadcasted_iota(jnp.int32, (tq, tq), 1)
    ones_lower = jnp.where(row >= col, 1.0, 0.0).astype(BF16)

    @pl.when(qi == 0)
    def _():
        def key_tile(t, best):
            kf = k_ref[pl.ds(pl.multiple_of(t * tq, tq), tq), :].astype(F32)
            return jnp.maximum(best, jnp.max(kf * kf, axis=0, keepdims=True))

        col_max = lax.fori_loop(0, seq // tq, key_tile, jnp.zeros((1, width), F32))
        for hh in range(n_heads):
            bound_sq = jnp.sum(jnp.where(half_lanes[hh % 2][0:1, :], col_max[:, group[hh]], 0.0),
                               axis=-1, keepdims=True)
            knorm_ref[hh] = jnp.broadcast_to(bound_sq, knorm_ref.shape[1:])

    for hh in range(n_heads):
        q = q_ref[:, group[hh]]
        qf = q.astype(F32)
        qh_ref[hh] = jnp.where(half_lanes[hh % 2], q, jnp.zeros_like(q))
        qsq = jnp.sum(jnp.where(half_lanes[hh % 2], qf * qf, 0.0), axis=-1, keepdims=True)
        bound_ref[hh] = jnp.sqrt(qsq * knorm_ref[hh, 0:1, 0:1]) * NORM_BOUND_SLACK
    acc_ref[...] = jnp.zeros(acc_ref.shape, F32)
    carry_ref[...] = jnp.zeros(carry_ref.shape, F32)

    def stage(kt, slot, diagonal=False):
        rows = pl.ds(pl.multiple_of(kt * tq, tq), tq)
        for hh in range(n_heads):
            z = _nt(qh_ref[hh], k_ref[rows, group[hh]])
            sp = jnp.maximum(z, jnp.log2(1.0 + jnp.exp2(jnp.minimum(z, EXP2_CLAMP))))
            carry = carry_ref[hh]
            zc = z - carry
            if diagonal:
                sp = jnp.where(col < row, sp, 0.0)
                zc = jnp.where(col < row, zc, -1e30)
            sp_ref[slot, hh] = sp.astype(BF16)
            zc_ref[slot, hh] = zc
            carry_ref[hh] = carry + jnp.sum(sp, axis=-1, keepdims=True)

    def consume(kt, slot):
        rows = pl.ds(pl.multiple_of(kt * tq, tq), tq)
        for hh in range(n_heads):
            inclusive = _nn(sp_ref[slot, hh], ones_lower)
            a = jnp.exp2(zc_ref[slot, hh] - inclusive)
            acc_ref[hh] += _nn(a.astype(BF16), v_ref[rows, group[hh]])

    def still_alive():
        slack = bound_ref[0] - carry_ref[0]
        for hh in range(1, n_heads):
            slack = jnp.maximum(slack, bound_ref[hh] - carry_ref[hh])
        return (jnp.max(slack) > -DEAD_EXPONENT).astype(jnp.int32)

    stage(qi, 0, diagonal=True)

    @pl.when(qi == 0)
    def _():
        consume(0, 0)

    @pl.when(qi > 0)
    def _():
        stage(qi - 1, 1)
        alive = still_alive()
        consume(qi, 0)
        n_pairs = (qi - 1) // 2

        def more(state):
            return (state[0] < n_pairs) & (state[1] > 0)

        def pair(state):
            kt = qi - 1 - 2 * state[0]
            stage(kt - 1, 0)
            consume(kt, 1)
            stage(kt - 2, 1)
            alive = still_alive()
            consume(kt - 1, 0)
            return state[0] + 1, alive

        pairs_done, alive = lax.while_loop(more, pair, (jnp.int32(0), alive))
        staged = qi - 1 - 2 * pairs_done

        @pl.when((alive == 0) | (staged == 0))
        def _():
            consume(staged, 1)

        @pl.when((alive > 0) & (staged == 1))
        def _():
            stage(0, 0)
            consume(1, 1)
            consume(0, 0)

    for hh in range(0, n_heads, 2):
        o_ref[:, group[hh]] = jnp.where(half_lanes[0], acc_ref[hh], acc_ref[hh + 1]).astype(BF16)


def _sb_core(q, k, v, *, batch, seq, tq=256, heads_per_step=8):
    t, d = q.shape
    tq = min(tq, seq)
    nq = seq // tq
    nh = heads_per_step
    width = nh // 2 * LANES
    return pl.pallas_call(
        _sb_core_kernel,
        grid=(batch, d // width, nq),
        in_specs=[
            pl.BlockSpec((tq, width), lambda b, g, i: (b * nq + i, g)),
            pl.BlockSpec((seq, width), lambda b, g, i: (b, g)),
            pl.BlockSpec((seq, width), lambda b, g, i: (b, g)),
        ],
        out_specs=pl.BlockSpec((tq, width), lambda b, g, i: (b * nq + i, g)),
        out_shape=jax.ShapeDtypeStruct((t, d), BF16),
        scratch_shapes=[
            pltpu.VMEM((nh, tq, LANES), BF16),
            pltpu.VMEM((2, nh, tq, tq), BF16),
            pltpu.VMEM((2, nh, tq, tq), F32),
            pltpu.VMEM((nh, tq, 1), F32),
            pltpu.VMEM((nh, tq, LANES), F32),
            pltpu.VMEM((nh, SUBLANES, LANES), F32),
            pltpu.VMEM((nh, tq, 1), F32),
        ],
        compiler_params=_params("arbitrary", "arbitrary", "arbitrary"),
        name="sb_core",
    )(q, k, v)


def kernel(x, c, cond_w, cond_b, ada_w, ada_b, ln_g, ln_b, ffn_up, ffn_conv_w, ffn_conv_b, ffn_down,
           gdn_w_in, gdn_conv_w, gdn_a_log, gdn_dt_bias, gdn_norm_w, gdn_w_out,
           ret_w_in, ret_w_out,
           gmlp_w_in, gmlp_ln_g, gmlp_ln_b, gmlp_w_s, gmlp_b_s, gmlp_w_out,
           sb_w_in, sb_w_out):
    batch, seq, d = x.shape
    depth = ada_w.shape[0]
    alpha = (2.0 * depth) ** 0.25
    mods = _conditioning(c, cond_w, cond_b, ada_w, ada_b).reshape(depth, batch, 6, d)
    xt = x.reshape(batch * seq, d)
    ffn_up_b = ffn_up.astype(BF16)
    ffn_down_b = ffn_down.astype(BF16)

    for i in range(depth):
        mod = mods[i]
        mixer = i % 4
        if mixer == 0:
            q, k, v, z, gb, gt = _gdn_in(xt, mod, gdn_w_in, gdn_conv_w, gdn_a_log, gdn_dt_bias, seq=seq)
            o, w_out = _gdn_core(q, k, v, z, gb, gt, gdn_norm_w, batch=batch, seq=seq), gdn_w_out
        elif mixer == 1:
            cos_t, sin_t = _rope_tables(seq, d // RET_HEADS // 2)
            q, k, v, gate = _ret_in(xt, mod, ret_w_in, cos_t, sin_t, seq=seq)
            o, w_out = _ret_core(q, k, v, gate, batch=batch, seq=seq), ret_w_out
        elif mixer == 2:
            o = _gmlp(xt, mod, gmlp_w_in, gmlp_ln_g, gmlp_ln_b, gmlp_w_s, gmlp_b_s, seq=seq)
            w_out = gmlp_w_out
        else:
            q, k, v = _sb_in(xt, mod, sb_w_in, seq=seq)
            o, w_out = _sb_core(q, k, v, batch=batch, seq=seq), sb_w_out
        xt = _sublayer_tail(o, w_out, xt, mod, ffn_up_b, ffn_conv_w[i], ffn_conv_b[i], ffn_down_b,
                            ln_g, ln_b, layer=i, seq=seq, alpha=alpha)
    return xt.reshape(batch, seq, d)
```

```python
import functools
import math

import jax
import jax.numpy as jnp
from jax import lax
from jax.experimental import pallas as pl
from jax.experimental.pallas import tpu as pltpu

F32 = jnp.float32
BF16 = jnp.bfloat16

LANES = 128
SUBLANES = 8
VMEM_LIMIT = 56 * 1024 * 1024

LN_EPS = 1e-5
GDN_HEADS = 8
GDN_CHUNK = 64
GDN_CONV = 4
RET_HEADS = 4
RET_CHUNK = 128
RET_ROPE_BASE = 10000.0
GMLP_CHUNK = 128
GMLP_GROUPS = 8
SB_HEADS = 16
FFN_CONV = 3
DEAD_EXPONENT = 160.0
NORM_BOUND_SLACK = 1.001
EXP2_CLAMP = 64.0


def _params(*sem):
    return pltpu.CompilerParams(dimension_semantics=sem, vmem_limit_bytes=VMEM_LIMIT)


def _nn(a, b):
    return jnp.dot(a, b, preferred_element_type=F32)


def _nt(a, b):
    return lax.dot_general(a, b, (((1,), (1,)), ((), ())), preferred_element_type=F32)


def _tn(a, b):
    return lax.dot_general(a, b, (((0,), (0,)), ((), ())), preferred_element_type=F32)


def _sigmoid(x):
    return 1.0 / (1.0 + jnp.exp(-x))


def _silu(x):
    return x * _sigmoid(x)


def _softplus(x):
    return jnp.maximum(x, 0.0) + jnp.log(1.0 + jnp.exp(-jnp.abs(x)))


def _split3(x):
    hi = x.astype(BF16)
    r = x - hi.astype(F32)
    mid = r.astype(BF16)
    lo = (r - mid.astype(F32)).astype(BF16)
    return hi, mid, lo


def _modulate(x, mod_ref, shift_row, scale_row):
    return (x * (1.0 + mod_ref[0, scale_row:scale_row + 1, :])
            + mod_ref[0, shift_row:shift_row + 1, :]).astype(BF16)


def _layer_norm_rows(r, g, b, eps):
    mu = jnp.mean(r, axis=-1, keepdims=True)
    rc = r - mu
    var = jnp.mean(rc * rc, axis=-1, keepdims=True)
    return rc * lax.rsqrt(var + eps) * g + b


def _lane_replicated_columns(rows):
    n, k = rows.shape
    padded = jnp.concatenate([rows, jnp.zeros((LANES - n, k), F32)], axis=0)
    t = padded.T
    return [jnp.broadcast_to(t[:, b:b + 1], (k, LANES)) for b in range(n)]


def _rowvec_matmul(col, w):
    n = w.shape[1]
    parts = [jnp.sum(col * w[:, c:c + LANES], axis=0, keepdims=True) for c in range(0, n, LANES)]
    return jnp.concatenate(parts, axis=1)


def _cond_kernel(c_ref, cw_ref, cb_ref, aw_ref, ab_ref, o_ref, ecol_ref):
    nb = c_ref.shape[0]

    @pl.when((pl.program_id(0) == 0) & (pl.program_id(1) == 0))
    def _():
        ccols = _lane_replicated_columns(c_ref[...])
        cw = cw_ref[...]
        e = jnp.concatenate([_rowvec_matmul(col, cw) for col in ccols], axis=0) + cb_ref[...]
        ecols = _lane_replicated_columns(_silu(e))
        for b in range(nb):
            ecol_ref[b] = ecols[b]

    w = aw_ref[0]
    for b in range(nb):
        o_ref[0, b:b + 1, :] = _rowvec_matmul(ecol_ref[b], w) + ab_ref[0]


def _conditioning(c, cond_w, cond_b, ada_w, ada_b):
    nb, d = c.shape
    depth, _, n6 = ada_w.shape
    tn = 1536
    return pl.pallas_call(
        _cond_kernel,
        grid=(depth, n6 // tn),
        in_specs=[
            pl.BlockSpec((nb, d), lambda i, j: (0, 0)),
            pl.BlockSpec((d, d), lambda i, j: (0, 0)),
            pl.BlockSpec((1, d), lambda i, j: (0, 0)),
            pl.BlockSpec((1, d, tn), lambda i, j: (i, 0, j)),
            pl.BlockSpec((1, 1, tn), lambda i, j: (i, 0, j)),
        ],
        out_specs=pl.BlockSpec((1, nb, tn), lambda i, j: (i, 0, j)),
        out_shape=jax.ShapeDtypeStruct((depth, nb, n6), F32),
        scratch_shapes=[pltpu.VMEM((nb, d, LANES), F32)],
        compiler_params=_params("arbitrary", "arbitrary"),
        name="conditioning",
    )(c, cond_w, cond_b.reshape(1, d), ada_w, ada_b.reshape(depth, 1, n6))


def _ffn_kernel(o_ref, wo_ref, g1_ref, b1_ref, x_ref, mod_ref, wup_ref, cw_ref, cb_ref, wd_ref,
                g_ref, b_ref, out_ref, x1_ref, h_ref, buf_ref, act_ref, *, tiles_per_seq, alpha):
    i = pl.program_id(0)
    tm = x_ref.shape[0]
    f = wd_ref.shape[0]
    n_chunks, _, tf = buf_ref.shape

    @pl.when(i % tiles_per_seq == 0)
    def _():
        for c in range(n_chunks):
            buf_ref[c, 0:SUBLANES, :] = jnp.zeros((SUBLANES, tf), F32)

    half = tm // 2
    halves = (slice(0, half), slice(half, tm))
    for rs in halves:
        mixed = alpha * x_ref[rs, :] + (1.0 + mod_ref[0, 2:3, :]) * _nn(o_ref[rs, :], wo_ref[...])
        x1 = _layer_norm_rows(mixed, g1_ref[...], b1_ref[...], LN_EPS)
        x1_ref[rs, :] = x1
        h_ref[rs, :] = _modulate(x1, mod_ref, 3, 4)

    for c in range(n_chunks):
        cols = slice(c * tf, (c + 1) * tf)
        buf_ref[c, SUBLANES:SUBLANES + tm, :] = _nn(h_ref[...], wup_ref[:, cols])
        conv = cb_ref[:, cols]
        for tap in range(FFN_CONV):
            off = SUBLANES - (FFN_CONV - 1) + tap
            conv = conv + cw_ref[tap:tap + 1, cols] * buf_ref[c, off:off + tm, :]
        buf_ref[c, 0:SUBLANES, :] = buf_ref[c, tm:tm + SUBLANES, :]
        up = _nn(h_ref[...], wup_ref[:, f + c * tf:f + (c + 1) * tf])
        act_ref[:, cols] = (_silu(conv) * up).astype(BF16)

    for rs in halves:
        y = _nn(act_ref[rs, :], wd_ref[...])
        r = alpha * x1_ref[rs, :] + (1.0 + mod_ref[0, 5:6, :]) * y
        out_ref[rs, :] = _layer_norm_rows(r, g_ref[...], b_ref[...], LN_EPS)


def _resident(shape, layer=None):
    if layer is None:
        return pl.BlockSpec(shape, lambda *_: (0,) * len(shape), pipeline_mode=pl.Buffered(1))
    return pl.BlockSpec((None,) + tuple(shape), lambda *_: (layer,) + (0,) * len(shape),
                        pipeline_mode=pl.Buffered(1))


def _sublayer_tail(o, w_out, x, mod, w_up, conv_w, conv_b, w_down, ln_g, ln_b, *, layer, seq, alpha,
                   tm=512, n_chunks=11):
    t, d = x.shape
    kdim = o.shape[1]
    f = w_down.shape[1]
    tf = f // n_chunks
    tps = seq // tm
    rows = lambda i: (i, 0)
    return pl.pallas_call(
        functools.partial(_ffn_kernel, tiles_per_seq=tps, alpha=alpha),
        grid=(t // tm,),
        in_specs=[
            pl.BlockSpec((tm, kdim), rows),
            _resident((kdim, d)),
            _resident((1, d)),
            _resident((1, d)),
            pl.BlockSpec((tm, d), rows),
            pl.BlockSpec((1, 6, d), lambda i: (i // tps, 0, 0)),
            _resident((d, 2 * f), layer),
            _resident((FFN_CONV, f)),
            _resident((1, f)),
            _resident((f, d), layer),
            _resident((1, d)),
            _resident((1, d)),
        ],
        out_specs=pl.BlockSpec((tm, d), rows),
        out_shape=jax.ShapeDtypeStruct((t, d), F32),
        scratch_shapes=[
            pltpu.VMEM((tm, d), F32),
            pltpu.VMEM((tm, d), BF16),
            pltpu.VMEM((n_chunks, tm + SUBLANES, tf), F32),
            pltpu.VMEM((tm, f), BF16),
        ],
        compiler_params=_params("arbitrary"),
        name="sublayer_tail",
    )(o, w_out.astype(BF16), ln_g[layer, 0].reshape(1, d), ln_b[layer, 0].reshape(1, d), x, mod,
      w_up, conv_w, conv_b.reshape(1, f), w_down, ln_g[layer, 1].reshape(1, d), ln_b[layer, 1].reshape(1, d))


def _gdn_in_kernel(x_ref, mod_ref, wqkv_ref, wz_ref, wab_ref, cw_ref, alog_ref, dtb_ref,
                   q_ref, k_ref, v_ref, z_ref, gb_ref, gt_ref, buf_ref, *, tiles_per_seq, chunk):
    i = pl.program_id(0)
    tm, d = x_ref.shape
    nh = GDN_HEADS
    hd = d // nh

    @pl.when(i % tiles_per_seq == 0)
    def _():
        buf_ref[0:SUBLANES, :] = jnp.zeros((SUBLANES, buf_ref.shape[1]), F32)

    h = _modulate(x_ref[...], mod_ref, 0, 1)
    outs = (q_ref, k_ref, v_ref)
    for s in range(3):
        cs = slice(s * d, (s + 1) * d)
        buf_ref[SUBLANES:SUBLANES + tm, cs] = _nn(h, wqkv_ref[:, cs])
        y = None
        for tap in range(GDN_CONV):
            off = SUBLANES - (GDN_CONV - 1) + tap
            term = cw_ref[tap:tap + 1, cs] * buf_ref[off:off + tm, cs]
            y = term if y is None else y + term
        y = _silu(y)
        if s < 2:
            scale = hd ** -0.5 if s == 0 else 1.0
            for hh in range(nh):
                seg = y[:, hh * hd:(hh + 1) * hd]
                inv = lax.rsqrt(jnp.sum(seg * seg, axis=-1, keepdims=True) + 1e-6)
                outs[s][:, hh * hd:(hh + 1) * hd] = (seg * (inv * scale)).astype(BF16)
        else:
            outs[s][...] = y.astype(BF16)
    buf_ref[0:SUBLANES, :] = buf_ref[tm:tm + SUBLANES, :]

    z_ref[...] = _nn(h, wz_ref[...]).astype(BF16)

    pab = _nn(h, wab_ref[...])
    lane = lax.broadcasted_iota(jnp.int32, pab.shape, 1)
    g = -jnp.exp(alog_ref[...]) * _softplus(pab + dtb_ref[...])
    gb = jnp.where(lane < nh, g, jnp.where(lane < 2 * nh, _sigmoid(pab), 0.0))
    row = lax.broadcasted_iota(jnp.int32, (tm, tm), 0)
    col = lax.broadcasted_iota(jnp.int32, (tm, tm), 1)
    tri = jnp.where(col <= row, jnp.where(jnp.bitwise_xor(row, col) < chunk, 1.0, 0.0), 0.0).astype(BF16)
    hi, mid, lo = _split3(gb)
    cum = _nn(tri, hi) + _nn(tri, mid) + _nn(tri, lo)
    gb = jnp.where(lane < nh, cum, gb)
    gb_ref[...] = gb
    gt_ref[...] = gb.T[0:2 * nh, :]


def _gdn_in(x, mod, w_in, conv_w, a_log, dt_bias, *, seq, tm=256):
    t, d = x.shape
    nh = GDN_HEADS
    tps = seq // tm
    w_b = w_in.astype(BF16)
    w_ab = jnp.pad(w_b[:, 4 * d:], ((0, 0), (0, LANES - 2 * nh)))
    once = pl.Buffered(1)
    alog = jnp.pad(a_log, (0, LANES - nh)).reshape(1, LANES)
    dtb = jnp.pad(dt_bias, (0, LANES - nh)).reshape(1, LANES)
    row = lambda i: (i, 0)
    fixed = lambda i: (0, 0)
    return pl.pallas_call(
        functools.partial(_gdn_in_kernel, tiles_per_seq=tps, chunk=GDN_CHUNK),
        grid=(t // tm,),
        in_specs=[
            pl.BlockSpec((tm, d), row),
            pl.BlockSpec((1, 6, d), lambda i: (i // tps, 0, 0)),
            pl.BlockSpec((d, 3 * d), fixed, pipeline_mode=once),
            pl.BlockSpec((d, d), lambda i: (0, 3), pipeline_mode=once),
            pl.BlockSpec((d, LANES), fixed, pipeline_mode=once),
            pl.BlockSpec((GDN_CONV, 3 * d), fixed, pipeline_mode=once),
            pl.BlockSpec((1, LANES), fixed),
            pl.BlockSpec((1, LANES), fixed),
        ],
        out_specs=[pl.BlockSpec((tm, d), row)] * 4 + [pl.BlockSpec((tm, LANES), row),
                                                      pl.BlockSpec((2 * nh, tm), lambda i: (0, i))],
        out_shape=[jax.ShapeDtypeStruct((t, d), BF16)] * 4 + [jax.ShapeDtypeStruct((t, LANES), F32),
                                                              jax.ShapeDtypeStruct((2 * nh, t), F32)],
        scratch_shapes=[pltpu.VMEM((tm + SUBLANES, 3 * d), F32)],
        compiler_params=_params("arbitrary"),
        name="gdn_in",
    )(x, mod, w_b, w_b, w_ab, conv_w, alog, dtb)


def _gdn_core_kernel(q_ref, k_ref, v_ref, z_ref, gb_ref, gt_ref, nw_ref, o_ref,
                     state_ref, inv_ref, pw_ref, rhs_ref, u_ref, w_ref, qk_ref, qg_ref, kd_ref, dl_ref,
                     vn_ref, oi_ref, *, chunk):
    rows, d = q_ref.shape
    nh = GDN_HEADS
    hd = d // nh
    n_chunks = rows // chunk
    heads = range(nh)
    lanes = [slice(hh * hd, (hh + 1) * hd) for hh in heads]

    @pl.when(pl.program_id(1) == 0)
    def _():
        state_ref[...] = jnp.zeros(state_ref.shape, F32)

    row = lax.broadcasted_iota(jnp.int32, (rows, rows), 0)
    col = lax.broadcasted_iota(jnp.int32, (rows, rows), 1)
    same_chunk = jnp.bitwise_xor(row, col) < chunk
    gb = gb_ref[...]
    for hh in heads:
        kh = k_ref[:, lanes[hh]]
        kf = kh.astype(F32)
        qh = q_ref[:, lanes[hh]]
        gc = jnp.broadcast_to(gb[:, hh:hh + 1], (rows, hd))
        beta = jnp.broadcast_to(gb[:, nh + hh:nh + hh + 1], (rows, hd))
        diff = jnp.broadcast_to(gb[:, hh:hh + 1], (rows, rows)) - gt_ref[hh:hh + 1, :]
        causal = same_chunk & (row >= col)
        decay = jnp.where(causal, jnp.exp(jnp.where(causal, diff, 0.0)), 0.0)
        kb = kf * beta
        a = jnp.where(row > col, _nt(kb.astype(BF16), kh) * decay, 0.0)
        inv_ref[hh] = jnp.where(row == col, 1.0, 0.0) - a
        pw_ref[hh] = a.astype(BF16)
        qk_ref[hh] = (_nt(qh, kh) * decay).astype(BF16)
        rhs_ref[hh, :, 0:hd] = (v_ref[:, lanes[hh]].astype(F32) * beta).astype(BF16)
        rhs_ref[hh, :, hd:2 * hd] = (kb * jnp.exp(gc)).astype(BF16)
        qg_ref[hh] = (qh.astype(F32) * jnp.exp(gc)).astype(BF16)
        for c in range(n_chunks):
            rs = slice(c * chunk, (c + 1) * chunk)
            g_last = gc[(c + 1) * chunk - 1:(c + 1) * chunk, :]
            kd_ref[hh, rs, :] = (kf[rs] * jnp.exp(g_last - gc[rs])).astype(BF16)
            dl_ref[hh, c] = jnp.exp(g_last)

    for _ in range(int(math.log2(chunk)) - 1):
        for hh in heads:
            power = _nn(pw_ref[hh], pw_ref[hh]).astype(BF16)
            pw_ref[hh] = power
            inv = inv_ref[hh]
            inv_ref[hh] = inv + _nn(inv.astype(BF16), power)
    for hh in heads:
        sol = _nn(inv_ref[hh].astype(BF16), rhs_ref[hh])
        u_ref[hh] = sol[:, 0:hd]
        w_ref[hh] = sol[:, hd:2 * hd].astype(BF16)

    for c in range(n_chunks):
        rs = slice(c * chunk, (c + 1) * chunk)
        states = [state_ref[hh].astype(BF16) for hh in heads]
        v_new = [(u_ref[hh, rs, :] - _nn(w_ref[hh, rs, :], states[hh])).astype(BF16) for hh in heads]
        for hh in heads:
            vn_ref[hh, rs, :] = v_new[hh]
            state_ref[hh] = state_ref[hh] * dl_ref[hh, c] + _tn(kd_ref[hh, rs, :], v_new[hh])
            oi_ref[hh, rs, :] = _nn(qg_ref[hh, rs, :], states[hh])

    for hh in heads:
        o = oi_ref[hh] + _nn(qk_ref[hh], vn_ref[hh])
        o = o * lax.rsqrt(jnp.mean(o * o, axis=-1, keepdims=True) + 1e-6) * nw_ref[...]
        o_ref[:, lanes[hh]] = (o * _silu(z_ref[:, lanes[hh]].astype(F32))).astype(BF16)


def _gdn_core(q, k, v, z, gb, gt, norm_w, *, batch, seq, rows=256):
    t, d = q.shape
    n = seq // rows
    nh = GDN_HEADS
    hd = d // nh
    row = lambda b, j: (b * n + j, 0)
    return pl.pallas_call(
        functools.partial(_gdn_core_kernel, chunk=GDN_CHUNK),
        grid=(batch, n),
        in_specs=[pl.BlockSpec((rows, d), row)] * 4 + [
            pl.BlockSpec((rows, LANES), row),
            pl.BlockSpec((2 * nh, rows), lambda b, j: (0, b * n + j)),
            pl.BlockSpec((1, hd), lambda b, j: (0, 0)),
        ],
        out_specs=pl.BlockSpec((rows, d), row),
        out_shape=jax.ShapeDtypeStruct((t, d), BF16),
        scratch_shapes=[
            pltpu.VMEM((nh, hd, hd), F32),
            pltpu.VMEM((nh, rows, rows), F32),
            pltpu.VMEM((nh, rows, rows), BF16),
            pltpu.VMEM((nh, rows, 2 * hd), BF16),
            pltpu.VMEM((nh, rows, hd), F32),
            pltpu.VMEM((nh, rows, hd), BF16),
            pltpu.VMEM((nh, rows, rows), BF16),
            pltpu.VMEM((nh, rows, hd), BF16),
            pltpu.VMEM((nh, rows, hd), BF16),
            pltpu.VMEM((nh, rows // GDN_CHUNK, 1, hd), F32),
            pltpu.VMEM((nh, rows, hd), BF16),
            pltpu.VMEM((nh, rows, hd), F32),
        ],
        compiler_params=_params("arbitrary", "arbitrary"),
        name="gdn_core",
    )(q, k, v, z, gb, gt, norm_w.reshape(1, hd))


def _ret_in_kernel(x_ref, mod_ref, w_ref, q_ref, k_ref, v_ref, gate_ref, *, tiles_per_seq):
    tm, d = x_ref.shape
    nh = RET_HEADS
    dk = d // nh
    half = dk // 2
    h = _modulate(x_ref[...], mod_ref, 0, 1)
    first_pos = (pl.program_id(0) % tiles_per_seq) * tm
    pos = (lax.broadcasted_iota(jnp.int32, (tm, half), 0) + first_pos).astype(F32)
    frac = lax.broadcasted_iota(jnp.int32, (tm, half), 1).astype(F32) / (half - 1.0)
    ang = pos * jnp.exp(-frac * math.log(RET_ROPE_BASE))
    cos_a = jnp.cos(ang)
    sin_a = jnp.sin(ang)
    for s, (out, scale) in enumerate(((q_ref, 1.0), (k_ref, dk ** -0.5))):
        t = _nn(h, w_ref[:, s * d:(s + 1) * d])
        for hh in range(nh):
            t1 = t[:, hh * dk:hh * dk + half]
            t2 = t[:, hh * dk + half:(hh + 1) * dk]
            out[:, hh * dk:hh * dk + half] = ((t1 * cos_a - t2 * sin_a) * scale).astype(BF16)
            out[:, hh * dk + half:(hh + 1) * dk] = ((t1 * sin_a + t2 * cos_a) * scale).astype(BF16)
    v_ref[...] = _nn(h, w_ref[:, 2 * d:4 * d]).astype(BF16)
    gate_ref[...] = _silu(_nn(h, w_ref[:, 4 * d:6 * d])).astype(BF16)


def _ret_in(x, mod, w_in, *, seq, tm=512):
    t, d = x.shape
    tps = seq // tm
    row = lambda i: (i, 0)
    return pl.pallas_call(
        functools.partial(_ret_in_kernel, tiles_per_seq=tps),
        grid=(t // tm,),
        in_specs=[
            pl.BlockSpec((tm, d), row),
            pl.BlockSpec((1, 6, d), lambda i: (i // tps, 0, 0)),
            _resident((d, 6 * d)),
        ],
        out_specs=[pl.BlockSpec((tm, d), row), pl.BlockSpec((tm, d), row),
                   pl.BlockSpec((tm, 2 * d), row), pl.BlockSpec((tm, 2 * d), row)],
        out_shape=[jax.ShapeDtypeStruct((t, d), BF16), jax.ShapeDtypeStruct((t, d), BF16),
                   jax.ShapeDtypeStruct((t, 2 * d), BF16), jax.ShapeDtypeStruct((t, 2 * d), BF16)],
        compiler_params=_params("arbitrary"),
        name="ret_in",
    )(x, mod, w_in.astype(BF16))


def _ret_core_kernel(q_ref, k_ref, v_ref, gate_ref, o_ref, state_ref, *, chunk):
    rows, d = q_ref.shape
    c = chunk
    nh = RET_HEADS
    dk = d // nh
    dv = v_ref.shape[1] // nh

    @pl.when(pl.program_id(1) == 0)
    def _():
        state_ref[...] = jnp.zeros(state_ref.shape, F32)

    rel = (lax.broadcasted_iota(jnp.int32, (c, c), 0)
           - lax.broadcasted_iota(jnp.int32, (c, c), 1)).astype(F32)
    idx = lax.broadcasted_iota(jnp.int32, (c, 1), 0).astype(F32)
    for hh in range(nh):
        log_gamma = math.log(1.0 - 2.0 ** (-5.0 - hh))
        dmask = jnp.where(rel >= 0, jnp.exp(jnp.maximum(rel, 0.0) * log_gamma), 0.0)
        zeta = jnp.exp((c - 1.0 - idx) * log_gamma)
        xi = jnp.exp((idx + 1.0) * log_gamma)
        state = state_ref[hh]
        for ci in range(rows // c):
            rs = slice(ci * c, (ci + 1) * c)
            qh = q_ref[rs, hh * dk:(hh + 1) * dk]
            kh = k_ref[rs, hh * dk:(hh + 1) * dk]
            vh = v_ref[rs, hh * dv:(hh + 1) * dv]
            scores = _nt(qh, kh) * dmask
            o = _nn(scores.astype(BF16), vh) + _nn(qh, state.astype(BF16)) * xi
            state = state * math.exp(c * log_gamma) + _tn((kh.astype(F32) * zeta).astype(BF16), vh)
            mu = jnp.mean(o, axis=-1, keepdims=True)
            oc = o - mu
            var = jnp.mean(oc * oc, axis=-1, keepdims=True)
            o = oc * lax.rsqrt(var + 1e-6)
            gate = gate_ref[rs, hh * dv:(hh + 1) * dv].astype(F32)
            o_ref[rs, hh * dv:(hh + 1) * dv] = (o * gate).astype(BF16)
        state_ref[hh] = state


def _ret_core(q, k, v, gate, *, batch, seq, rows=512):
    t, d = q.shape
    c = rows
    n = seq // rows
    dk = d // RET_HEADS
    dv = v.shape[1] // RET_HEADS
    row = lambda b, j: (b * n + j, 0)
    return pl.pallas_call(
        functools.partial(_ret_core_kernel, chunk=RET_CHUNK),
        grid=(batch, n),
        in_specs=[pl.BlockSpec((c, d), row), pl.BlockSpec((c, d), row),
                  pl.BlockSpec((c, 2 * d), row), pl.BlockSpec((c, 2 * d), row)],
        out_specs=pl.BlockSpec((c, 2 * d), row),
        out_shape=jax.ShapeDtypeStruct((t, 2 * d), BF16),
        scratch_shapes=[pltpu.VMEM((RET_HEADS, dk, dv), F32)],
        compiler_params=_params("arbitrary", "arbitrary"),
        name="ret_core",
    )(q, k, v, gate)


def _gelu(t):
    p = 0.3275911 / math.sqrt(2.0)
    half_coefs = [0.5 * a for a in (0.254829592, -0.284496736, 1.421413741, -1.453152027, 1.061405429)]
    u = 1.0 / (1.0 + p * jnp.abs(t))
    poly = half_coefs[4]
    for coef in half_coefs[3::-1]:
        poly = poly * u + coef
    e = poly * u * jnp.exp2(t * t * (-0.5 * math.log2(math.e)))
    return t * jnp.where(t >= 0.0, 1.0 - e, e)


def _gmlp_kernel(x_ref, mod_ref, win_ref, lng_ref, lnb_ref, ws_ref, bs_ref, out_ref):
    tm, d = x_ref.shape
    width = win_ref.shape[1] // 2
    c = GMLP_CHUNK
    ng = GMLP_GROUPS
    gw = width // ng
    h = _modulate(x_ref[...], mod_ref, 0, 1)

    v = _gelu(_nn(h, win_ref[:, width:]))
    v = _layer_norm_rows(v, lng_ref[...], lnb_ref[...], LN_EPS).astype(BF16)
    u = _gelu(_nn(h, win_ref[:, :width]))
    row = lax.broadcasted_iota(jnp.int32, (c, c), 0)
    col = lax.broadcasted_iota(jnp.int32, (c, c), 1)
    for n in range(tm // c):
        rs = slice(n * c, (n + 1) * c)
        for gi in range(ng):
            ls = slice(gi * gw, (gi + 1) * gw)
            ws = jnp.where(row >= col, ws_ref[gi], 0.0).astype(BF16)
            vs = _nn(ws, v[rs, ls]) + bs_ref[:, gi:gi + 1]
            out_ref[rs, ls] = (u[rs, ls] * vs).astype(BF16)


def _gmlp(x, mod, w_in, ln_g, ln_b, w_s, b_s, *, seq, tm=512):
    t, d = x.shape
    width = w_in.shape[1] // 2
    ng, c, _ = w_s.shape
    tps = seq // tm
    return pl.pallas_call(
        _gmlp_kernel,
        grid=(t // tm,),
        in_specs=[
            pl.BlockSpec((tm, d), lambda i: (i, 0)),
            pl.BlockSpec((1, 6, d), lambda i: (i // tps, 0, 0)),
            _resident((d, 2 * width)),
            _resident((1, width)),
            _resident((1, width)),
            _resident((ng, c, c)),
            _resident((c, ng)),
        ],
        out_specs=pl.BlockSpec((tm, width), lambda i: (i, 0)),
        out_shape=jax.ShapeDtypeStruct((t, width), BF16),
        compiler_params=_params("arbitrary"),
        name="gmlp",
    )(x, mod, w_in.astype(BF16), ln_g.reshape(1, width), ln_b.reshape(1, width), w_s, b_s.T)


def _sb_in_kernel(x_ref, mod_ref, w_ref, q_ref, k_ref, v_ref, *, q_scale):
    d = x_ref.shape[1]
    h = _modulate(x_ref[...], mod_ref, 0, 1)
    q_ref[...] = (_nn(h, w_ref[:, 0:d]) * q_scale).astype(BF16)
    k_ref[...] = _nn(h, w_ref[:, d:2 * d]).astype(BF16)
    v_ref[...] = _nn(h, w_ref[:, 2 * d:3 * d]).astype(BF16)


def _sb_in(x, mod, w_in, *, seq, tm=512):
    t, d = x.shape
    tps = seq // tm
    row = lambda i: (i, 0)
    return pl.pallas_call(
        functools.partial(_sb_in_kernel, q_scale=(d // SB_HEADS) ** -0.5 * math.log2(math.e)),
        grid=(t // tm,),
        in_specs=[
            pl.BlockSpec((tm, d), row),
            pl.BlockSpec((1, 6, d), lambda i: (i // tps, 0, 0)),
            pl.BlockSpec((d, 3 * d), lambda i: (0, 0)),
        ],
        out_specs=[pl.BlockSpec((tm, d), row)] * 3,
        out_shape=[jax.ShapeDtypeStruct((t, d), BF16)] * 3,
        compiler_params=_params("arbitrary"),
        name="sb_in",
    )(x, mod, w_in.astype(BF16))


def _sb_core_kernel(q_ref, k_ref, v_ref, o_ref, qh_ref, sp_ref, zc_ref, carry_ref, acc_ref, knorm_ref,
                    bound_ref):
    tq, width = q_ref.shape
    seq = k_ref.shape[0]
    n_heads = qh_ref.shape[0]
    dh = LANES // 2
    qi = pl.program_id(2)
    lane = lax.broadcasted_iota(jnp.int32, (tq, LANES), 1)
    half_lanes = (lane < dh, lane >= dh)
    group = [slice((hh // 2) * LANES, (hh // 2 + 1) * LANES) for hh in range(n_heads)]
    row = lax.broadcasted_iota(jnp.int32, (tq, tq), 0)
    col = lax.broadcasted_iota(jnp.int32, (tq, tq), 1)
    ones_lower = jnp.where(row >= col, 1.0, 0.0).astype(BF16)

    @pl.when(qi == 0)
    def _():
        def key_tile(t, best):
            kf = k_ref[pl.ds(pl.multiple_of(t * tq, tq), tq), :].astype(F32)
            return jnp.maximum(best, jnp.max(kf * kf, axis=0, keepdims=True))

        col_max = lax.fori_loop(0, seq // tq, key_tile, jnp.zeros((1, width), F32))
        for hh in range(n_heads):
            bound_sq = jnp.sum(jnp.where(half_lanes[hh % 2][0:1, :], col_max[:, group[hh]], 0.0),
                               axis=-1, keepdims=True)
            knorm_ref[hh] = jnp.broadcast_to(bound_sq, knorm_ref.shape[1:])

    for hh in range(n_heads):
        q = q_ref[:, group[hh]]
        qf = q.astype(F32)
        qh_ref[hh] = jnp.where(half_lanes[hh % 2], q, jnp.zeros_like(q))
        qsq = jnp.sum(jnp.where(half_lanes[hh % 2], qf * qf, 0.0), axis=-1, keepdims=True)
        bound_ref[hh] = jnp.sqrt(qsq * knorm_ref[hh, 0:1, 0:1]) * NORM_BOUND_SLACK
    acc_ref[...] = jnp.zeros(acc_ref.shape, F32)
    carry_ref[...] = jnp.zeros(carry_ref.shape, F32)

    def stage(kt, slot, diagonal=False):
        rows = pl.ds(pl.multiple_of(kt * tq, tq), tq)
        for hh in range(n_heads):
            z = _nt(qh_ref[hh], k_ref[rows, group[hh]])
            sp = jnp.maximum(z, jnp.log2(1.0 + jnp.exp2(jnp.minimum(z, EXP2_CLAMP))))
            carry = carry_ref[hh]
            zc = z - carry
            if diagonal:
                sp = jnp.where(col < row, sp, 0.0)
                zc = jnp.where(col < row, zc, -1e30)
            sp_ref[slot, hh] = sp.astype(BF16)
            zc_ref[slot, hh] = zc
            carry_ref[hh] = carry + jnp.sum(sp, axis=-1, keepdims=True)

    def consume(kt, slot):
        rows = pl.ds(pl.multiple_of(kt * tq, tq), tq)
        for hh in range(n_heads):
            inclusive = _nn(sp_ref[slot, hh], ones_lower)
            a = jnp.exp2(zc_ref[slot, hh] - inclusive)
            acc_ref[hh] += _nn(a.astype(BF16), v_ref[rows, group[hh]])

    def still_alive():
        slack = bound_ref[0] - carry_ref[0]
        for hh in range(1, n_heads):
            slack = jnp.maximum(slack, bound_ref[hh] - carry_ref[hh])
        return (jnp.max(slack) > -DEAD_EXPONENT).astype(jnp.int32)

    stage(qi, 0, diagonal=True)

    @pl.when(qi == 0)
    def _():
        consume(0, 0)

    @pl.when(qi > 0)
    def _():
        stage(qi - 1, 1)
        alive = still_alive()
        consume(qi, 0)
        n_pairs = (qi - 1) // 2

        def more(state):
            return (state[0] < n_pairs) & (state[1] > 0)

        def pair(state):
            kt = qi - 1 - 2 * state[0]
            stage(kt - 1, 0)
            consume(kt, 1)
            stage(kt - 2, 1)
            alive = still_alive()
            consume(kt - 1, 0)
            return state[0] + 1, alive

        pairs_done, alive = lax.while_loop(more, pair, (jnp.int32(0), alive))
        staged = qi - 1 - 2 * pairs_done

        @pl.when((alive == 0) | (staged == 0))
        def _():
            consume(staged, 1)

        @pl.when((alive > 0) & (staged == 1))
        def _():
            stage(0, 0)
            consume(1, 1)
            consume(0, 0)

    for hh in range(0, n_heads, 2):
        o_ref[:, group[hh]] = jnp.where(half_lanes[0], acc_ref[hh], acc_ref[hh + 1]).astype(BF16)


def _sb_core(q, k, v, *, batch, seq, tq=256, heads_per_step=8):
    t, d = q.shape
    tq = min(tq, seq)
    nq = seq // tq
    nh = heads_per_step
    width = nh // 2 * LANES
    return pl.pallas_call(
        _sb_core_kernel,
        grid=(batch, d // width, nq),
        in_specs=[
            pl.BlockSpec((tq, width), lambda b, g, i: (b * nq + i, g)),
            pl.BlockSpec((seq, width), lambda b, g, i: (b, g)),
            pl.BlockSpec((seq, width), lambda b, g, i: (b, g)),
        ],
        out_specs=pl.BlockSpec((tq, width), lambda b, g, i: (b * nq + i, g)),
        out_shape=jax.ShapeDtypeStruct((t, d), BF16),
        scratch_shapes=[
            pltpu.VMEM((nh, tq, LANES), BF16),
            pltpu.VMEM((2, nh, tq, tq), BF16),
            pltpu.VMEM((2, nh, tq, tq), F32),
            pltpu.VMEM((nh, tq, 1), F32),
            pltpu.VMEM((nh, tq, LANES), F32),
            pltpu.VMEM((nh, SUBLANES, LANES), F32),
            pltpu.VMEM((nh, tq, 1), F32),
        ],
        compiler_params=_params("arbitrary", "arbitrary", "arbitrary"),
        name="sb_core",
    )(q, k, v)


def kernel(x, c, cond_w, cond_b, ada_w, ada_b, ln_g, ln_b, ffn_up, ffn_conv_w, ffn_conv_b, ffn_down,
           gdn_w_in, gdn_conv_w, gdn_a_log, gdn_dt_bias, gdn_norm_w, gdn_w_out,
           ret_w_in, ret_w_out,
           gmlp_w_in, gmlp_ln_g, gmlp_ln_b, gmlp_w_s, gmlp_b_s, gmlp_w_out,
           sb_w_in, sb_w_out):
    batch, seq, d = x.shape
    depth = ada_w.shape[0]
    alpha = (2.0 * depth) ** 0.25
    mods = _conditioning(c, cond_w, cond_b, ada_w, ada_b).reshape(depth, batch, 6, d)
    xt = x.reshape(batch * seq, d)
    ffn_up_b = ffn_up.astype(BF16)
    ffn_down_b = ffn_down.astype(BF16)

    for i in range(depth):
        mod = mods[i]
        mixer = i % 4
        if mixer == 0:
            q, k, v, z, gb, gt = _gdn_in(xt, mod, gdn_w_in, gdn_conv_w, gdn_a_log, gdn_dt_bias, seq=seq)
            o, w_out = _gdn_core(q, k, v, z, gb, gt, gdn_norm_w, batch=batch, seq=seq), gdn_w_out
        elif mixer == 1:
            q, k, v, gate = _ret_in(xt, mod, ret_w_in, seq=seq)
            o, w_out = _ret_core(q, k, v, gate, batch=batch, seq=seq), ret_w_out
        elif mixer == 2:
            o = _gmlp(xt, mod, gmlp_w_in, gmlp_ln_g, gmlp_ln_b, gmlp_w_s, gmlp_b_s, seq=seq)
            w_out = gmlp_w_out
        else:
            q, k, v = _sb_in(xt, mod, sb_w_in, seq=seq)
            o, w_out = _sb_core(q, k, v, batch=batch, seq=seq), sb_w_out
        xt = _sublayer_tail(o, w_out, xt, mod, ffn_up_b, ffn_conv_w[i], ffn_conv_b[i], ffn_down_b,
                            ln_g, ln_b, layer=i, seq=seq, alpha=alpha)
    return xt.reshape(batch, seq, d)
```
